```python
import math
import jax, jax.numpy as jnp
from jax import lax
import numpy as np

D_MODEL = 1024
BATCH = 8
SEQ = 4096
DEPTH = 1

MIX_WIDTH = D_MODEL
ATT_WIDTH = MIX_WIDTH // 2
RWKV_WIDTH = MIX_WIDTH - ATT_WIDTH
HD_ATT = 64
H_ATT = ATT_WIDTH // HD_ATT
HD_RWKV = 64
H_RWKV = RWKV_WIDTH // HD_RWKV
Q_LORA = 256
KV_LATENT = 128
IDX_HEADS = 8
IDX_DIM = 64
TOPK_MAX = 256
Q_BLOCK = 128
NUM_BUCKETS = 32
MAX_DISTANCE = 128
DECAY_LORA = 64
AAA_LORA = 64
GATE_LORA = 160
GN_EPS = 64e-5
ATT_COLS = Q_LORA + KV_LATENT + IDX_DIM + IDX_HEADS
RWKV_COLS = 3 * RWKV_WIDTH + DECAY_LORA + AAA_LORA + GATE_LORA
IN_COLS = ATT_COLS + RWKV_COLS
D_FF = -(-8 * D_MODEL // 768) * 256
NORM_EPS = 1e-6

kernel_name = "hybrid_dsa_rwkv7_sandwich_adaln"


def rms_norm(x, gain):
    xf = x.astype(jnp.float32)
    y = xf * lax.rsqrt(jnp.mean(xf * xf, axis=-1, keepdims=True) + NORM_EPS)
    return (y * gain.astype(jnp.float32)).astype(x.dtype)


def t5_bucket(rel):
    max_exact = NUM_BUCKETS // 2
    nf = jnp.maximum(rel, 1).astype(jnp.float32)
    large = max_exact + (jnp.log(nf / max_exact) / math.log(MAX_DISTANCE / max_exact)
                         * (NUM_BUCKETS - max_exact)).astype(jnp.int32)
    large = jnp.minimum(large, NUM_BUCKETS - 1)
    return jnp.where(rel < max_exact, rel, large)


def dsa_attention(c_q, c_kv, idx_k_raw, idx_w_raw, q_norm, w_uq, w_idx_q, kv_norm, idx_k_norm, w_uk, w_uv, rel_bias):
    B, T, _ = c_q.shape
    topk = min(TOPK_MAX, T // 4)
    cq = rms_norm(c_q, q_norm)
    q = jnp.einsum('btr,rhd->bthd', cq, w_uq)
    q_abs = jnp.einsum('bthd,hcd->bthc', q, w_uk) * (HD_ATT ** -0.5)
    ckv = rms_norm(c_kv, kv_norm)
    iq = jnp.einsum('btr,rhd->bthd', cq, w_idx_q).astype(jnp.float32)
    ik = rms_norm(idx_k_raw, idx_k_norm).astype(jnp.float32)
    iw = idx_w_raw.astype(jnp.float32) * (IDX_HEADS ** -0.5 * IDX_DIM ** -0.5)
    s_pos = jnp.arange(T)

    def block(i):
        t0 = i * Q_BLOCK
        qi = lax.dynamic_slice_in_dim(iq, t0, Q_BLOCK, axis=1)
        wi = lax.dynamic_slice_in_dim(iw, t0, Q_BLOCK, axis=1)
        qa = lax.dynamic_slice_in_dim(q_abs, t0, Q_BLOCK, axis=1)
        t_pos = t0 + jnp.arange(Q_BLOCK)
        score = jnp.einsum('bth,bths->bts', wi, jax.nn.relu(jnp.einsum('bthd,bsd->bths', qi, ik)))
        causal = s_pos[None, :] <= t_pos[:, None]
        score = jnp.where(causal[None], score, -jnp.inf)
        _, sel = lax.top_k(score, topk)
        kv_sel = jax.vmap(lambda m, j: m[j])(ckv, sel)
        rel = t_pos[None, :, None] - sel
        bias = rel_bias[t5_bucket(jnp.maximum(rel, 0))]
        logits = (jnp.einsum('bthc,btkc->bthk', qa, kv_sel).astype(jnp.float32)
                  + jnp.moveaxis(bias, -1, 2).astype(jnp.float32))
        logits = jnp.where((rel >= 0)[:, :, None, :], logits, -jnp.inf)
        p = jax.nn.softmax(logits, axis=-1).astype(kv_sel.dtype)
        return jnp.einsum('bthk,btkc->bthc', p, kv_sel)

    o = lax.map(block, jnp.arange(T // Q_BLOCK))
    o = jnp.moveaxis(o, 0, 1).reshape(B, T, H_ATT, KV_LATENT)
    return jnp.einsum('bthc,hcd->bthd', o, w_uv).reshape(B, T, ATT_WIDTH)


def rwkv7_scan(r, w, k, v, a_vec, b_vec):
    B, T, H, N = r.shape

    def step(S, inp):
        r_t, w_t, k_t, v_t, a_t, b_t = inp
        sa = jnp.einsum('bhij,bhj->bhi', S, a_t)
        S = S * w_t[:, :, None, :] + sa[..., None] * b_t[:, :, None, :] + v_t[..., None] * k_t[:, :, None, :]
        return S, jnp.einsum('bhij,bhj->bhi', S, r_t)

    xs = tuple(jnp.moveaxis(z, 1, 0) for z in (r, w, k, v, a_vec, b_vec))
    _, y = lax.scan(step, jnp.zeros((B, H, N, N), jnp.float32), xs)
    return jnp.moveaxis(y, 0, 1)


def rwkv7_group(p, mu_shift, w0, w_decay_up, a0, w_aaa_up, w_gate_up, k_k, k_a, r_k, ln_x_gain, ln_x_bias):
    B, T, _ = p.shape
    dt = p.dtype
    p_prev = jnp.pad(p, ((0, 0), (1, 0), (0, 0)))[:, :T]
    p = p + mu_shift * (p_prev - p)
    splits = [RWKV_WIDTH, 2 * RWKV_WIDTH, 3 * RWKV_WIDTH, 3 * RWKV_WIDTH + DECAY_LORA, 3 * RWKV_WIDTH + DECAY_LORA + AAA_LORA]
    r, k, v, wd, ad, gd = jnp.split(p, splits, axis=-1)
    w_log = -jax.nn.softplus(-(w0 + jnp.tanh(wd) @ w_decay_up)) - 0.5
    decay = jnp.exp(-jnp.exp(w_log.astype(jnp.float32)))
    a = jax.nn.sigmoid(a0 + ad @ w_aaa_up)
    g = jax.nn.sigmoid(gd) @ w_gate_up
    hs = (B, T, H_RWKV, HD_RWKV)
    kk = (k * k_k).astype(jnp.float32).reshape(hs)
    kk = kk / jnp.maximum(jnp.linalg.norm(kk, axis=-1, keepdims=True), 1e-12)
    k = k * (1 + (a - 1) * k_a)
    rf = r.astype(jnp.float32).reshape(hs)
    kf = k.astype(jnp.float32).reshape(hs)
    vf = v.astype(jnp.float32).reshape(hs)
    af = a.astype(jnp.float32).reshape(hs)
    y = rwkv7_scan(rf, decay.reshape(hs), kf, vf, -kk, kk * af)
    mean = jnp.mean(y, axis=-1, keepdims=True)
    var = jnp.mean(jnp.square(y - mean), axis=-1, keepdims=True)
    y = ((y - mean) * lax.rsqrt(var + GN_EPS)).reshape(B, T, RWKV_WIDTH)
    y = y * ln_x_gain.astype(jnp.float32) + ln_x_bias.astype(jnp.float32)
    bonus = jnp.sum(rf * kf * r_k.astype(jnp.float32), axis=-1, keepdims=True) * vf
    y = (y + bonus.reshape(B, T, RWKV_WIDTH)) * g.astype(jnp.float32)
    return y.astype(dt)


def hybrid_layer(x, c, rel_bias, ada_w, ada_b, mix_pre_norm, mix_post_norm, ffn_pre_norm, ffn_post_norm,
                 w_in, q_norm, w_uq, w_idx_q, kv_norm, idx_k_norm, w_uk, w_uv,
                 mu_shift, w0, w_decay_up, a0, w_aaa_up, w_gate_up, k_k, k_a, r_k, ln_x_gain, ln_x_bias,
                 w_out, w_ffn_gate, w_ffn_up, w_ffn_down):
    mod = jax.nn.silu(c) @ ada_w + ada_b
    sh_m, sc_m, g_m, sh_f, sc_f, g_f = [m[:, None, :] for m in jnp.split(mod, 6, axis=-1)]
    h = rms_norm(x, mix_pre_norm) * (1 + sc_m) + sh_m
    proj = h @ w_in
    att_p, rwkv_p = proj[..., :ATT_COLS], proj[..., ATT_COLS:]
    c_q, c_kv, idx_k_raw, idx_w_raw = jnp.split(att_p, [Q_LORA, Q_LORA + KV_LATENT, Q_LORA + KV_LATENT + IDX_DIM], axis=-1)
    y_att = dsa_attention(c_q, c_kv, idx_k_raw, idx_w_raw, q_norm, w_uq, w_idx_q, kv_norm, idx_k_norm, w_uk, w_uv, rel_bias)
    y_rwkv = rwkv7_group(rwkv_p, mu_shift, w0, w_decay_up, a0, w_aaa_up, w_gate_up, k_k, k_a, r_k, ln_x_gain, ln_x_bias)
    mix = jnp.concatenate([y_att, y_rwkv], axis=-1) @ w_out
    x = x + g_m * rms_norm(mix, mix_post_norm)
    hf = rms_norm(x, ffn_pre_norm) * (1 + sc_f) + sh_f
    f = (jax.nn.silu(hf @ w_ffn_gate) * (hf @ w_ffn_up)) @ w_ffn_down
    return x + g_f * rms_norm(f, ffn_post_norm)


def setup_inputs(seed: int = 0) -> dict:
    key = jax.random.key(seed)
    ks = jax.random.split(key, 40)
    f32 = jnp.float32
    L = DEPTH

    def nrm(k, shape, scale):
        return jax.random.normal(k, shape, f32) * scale

    def gain(k, n):
        return 1.0 + 0.05 * jax.random.normal(k, (L, n), f32)

    return {
        "x": jax.random.normal(ks[0], (BATCH, SEQ, D_MODEL), f32),
        "c": jax.random.normal(ks[1], (BATCH, D_MODEL), f32),
        "rel_bias": nrm(ks[2], (NUM_BUCKETS, H_ATT), 0.5),
        "ada_w": nrm(ks[3], (L, D_MODEL, 6 * D_MODEL), 0.5 * D_MODEL ** -0.5),
        "ada_b": nrm(ks[4], (L, 6 * D_MODEL), 0.02),
        "mix_pre_norm": gain(ks[5], D_MODEL),
        "mix_post_norm": gain(ks[6], D_MODEL),
        "ffn_pre_norm": gain(ks[7], D_MODEL),
        "ffn_post_norm": gain(ks[8], D_MODEL),
        "w_in": nrm(ks[9], (L, D_MODEL, IN_COLS), D_MODEL ** -0.5),
        "q_norm": gain(ks[10], Q_LORA),
        "w_uq": nrm(ks[11], (L, Q_LORA, H_ATT, HD_ATT), Q_LORA ** -0.5),
        "w_idx_q": nrm(ks[12], (L, Q_LORA, IDX_HEADS, IDX_DIM), Q_LORA ** -0.5),
        "kv_norm": gain(ks[13], KV_LATENT),
        "idx_k_norm": gain(ks[14], IDX_DIM),
        "w_uk": nrm(ks[15], (L, H_ATT, KV_LATENT, HD_ATT), KV_LATENT ** -0.5),
        "w_uv": nrm(ks[16], (L, H_ATT, KV_LATENT, HD_ATT), KV_LATENT ** -0.5),
        "mu_shift": jax.random.uniform(ks[17], (L, RWKV_COLS), f32),
        "w0": nrm(ks[18], (L, RWKV_WIDTH), 0.5),
        "w_decay_up": nrm(ks[19], (L, DECAY_LORA, RWKV_WIDTH), 0.1 * DECAY_LORA ** -0.5),
        "a0": nrm(ks[20], (L, RWKV_WIDTH), 0.1),
        "w_aaa_up": nrm(ks[21], (L, AAA_LORA, RWKV_WIDTH), 0.5 * AAA_LORA ** -0.5),
        "w_gate_up": nrm(ks[22], (L, GATE_LORA, RWKV_WIDTH), GATE_LORA ** -0.5),
        "k_k": 0.85 + 0.05 * jax.random.normal(ks[23], (L, RWKV_WIDTH), f32),
        "k_a": 1.0 + 0.05 * jax.random.normal(ks[24], (L, RWKV_WIDTH), f32),
        "r_k": nrm(ks[25], (L, H_RWKV, HD_RWKV), 0.1),
        "ln_x_gain": gain(ks[26], RWKV_WIDTH),
        "ln_x_bias": nrm(ks[27], (L, RWKV_WIDTH), 0.02),
        "w_out": nrm(ks[28], (L, MIX_WIDTH, D_MODEL), MIX_WIDTH ** -0.5),
        "w_ffn_gate": nrm(ks[29], (L, D_MODEL, D_FF), D_MODEL ** -0.5),
        "w_ffn_up": nrm(ks[30], (L, D_MODEL, D_FF), D_MODEL ** -0.5),
        "w_ffn_down": nrm(ks[31], (L, D_FF, D_MODEL), D_FF ** -0.5),
    }


def reference(x, c, rel_bias, ada_w, ada_b, mix_pre_norm, mix_post_norm, ffn_pre_norm, ffn_post_norm,
              w_in, q_norm, w_uq, w_idx_q, kv_norm, idx_k_norm, w_uk, w_uv,
              mu_shift, w0, w_decay_up, a0, w_aaa_up, w_gate_up, k_k, k_a, r_k, ln_x_gain, ln_x_bias,
              w_out, w_ffn_gate, w_ffn_up, w_ffn_down):
    layer_params = (ada_w, ada_b, mix_pre_norm, mix_post_norm, ffn_pre_norm, ffn_post_norm,
                    w_in, q_norm, w_uq, w_idx_q, kv_norm, idx_k_norm, w_uk, w_uv,
                    mu_shift, w0, w_decay_up, a0, w_aaa_up, w_gate_up, k_k, k_a, r_k, ln_x_gain, ln_x_bias,
                    w_out, w_ffn_gate, w_ffn_up, w_ffn_down)
    for l in range(DEPTH):
        x = hybrid_layer(x, c, rel_bias, *[p[l] for p in layer_params])
    return x
```

```python
import functools
import math

import numpy as np
import jax
import jax.numpy as jnp
from jax import lax
from jax.experimental import pallas as pl
from jax.experimental.pallas import tpu as pltpu

F32 = jnp.float32
BF16 = jnp.bfloat16

D_MODEL = 1024
ATT_WIDTH = 512
RWKV_WIDTH = 512
HD_ATT = 64
H_ATT = 8
HD_RWKV = 64
H_RWKV = 8
Q_LORA = 256
KV_LATENT = 128
IDX_HEADS = 8
IDX_DIM = 64
TOPK_MAX = 256
NUM_BUCKETS = 32
MAX_DISTANCE = 128
DECAY_LORA = 64
AAA_LORA = 64
GATE_LORA = 160
GN_EPS = 64e-5
D_FF = 2816
NORM_EPS = 1e-6

ATT_PAD = 512
GATE_PAD = 256
RW_COLS = 3 * RWKV_WIDTH + DECAY_LORA + AAA_LORA + GATE_PAD
OFF_WD = 3 * RWKV_WIDTH
OFF_AD = OFF_WD + DECAY_LORA
OFF_GD = OFF_AD + AAA_LORA

TQ = 128
CHUNK = 64
NEG_BIG = -1e30
LOWEST = -3.0e38
VMEM_LIMIT = 56 * 1024 * 1024


def _cp(sem):
    return pltpu.CompilerParams(dimension_semantics=sem, vmem_limit_bytes=VMEM_LIMIT)


def _dot(a, b):
    return jnp.dot(a, b, preferred_element_type=F32)


def _dot_nt(a, b):
    return lax.dot_general(a, b, (((1,), (1,)), ((), ())), preferred_element_type=F32)


def _dot_tn(a, b):
    return lax.dot_general(a, b, (((0,), (0,)), ((), ())), preferred_element_type=F32)


def _rms(z, gain):
    return z * lax.rsqrt(jnp.mean(z * z, axis=-1, keepdims=True) + NORM_EPS) * gain


def _mod_kernel(c_ref, w_ref, b_ref, o_ref):
    c = c_ref[...]
    s = c * jax.nn.sigmoid(c)
    o_ref[...] = _dot(s.astype(BF16), w_ref[...].astype(BF16)) + b_ref[...]


def _mod_call(c, ada_w, ada_b):
    B, D = c.shape
    N = ada_w.shape[1]
    TN = 1536
    return pl.pallas_call(
        _mod_kernel,
        grid=(N // TN,),
        in_specs=[pl.BlockSpec((B, D), lambda j: (0, 0)),
                  pl.BlockSpec((D, TN), lambda j: (0, j)),
                  pl.BlockSpec((1, TN), lambda j: (0, j))],
        out_specs=pl.BlockSpec((B, TN), lambda j: (0, j)),
        out_shape=jax.ShapeDtypeStruct((B, N), F32),
        compiler_params=_cp(("parallel",)),
        name="mod",
    )(c, ada_w, ada_b.reshape(1, N))


def _inproj_kernel(x_ref, mod_ref, g_ref, w_ref, att_ref, rw_ref):
    x = x_ref[0]
    y = _rms(x, g_ref[...])
    sh = mod_ref[0, 0:1, :]
    sc = mod_ref[0, 1:2, :]
    h = (y * (1.0 + sc) + sh).astype(BF16)
    att_ref[0] = _dot(h, w_ref[:, 0:ATT_PAD])
    for n0 in range(0, RW_COLS, 384):
        rw_ref[0, :, n0:n0 + 384] = _dot(h, w_ref[:, ATT_PAD + n0:ATT_PAD + n0 + 384])


def _inproj_call(x, mod3, gain, w_in_p):
    B, T, D = x.shape
    TM = 512
    NP = w_in_p.shape[1]
    return pl.pallas_call(
        _inproj_kernel,
        grid=(B, T // TM),
        in_specs=[pl.BlockSpec((1, TM, D), lambda b, i: (b, i, 0)),
                  pl.BlockSpec((1, 6, D), lambda b, i: (b, 0, 0)),
                  pl.BlockSpec((1, D), lambda b, i: (0, 0)),
                  pl.BlockSpec((D, NP), lambda b, i: (0, 0))],
        out_specs=[pl.BlockSpec((1, TM, ATT_PAD), lambda b, i: (b, i, 0)),
                   pl.BlockSpec((1, TM, RW_COLS), lambda b, i: (b, i, 0))],
        out_shape=[jax.ShapeDtypeStruct((B, T, ATT_PAD), F32),
                   jax.ShapeDtypeStruct((B, T, RW_COLS), F32)],
        compiler_params=_cp(("parallel", "parallel")),
        name="inproj",
    )(x, mod3, gain, w_in_p)


def _dsaprep_kernel(att_ref, qn_ref, kvn_ref, ikn_ref, wuq_ref, wiq_ref, wukt_ref,
                    qabs_ref, iq_ref, ik_ref, ckv_ref, iw_ref):
    att = att_ref[0]
    o1 = Q_LORA
    o2 = o1 + KV_LATENT
    o3 = o2 + IDX_DIM
    cq = _rms(att[:, 0:o1], qn_ref[...]).astype(BF16)
    ckv_ref[0] = _rms(att[:, o1:o2], kvn_ref[...]).astype(BF16)
    ik_ref[0] = _rms(att[:, o2:o3], ikn_ref[...]).astype(BF16)
    iw_ref[0] = att[:, o3:o3 + IDX_HEADS] * (IDX_HEADS ** -0.5 * IDX_DIM ** -0.5)
    q = _dot(cq, wuq_ref[...])
    iqv = _dot(cq, wiq_ref[...])
    for h in range(H_ATT):
        qh = q[:, h * HD_ATT:(h + 1) * HD_ATT].astype(BF16)
        qabs_ref[0, h] = (_dot(qh, wukt_ref[h]) * (HD_ATT ** -0.5)).astype(BF16)
        iq_ref[0, h] = iqv[:, h * IDX_DIM:(h + 1) * IDX_DIM].astype(BF16)


def _dsaprep_call(att, q_norm, kv_norm, idx_k_norm, wuq, wiq, wukt):
    B, T, _ = att.shape
    TM = 512
    const2 = lambda b, i: (0, 0)
    return pl.pallas_call(
        _dsaprep_kernel,
        grid=(B, T // TM),
        in_specs=[pl.BlockSpec((1, TM, ATT_PAD), lambda b, i: (b, i, 0)),
                  pl.BlockSpec((1, Q_LORA), const2),
                  pl.BlockSpec((1, KV_LATENT), const2),
                  pl.BlockSpec((1, IDX_DIM), const2),
                  pl.BlockSpec((Q_LORA, ATT_WIDTH), const2),
                  pl.BlockSpec((Q_LORA, IDX_HEADS * IDX_DIM), const2),
                  pl.BlockSpec((H_ATT, HD_ATT, KV_LATENT), lambda b, i: (0, 0, 0))],
        out_specs=[pl.BlockSpec((1, H_ATT, TM, KV_LATENT), lambda b, i: (b, 0, i, 0)),
                   pl.BlockSpec((1, IDX_HEADS, TM, IDX_DIM), lambda b, i: (b, 0, i, 0)),
                   pl.BlockSpec((1, TM, IDX_DIM), lambda b, i: (b, i, 0)),
                   pl.BlockSpec((1, TM, KV_LATENT), lambda b, i: (b, i, 0)),
                   pl.BlockSpec((1, TM, IDX_HEADS), lambda b, i: (b, i, 0))],
        out_shape=[jax.ShapeDtypeStruct((B, H_ATT, T, KV_LATENT), BF16),
                   jax.ShapeDtypeStruct((B, IDX_HEADS, T, IDX_DIM), BF16),
                   jax.ShapeDtypeStruct((B, T, IDX_DIM), BF16),
                   jax.ShapeDtypeStruct((B, T, KV_LATENT), BF16),
                   jax.ShapeDtypeStruct((B, T, IDX_HEADS), F32)],
        compiler_params=_cp(("parallel", "parallel")),
        name="dsaprep",
    )(att, q_norm, kv_norm, idx_k_norm, wuq, wiq, wukt)


def _t5_bucket_np(rel):
    max_exact = NUM_BUCKETS // 2
    nf = np.maximum(rel, 1).astype(np.float32)
    large = max_exact + (np.log(nf / max_exact) / math.log(MAX_DISTANCE / max_exact)
                         * (NUM_BUCKETS - max_exact)).astype(np.int32)
    large = np.minimum(large, NUM_BUCKETS - 1)
    return np.where(rel < max_exact, rel, large).astype(np.int32)


def _near_bucket_tiles():
    i = np.arange(TQ)[:, None]
    s = np.arange(TQ)[None, :]
    tiles = [_t5_bucket_np(np.maximum(i - s + TQ * d, 0)) for d in range(2)]
    assert _t5_bucket_np(np.array([TQ + 1]))[0] == NUM_BUCKETS - 1
    return np.stack(tiles)


def _bias_kernel(rb_ref, bk_ref, o_ref):
    for d in range(2):
        bk = bk_ref[d]
        for h in range(H_ATT):
            def body(b, acc):
                return jnp.where(bk == b, rb_ref[b, h], acc)
            acc = lax.fori_loop(0, NUM_BUCKETS, body, jnp.zeros((TQ, TQ), F32))
            o_ref[d, h] = acc - rb_ref[NUM_BUCKETS - 1, h]


def _bias_call(rel_bias):
    buckets = jnp.asarray(_near_bucket_tiles())
    return pl.pallas_call(
        _bias_kernel,
        in_specs=[pl.BlockSpec(memory_space=pltpu.SMEM),
                  pl.BlockSpec(memory_space=pltpu.VMEM)],
        out_specs=pl.BlockSpec(memory_space=pltpu.VMEM),
        out_shape=jax.ShapeDtypeStruct((2, H_ATT, TQ, TQ), F32),
        name="bias",
    )(rel_bias, buckets)


def _dsa_kernel(iq_ref, iw_ref, qabs_ref, ik_ref, ckv_ref, bias_ref, wuv_ref, y_ref,
                score_scr, wb_scr, m_scr, l_scr, alpha_scr, p_scr, acc_scr, *, topk):
    i = pl.program_id(1)
    kf = float(topk)
    row = lax.broadcasted_iota(jnp.int32, (TQ, TQ), 0)
    col = lax.broadcasted_iota(jnp.int32, (TQ, TQ), 1)
    causal = col <= row

    iw = iw_ref[0]
    for h in range(IDX_HEADS):
        wb_scr[h] = jnp.broadcast_to(iw[:, h:h + 1], (TQ, TQ))
    iq2 = iq_ref[0].reshape(IDX_HEADS * TQ, IDX_DIM)

    def block_score(j):
        kj = ik_ref[0, pl.ds(pl.multiple_of(j * TQ, TQ), TQ), :]
        s = _dot_nt(iq2, kj)
        acc = jnp.zeros((TQ, TQ), F32)
        for h in range(IDX_HEADS):
            acc = acc + jnp.maximum(s[h * TQ:(h + 1) * TQ], 0.0) * wb_scr[h]
        return acc

    def p1_body(j, carry):
        rmax, rmin = carry
        sc = block_score(j)
        score_scr[j] = sc
        return jnp.maximum(rmax, sc), jnp.minimum(rmin, sc)

    rmax, rmin = lax.fori_loop(0, i, p1_body,
                               (jnp.full((TQ, TQ), -jnp.inf, F32), jnp.full((TQ, TQ), jnp.inf, F32)))
    sc = block_score(i)
    score_scr[i] = jnp.where(causal, sc, -jnp.inf)
    rmax = jnp.maximum(rmax, jnp.where(causal, sc, -jnp.inf))
    rmin = jnp.minimum(rmin, jnp.where(causal, sc, jnp.inf))
    hi0 = jnp.max(rmax, axis=-1, keepdims=True)
    lo0 = jnp.min(rmin, axis=-1, keepdims=True)

    nvalid = (i * TQ + lax.broadcasted_iota(jnp.int32, (TQ, 1), 0) + 1).astype(F32)
    needs_thr = nvalid > kf

    def count(pred_fn):
        def body(j, acc):
            return acc + jnp.where(pred_fn(score_scr[j]), 1.0, 0.0)
        acc = lax.fori_loop(0, i + 1, body, jnp.zeros((TQ, TQ), F32))
        return jnp.sum(acc, axis=-1, keepdims=True)

    def bisect(_, st):
        lo, hi, cnt_lo = st
        mid = 0.5 * lo + 0.5 * hi
        midb = jnp.broadcast_to(mid, (TQ, TQ))
        c = count(lambda s: s >= midb)
        ge = c >= kf
        return jnp.where(ge, mid, lo), jnp.where(ge, hi, mid), jnp.where(ge, c, cnt_lo)

    def refine(st):
        lo, hi, cnt_lo, _, _, _, it = st
        lo, hi, cnt_lo = lax.fori_loop(0, 8, bisect, (lo, hi, cnt_lo))
        lob = jnp.broadcast_to(lo, (TQ, TQ))

        def min_body(j, acc):
            s = score_scr[j]
            return jnp.minimum(acc, jnp.where(s >= lob, s, jnp.inf))
        thr = jnp.min(lax.fori_loop(0, i + 1, min_body, jnp.full((TQ, TQ), jnp.inf, F32)),
                      axis=-1, keepdims=True)
        thrb = jnp.broadcast_to(thr, (TQ, TQ))
        cgt = count(lambda s: s > thrb)
        unresolved = jnp.sum(jnp.where(needs_thr & (cgt >= kf), 1.0, 0.0))
        return lo, hi, cnt_lo, thr, cgt, unresolved, it + 1

    def refine_cond(st):
        return jnp.logical_and(st[5] > 0.0, st[6] < 40)

    st0 = (lo0, hi0, nvalid, lo0, nvalid, jnp.float32(1.0), jnp.int32(0))
    _, _, cnt_lo, thr, cgt, _, _ = lax.while_loop(refine_cond, refine, st0)

    tie = needs_thr & (cnt_lo > kf)
    n_tie = jnp.sum(jnp.where(tie, 1.0, 0.0))

    @pl.when(n_tie > 0.0)
    def _():
        upper = jnp.where(row <= col, 1.0, 0.0).astype(BF16)
        thrb = jnp.broadcast_to(thr, (TQ, TQ))
        tieb = jnp.broadcast_to(tie, (TQ, TQ))
        needb = jnp.broadcast_to(kf - cgt, (TQ, TQ))

        def body(j, carry):
            s = score_scr[j]
            eq = (s == thrb) & tieb
            pref = _dot(jnp.where(eq, 1.0, 0.0).astype(BF16), upper) + carry
            score_scr[j] = jnp.where(eq & (pref > needb), -jnp.inf, s)
            return jnp.broadcast_to(pref[:, TQ - 1:TQ], (TQ, TQ))
        lax.fori_loop(0, i + 1, body, jnp.zeros((TQ, TQ), F32))

    thr_eff = jnp.broadcast_to(jnp.where(needs_thr, thr, LOWEST), (TQ, TQ))

    m_scr[...] = jnp.full(m_scr.shape, NEG_BIG, F32)
    l_scr[...] = jnp.zeros(l_scr.shape, F32)
    acc_scr[...] = jnp.zeros(acc_scr.shape, F32)
    qa2 = qabs_ref[0].reshape(H_ATT * TQ, KV_LATENT)

    def attend(j, near):
        kv = ckv_ref[0, pl.ds(pl.multiple_of(j * TQ, TQ), TQ), :]
        logits = _dot_nt(qa2, kv)
        sel = score_scr[j] >= thr_eff
        for h in range(H_ATT):
            lg = logits[h * TQ:(h + 1) * TQ]
            if near is not None:
                lg = lg + bias_ref[near, h]
            lg = jnp.where(sel, lg, NEG_BIG)
            m_prev = m_scr[h]
            m_new = jnp.maximum(m_prev, jnp.max(lg, axis=-1, keepdims=True))
            alpha = jnp.exp(m_prev - m_new)
            p = jnp.exp(lg - m_new)
            l_scr[h] = alpha * l_scr[h] + jnp.sum(p, axis=-1, keepdims=True)
            m_scr[h] = m_new
            alpha_scr[h] = alpha
            p_scr[h * TQ:(h + 1) * TQ, :] = p.astype(BF16)
        pv = _dot(p_scr[...], kv)
        for h in range(H_ATT):
            sl = slice(h * TQ, (h + 1) * TQ)
            acc_scr[sl, :] = alpha_scr[h] * acc_scr[sl, :] + pv[sl]

    def far_body(j, c):
        attend(j, None)
        return c
    lax.fori_loop(0, jnp.maximum(i - 1, 0), far_body, 0)

    @pl.when(i >= 1)
    def _():
        attend(i - 1, 1)

    attend(i, 0)

    y = jnp.zeros((TQ, ATT_WIDTH), F32)
    for h in range(H_ATT):
        o = acc_scr[h * TQ:(h + 1) * TQ, :] / l_scr[h]
        y = y + _dot(o.astype(BF16), wuv_ref[h])
    y_ref[0] = y


def _dsa_call(iq, iw, qabs, ik, ckv, bias_tiles, wuv_pad, topk):
    B, _, T, _ = iq.shape
    NT = T // TQ
    return pl.pallas_call(
        functools.partial(_dsa_kernel, topk=topk),
        grid=(B, NT),
        in_specs=[pl.BlockSpec((1, IDX_HEADS, TQ, IDX_DIM), lambda b, i: (b, 0, i, 0)),
                  pl.BlockSpec((1, TQ, IDX_HEADS), lambda b, i: (b, i, 0)),
                  pl.BlockSpec((1, H_ATT, TQ, KV_LATENT), lambda b, i: (b, 0, i, 0)),
                  pl.BlockSpec((1, T, IDX_DIM), lambda b, i: (b, 0, 0)),
                  pl.BlockSpec((1, T, KV_LATENT), lambda b, i: (b, 0, 0)),
                  pl.BlockSpec((2, H_ATT, TQ, TQ), lambda b, i: (0, 0, 0, 0)),
                  pl.BlockSpec((H_ATT, KV_LATENT, ATT_WIDTH), lambda b, i: (0, 0, 0))],
        out_specs=pl.BlockSpec((1, TQ, ATT_WIDTH), lambda b, i: (b, i, 0)),
        out_shape=jax.ShapeDtypeStruct((B, T, ATT_WIDTH), F32),
        scratch_shapes=[pltpu.VMEM((NT, TQ, TQ), F32),
                        pltpu.VMEM((IDX_HEADS, TQ, TQ), F32),
                        pltpu.VMEM((H_ATT, TQ, TQ), F32),
                        pltpu.VMEM((H_ATT, TQ, TQ), F32),
                        pltpu.VMEM((H_ATT, TQ, TQ), F32),
                        pltpu.VMEM((H_ATT * TQ, TQ), BF16),
                        pltpu.VMEM((H_ATT * TQ, KV_LATENT), F32)],
        compiler_params=_cp(("parallel", "arbitrary")),
        name="dsa",
    )(iq, iw, qabs, ik, ckv, bias_tiles, wuv_pad)


def _rwkv_kernel(p_ref, prev_ref, mu_ref, w0_ref, wdec_ref, a0_ref, waaa_ref, wgate_ref,
                 kk_ref, ka_ref, rk_ref, lng_ref, lnb_ref, y_ref, h_scr):
    c = pl.program_id(1)
    C = CHUNK
    N = HD_RWKV

    @pl.when(c == 0)
    def _():
        h_scr[...] = jnp.zeros(h_scr.shape, F32)

    p = p_ref[0]
    prev_row = jnp.where(c > 0, prev_ref[0, 7:8, :], 0.0)
    rid = lax.broadcasted_iota(jnp.int32, (C, 1), 0)
    p_prev = jnp.where(rid == 0, prev_row, pltpu.roll(p, 1, 0))
    p = p + mu_ref[...] * (p_prev - p)

    W = RWKV_WIDTH
    r = p[:, 0:W]
    k = p[:, W:2 * W]
    v = p[:, 2 * W:3 * W]
    wd = p[:, OFF_WD:OFF_WD + DECAY_LORA]
    ad = p[:, OFF_AD:OFF_AD + AAA_LORA]
    gd = p[:, OFF_GD:OFF_GD + GATE_PAD]

    w_log = -jax.nn.softplus(-(w0_ref[...] + _dot(jnp.tanh(wd).astype(BF16), wdec_ref[...]))) - 0.5
    logw = -jnp.exp(w_log)
    a = jax.nn.sigmoid(a0_ref[...] + _dot(ad.astype(BF16), waaa_ref[...]))
    g = _dot(jax.nn.sigmoid(gd).astype(BF16), wgate_ref[...])

    r64 = lax.broadcasted_iota(jnp.int32, (C, C), 0)
    c64 = lax.broadcasted_iota(jnp.int32, (C, C), 1)
    tril_incl = r64 >= c64
    tril_strict = r64 > c64
    eye = r64 == c64
    cum = jnp.dot(jnp.where(tril_incl, 1.0, 0.0), logw, preferred_element_type=F32,
                  precision=lax.Precision.HIGHEST)
    cum_last = cum[C - 1:C, :]
    g_in = jnp.exp(cum - logw)
    g_t = jnp.exp(cum)
    g_inv = jnp.exp(-cum)
    g_end = jnp.exp(cum_last - cum)
    g_all = jnp.exp(cum_last)

    kkf = k * kk_ref[...]
    kmod = k * (1.0 + (a - 1.0) * ka_ref[...])
    rkr = r * kmod * rk_ref[...]

    for h in range(H_RWKV):
        sl = slice(h * N, (h + 1) * N)
        kk_h = kkf[:, sl]
        nrm = jnp.sqrt(jnp.sum(kk_h * kk_h, axis=-1, keepdims=True))
        kk_h = kk_h / jnp.maximum(nrm, 1e-12)
        a_h = a[:, sl]
        k_h = kmod[:, sl]
        v_h = v[:, sl]
        r_h = r[:, sl]
        bvec = kk_h * a_h
        at = (-kk_h) * g_in[:, sl]
        rt = r_h * g_t[:, sl]
        bt = bvec * g_inv[:, sl]
        kt = k_h * g_inv[:, sl]
        bh = bvec * g_end[:, sl]
        kh = k_h * g_end[:, sl]

        lhs = jnp.concatenate([at, rt], axis=0).astype(BF16)
        rhs = jnp.concatenate([bt, kt], axis=0).astype(BF16)
        g4 = _dot_nt(lhs, rhs)
        a_ab = jnp.where(tril_strict, g4[0:C, 0:C], 0.0)
        a_ak = jnp.where(tril_strict, g4[0:C, C:2 * C], 0.0)
        a_rb = jnp.where(tril_incl, g4[C:2 * C, 0:C], 0.0)
        a_rk = jnp.where(tril_incl, g4[C:2 * C, C:2 * C], 0.0)

        tinv = jnp.where(eye, 1.0, 0.0) + a_ab
        pw = a_ab
        for _ in range(5):
            pwb = pw.astype(BF16)
            pw = _dot(pwb, pwb)
            tinv = tinv + _dot(tinv.astype(BF16), pw.astype(BF16))

        hst = h_scr[h]
        hb = hst.astype(BF16)
        vb = v_h.astype(BF16)
        x = _dot(at.astype(BF16), hb) + _dot(a_ak.astype(BF16), vb)
        u = _dot(tinv.astype(BF16), x.astype(BF16))
        ub = u.astype(BF16)
        y = _dot(rt.astype(BF16), hb) + _dot(a_rb.astype(BF16), ub) + _dot(a_rk.astype(BF16), vb)

        g_col = jnp.sum(jnp.where(eye, jnp.broadcast_to(g_all[:, sl], (N, N)), 0.0), axis=1, keepdims=True)
        h_scr[h] = g_col * hst + _dot_tn(bh.astype(BF16), ub) + _dot_tn(kh.astype(BF16), vb)

        mean = jnp.mean(y, axis=-1, keepdims=True)
        yc = y - mean
        var = jnp.mean(yc * yc, axis=-1, keepdims=True)
        yn = yc * lax.rsqrt(var + GN_EPS) * lng_ref[:, sl] + lnb_ref[:, sl]
        bonus = jnp.sum(rkr[:, sl], axis=-1, keepdims=True) * v_h
        y_ref[0, :, sl] = (yn + bonus) * g[:, sl]


def _rwkv_call(rw, mu_p, w0, wdec, a0, waaa, wgate_p, k_k, k_a, r_k, lng, lnb):
    B, T, _ = rw.shape
    NC = T // CHUNK
    W = RWKV_WIDTH
    c2 = lambda b, c: (0, 0)
    rowspec = pl.BlockSpec((1, W), c2)
    return pl.pallas_call(
        _rwkv_kernel,
        grid=(B, NC),
        in_specs=[pl.BlockSpec((1, CHUNK, RW_COLS), lambda b, c: (b, c, 0)),
                  pl.BlockSpec((1, 8, RW_COLS), lambda b, c: (b, jnp.maximum(c * (CHUNK // 8) - 1, 0), 0)),
                  pl.BlockSpec((1, RW_COLS), c2),
                  rowspec,
                  pl.BlockSpec((DECAY_LORA, W), c2),
                  rowspec,
                  pl.BlockSpec((AAA_LORA, W), c2),
                  pl.BlockSpec((GATE_PAD, W), c2),
                  rowspec, rowspec, rowspec, rowspec, rowspec],
        out_specs=pl.BlockSpec((1, CHUNK, W), lambda b, c: (b, c, 0)),
        out_shape=jax.ShapeDtypeStruct((B, T, W), F32),
        scratch_shapes=[pltpu.VMEM((H_RWKV, HD_RWKV, HD_RWKV), F32)],
        compiler_params=_cp(("parallel", "arbitrary")),
        name="rwkv",
    )(rw, rw, mu_p, w0, wdec, a0, waaa, wgate_p, k_k, k_a, r_k, lng, lnb)


def _ffn_kernel(x_ref, ya_ref, yr_ref, mod_ref, gpost_ref, gpre_ref, gfpost_ref,
                wo_ref, wg_ref, wu_ref, wd_ref, o_ref, x1_scr, hf_scr, acc_scr):
    j = pl.program_id(2)

    @pl.when(j == 0)
    def _():
        mix = (_dot(ya_ref[0].astype(BF16), wo_ref[0:ATT_WIDTH, :])
               + _dot(yr_ref[0].astype(BF16), wo_ref[ATT_WIDTH:, :]))
        x1 = x_ref[0] + mod_ref[0, 2:3, :] * _rms(mix, gpost_ref[...])
        x1_scr[...] = x1
        hf = _rms(x1, gpre_ref[...]) * (1.0 + mod_ref[0, 4:5, :]) + mod_ref[0, 3:4, :]
        hf_scr[...] = hf.astype(BF16)
        acc_scr[...] = jnp.zeros(acc_scr.shape, F32)

    hf = hf_scr[...]
    gate = _dot(hf, wg_ref[...])
    up = _dot(hf, wu_ref[...])
    act = (gate * jax.nn.sigmoid(gate) * up).astype(BF16)
    acc_scr[...] += _dot(act, wd_ref[...])

    @pl.when(j == pl.num_programs(2) - 1)
    def _():
        o_ref[0] = x1_scr[...] + mod_ref[0, 5:6, :] * _rms(acc_scr[...], gfpost_ref[...])


def _ffn_call(x, y_att, y_rwkv, mod3, g_post, g_pre, g_fpost, wo, wg, wu, wd):
    B, T, D = x.shape
    TM = 512
    TF = D_FF // 2
    c3 = lambda b, i, j: (0, 0)
    return pl.pallas_call(
        _ffn_kernel,
        grid=(B, T // TM, D_FF // TF),
        in_specs=[pl.BlockSpec((1, TM, D), lambda b, i, j: (b, i, 0)),
                  pl.BlockSpec((1, TM, ATT_WIDTH), lambda b, i, j: (b, i, 0)),
                  pl.BlockSpec((1, TM, RWKV_WIDTH), lambda b, i, j: (b, i, 0)),
                  pl.BlockSpec((1, 6, D), lambda b, i, j: (b, 0, 0)),
                  pl.BlockSpec((1, D), c3), pl.BlockSpec((1, D), c3), pl.BlockSpec((1, D), c3),
                  pl.BlockSpec((D, D), c3),
                  pl.BlockSpec((D, TF), lambda b, i, j: (0, j)),
                  pl.BlockSpec((D, TF), lambda b, i, j: (0, j)),
                  pl.BlockSpec((TF, D), lambda b, i, j: (j, 0))],
        out_specs=pl.BlockSpec((1, TM, D), lambda b, i, j: (b, i, 0)),
        out_shape=jax.ShapeDtypeStruct((B, T, D), F32),
        scratch_shapes=[pltpu.VMEM((TM, D), F32), pltpu.VMEM((TM, D), BF16), pltpu.VMEM((TM, D), F32)],
        compiler_params=_cp(("parallel", "parallel", "arbitrary")),
        name="ffn",
    )(x, y_att, y_rwkv, mod3, g_post, g_pre, g_fpost, wo, wg, wu, wd)


def _pad_cols(w, n):
    return jnp.pad(w, ((0, 0), (0, n - w.shape[1])))


def _layer(x, mod3, rel_bias, mix_pre_norm, mix_post_norm, ffn_pre_norm, ffn_post_norm,
           w_in, q_norm, w_uq, w_idx_q, kv_norm, idx_k_norm, w_uk, w_uv,
           mu_shift, w0, w_decay_up, a0, w_aaa_up, w_gate_up, k_k, k_a, r_k, ln_x_gain, ln_x_bias,
           w_out, w_ffn_gate, w_ffn_up, w_ffn_down):
    B, T, D = x.shape
    att_cols = Q_LORA + KV_LATENT + IDX_DIM + IDX_HEADS
    n_main = 3 * RWKV_WIDTH + DECAY_LORA + AAA_LORA
    w_att = _pad_cols(w_in[:, :att_cols], ATT_PAD)
    w_rw = w_in[:, att_cols:]
    w_in_p = jnp.concatenate([w_att, w_rw[:, :n_main], _pad_cols(w_rw[:, n_main:], GATE_PAD)], axis=1).astype(BF16)
    mu_p = jnp.concatenate([mu_shift[:n_main], jnp.pad(mu_shift[n_main:], (0, GATE_PAD - GATE_LORA))]).reshape(1, RW_COLS)
    wgate_p = jnp.pad(w_gate_up, ((0, GATE_PAD - GATE_LORA), (0, 0))).astype(BF16)

    att, rw = _inproj_call(x, mod3, mix_pre_norm.reshape(1, D), w_in_p)

    wuq = w_uq.reshape(Q_LORA, ATT_WIDTH).astype(BF16)
    wiq = w_idx_q.reshape(Q_LORA, IDX_HEADS * IDX_DIM).astype(BF16)
    wukt = jnp.transpose(w_uk, (0, 2, 1)).astype(BF16)
    qabs, iq, ik, ckv, iw = _dsaprep_call(att, q_norm.reshape(1, -1), kv_norm.reshape(1, -1),
                                          idx_k_norm.reshape(1, -1), wuq, wiq, wukt)
    bias_tiles = _bias_call(rel_bias)
    wuv_pad = jnp.zeros((H_ATT, KV_LATENT, ATT_WIDTH), F32)
    for h in range(H_ATT):
        wuv_pad = wuv_pad.at[h, :, h * HD_ATT:(h + 1) * HD_ATT].set(w_uv[h])
    topk = min(TOPK_MAX, T // 4)
    y_att = _dsa_call(iq, iw, qabs, ik, ckv, bias_tiles, wuv_pad.astype(BF16), topk)

    row = lambda z: z.reshape(1, RWKV_WIDTH)
    y_rwkv = _rwkv_call(rw, mu_p, row(w0), w_decay_up.astype(BF16), row(a0), w_aaa_up.astype(BF16),
                        wgate_p, row(k_k), row(k_a), row(r_k), row(ln_x_gain), row(ln_x_bias))

    return _ffn_call(x, y_att, y_rwkv, mod3, mix_post_norm.reshape(1, D), ffn_pre_norm.reshape(1, D),
                     ffn_post_norm.reshape(1, D), w_out.astype(BF16), w_ffn_gate.astype(BF16),
                     w_ffn_up.astype(BF16), w_ffn_down.astype(BF16))


def kernel(x, c, rel_bias, ada_w, ada_b, mix_pre_norm, mix_post_norm, ffn_pre_norm, ffn_post_norm, w_in, q_norm, w_uq, w_idx_q, kv_norm, idx_k_norm, w_uk, w_uv, mu_shift, w0, w_decay_up, a0, w_aaa_up, w_gate_up, k_k, k_a, r_k, ln_x_gain, ln_x_bias, w_out, w_ffn_gate, w_ffn_up, w_ffn_down):
    B, T, D = x.shape
    assert D == D_MODEL and T % 512 == 0 and T // 4 >= 1
    layer_params = (mix_pre_norm, mix_post_norm, ffn_pre_norm, ffn_post_norm,
                    w_in, q_norm, w_uq, w_idx_q, kv_norm, idx_k_norm, w_uk, w_uv,
                    mu_shift, w0, w_decay_up, a0, w_aaa_up, w_gate_up, k_k, k_a, r_k, ln_x_gain, ln_x_bias,
                    w_out, w_ffn_gate, w_ffn_up, w_ffn_down)
    for l in range(ada_w.shape[0]):
        mod3 = _mod_call(c, ada_w[l], ada_b[l]).reshape(B, 6, D)
        x = _layer(x, mod3, rel_bias, *[p[l] for p in layer_params])
    return x
```

```python
import functools
import math

import numpy as np
import jax
import jax.numpy as jnp
from jax import lax
from jax.experimental import pallas as pl
from jax.experimental.pallas import tpu as pltpu

F32 = jnp.float32
BF16 = jnp.bfloat16

D_MODEL = 1024
ATT_WIDTH = 512
RWKV_WIDTH = 512
HD_ATT = 64
H_ATT = 8
HD_RWKV = 64
H_RWKV = 8
Q_LORA = 256
KV_LATENT = 128
IDX_HEADS = 8
IDX_DIM = 64
TOPK_MAX = 256
NUM_BUCKETS = 32
MAX_DISTANCE = 128
DECAY_LORA = 64
AAA_LORA = 64
GATE_LORA = 160
GN_EPS = 64e-5
D_FF = 2816
NORM_EPS = 1e-6

ATT_PAD = 512
GATE_PAD = 256
RW_COLS = 3 * RWKV_WIDTH + DECAY_LORA + AAA_LORA + GATE_PAD
OFF_WD = 3 * RWKV_WIDTH
OFF_AD = OFF_WD + DECAY_LORA
OFF_GD = OFF_AD + AAA_LORA

TQ = 128
CHUNK = 64
NEG_BIG = -1e30
LOWEST = -3.0e38
VMEM_LIMIT = 56 * 1024 * 1024


def _cp(sem):
    return pltpu.CompilerParams(dimension_semantics=sem, vmem_limit_bytes=VMEM_LIMIT)


def _dot(a, b):
    return jnp.dot(a, b, preferred_element_type=F32)


def _dot_nt(a, b):
    return lax.dot_general(a, b, (((1,), (1,)), ((), ())), preferred_element_type=F32)


def _dot_tn(a, b):
    return lax.dot_general(a, b, (((0,), (0,)), ((), ())), preferred_element_type=F32)


def _rms(z, gain):
    return z * lax.rsqrt(jnp.mean(z * z, axis=-1, keepdims=True) + NORM_EPS) * gain


def _mod_kernel(c_ref, w_ref, b_ref, o_ref):
    c = c_ref[...]
    s = c * jax.nn.sigmoid(c)
    o_ref[...] = _dot(s.astype(BF16), w_ref[...].astype(BF16)) + b_ref[...]


def _mod_call(c, ada_w, ada_b):
    B, D = c.shape
    N = ada_w.shape[1]
    TN = 1536
    return pl.pallas_call(
        _mod_kernel,
        grid=(N // TN,),
        in_specs=[pl.BlockSpec((B, D), lambda j: (0, 0)),
                  pl.BlockSpec((D, TN), lambda j: (0, j)),
                  pl.BlockSpec((1, TN), lambda j: (0, j))],
        out_specs=pl.BlockSpec((B, TN), lambda j: (0, j)),
        out_shape=jax.ShapeDtypeStruct((B, N), F32),
        compiler_params=_cp(("parallel",)),
        name="mod",
    )(c, ada_w, ada_b.reshape(1, N))


def _inproj_kernel(x_ref, mod_ref, g_ref, w_ref, att_ref, rw_ref):
    x = x_ref[0]
    y = _rms(x, g_ref[...])
    sh = mod_ref[0, 0:1, :]
    sc = mod_ref[0, 1:2, :]
    h = (y * (1.0 + sc) + sh).astype(BF16)
    att_ref[0] = _dot(h, w_ref[:, 0:ATT_PAD])
    for n0 in range(0, RW_COLS, 384):
        rw_ref[0, :, n0:n0 + 384] = _dot(h, w_ref[:, ATT_PAD + n0:ATT_PAD + n0 + 384])


def _inproj_call(x, mod3, gain, w_in_p):
    B, T, D = x.shape
    TM = 512
    NP = w_in_p.shape[1]
    return pl.pallas_call(
        _inproj_kernel,
        grid=(B, T // TM),
        in_specs=[pl.BlockSpec((1, TM, D), lambda b, i: (b, i, 0)),
                  pl.BlockSpec((1, 6, D), lambda b, i: (b, 0, 0)),
                  pl.BlockSpec((1, D), lambda b, i: (0, 0)),
                  pl.BlockSpec((D, NP), lambda b, i: (0, 0))],
        out_specs=[pl.BlockSpec((1, TM, ATT_PAD), lambda b, i: (b, i, 0)),
                   pl.BlockSpec((1, TM, RW_COLS), lambda b, i: (b, i, 0))],
        out_shape=[jax.ShapeDtypeStruct((B, T, ATT_PAD), F32),
                   jax.ShapeDtypeStruct((B, T, RW_COLS), F32)],
        compiler_params=_cp(("parallel", "parallel")),
        name="inproj",
    )(x, mod3, gain, w_in_p)


def _dsaprep_kernel(att_ref, qn_ref, kvn_ref, ikn_ref, wuq_ref, wiq_ref, wukt_ref,
                    qabs_ref, iq_ref, ik_ref, ckv_ref, iw_ref):
    att = att_ref[0]
    o1 = Q_LORA
    o2 = o1 + KV_LATENT
    o3 = o2 + IDX_DIM
    cq = _rms(att[:, 0:o1], qn_ref[...]).astype(BF16)
    ckv_ref[0] = _rms(att[:, o1:o2], kvn_ref[...]).astype(BF16)
    ik_ref[0] = _rms(att[:, o2:o3], ikn_ref[...]).astype(BF16)
    iw_ref[0] = att[:, o3:o3 + IDX_HEADS] * (IDX_HEADS ** -0.5 * IDX_DIM ** -0.5)
    q = _dot(cq, wuq_ref[...])
    iqv = _dot(cq, wiq_ref[...])
    for h in range(H_ATT):
        qh = q[:, h * HD_ATT:(h + 1) * HD_ATT].astype(BF16)
        qabs_ref[0, h] = (_dot(qh, wukt_ref[h]) * (HD_ATT ** -0.5)).astype(BF16)
        iq_ref[0, h] = iqv[:, h * IDX_DIM:(h + 1) * IDX_DIM].astype(BF16)


def _dsaprep_call(att, q_norm, kv_norm, idx_k_norm, wuq, wiq, wukt):
    B, T, _ = att.shape
    TM = 512
    const2 = lambda b, i: (0, 0)
    return pl.pallas_call(
        _dsaprep_kernel,
        grid=(B, T // TM),
        in_specs=[pl.BlockSpec((1, TM, ATT_PAD), lambda b, i: (b, i, 0)),
                  pl.BlockSpec((1, Q_LORA), const2),
                  pl.BlockSpec((1, KV_LATENT), const2),
                  pl.BlockSpec((1, IDX_DIM), const2),
                  pl.BlockSpec((Q_LORA, ATT_WIDTH), const2),
                  pl.BlockSpec((Q_LORA, IDX_HEADS * IDX_DIM), const2),
                  pl.BlockSpec((H_ATT, HD_ATT, KV_LATENT), lambda b, i: (0, 0, 0))],
        out_specs=[pl.BlockSpec((1, H_ATT, TM, KV_LATENT), lambda b, i: (b, 0, i, 0)),
                   pl.BlockSpec((1, IDX_HEADS, TM, IDX_DIM), lambda b, i: (b, 0, i, 0)),
                   pl.BlockSpec((1, TM, IDX_DIM), lambda b, i: (b, i, 0)),
                   pl.BlockSpec((1, TM, KV_LATENT), lambda b, i: (b, i, 0)),
                   pl.BlockSpec((1, TM, IDX_HEADS), lambda b, i: (b, i, 0))],
        out_shape=[jax.ShapeDtypeStruct((B, H_ATT, T, KV_LATENT), BF16),
                   jax.ShapeDtypeStruct((B, IDX_HEADS, T, IDX_DIM), BF16),
                   jax.ShapeDtypeStruct((B, T, IDX_DIM), BF16),
                   jax.ShapeDtypeStruct((B, T, KV_LATENT), BF16),
                   jax.ShapeDtypeStruct((B, T, IDX_HEADS), F32)],
        compiler_params=_cp(("parallel", "parallel")),
        name="dsaprep",
    )(att, q_norm, kv_norm, idx_k_norm, wuq, wiq, wukt)


def _t5_bucket_np(rel):
    max_exact = NUM_BUCKETS // 2
    nf = np.maximum(rel, 1).astype(np.float32)
    large = max_exact + (np.log(nf / max_exact) / math.log(MAX_DISTANCE / max_exact)
                         * (NUM_BUCKETS - max_exact)).astype(np.int32)
    large = np.minimum(large, NUM_BUCKETS - 1)
    return np.where(rel < max_exact, rel, large).astype(np.int32)


def _near_bucket_tiles():
    i = np.arange(TQ)[:, None]
    s = np.arange(TQ)[None, :]
    tiles = [_t5_bucket_np(np.maximum(i - s + TQ * d, 0)) for d in range(2)]
    assert _t5_bucket_np(np.array([TQ + 1]))[0] == NUM_BUCKETS - 1
    return np.stack(tiles)


def _bias_kernel(rb_ref, bk_ref, o_ref):
    for d in range(2):
        bk = bk_ref[d]
        for h in range(H_ATT):
            def body(b, acc):
                return jnp.where(bk == b, rb_ref[b, h], acc)
            acc = lax.fori_loop(0, NUM_BUCKETS, body, jnp.zeros((TQ, TQ), F32))
            o_ref[d, h] = acc - rb_ref[NUM_BUCKETS - 1, h]


def _bias_call(rel_bias):
    buckets = jnp.asarray(_near_bucket_tiles())
    return pl.pallas_call(
        _bias_kernel,
        in_specs=[pl.BlockSpec(memory_space=pltpu.SMEM),
                  pl.BlockSpec(memory_space=pltpu.VMEM)],
        out_specs=pl.BlockSpec(memory_space=pltpu.VMEM),
        out_shape=jax.ShapeDtypeStruct((2, H_ATT, TQ, TQ), F32),
        name="bias",
    )(rel_bias, buckets)


def _dsa_kernel(iq_ref, iw_ref, qabs_ref, ik_ref, ckv_ref, bias_ref, wuv_ref, y_ref,
                score_scr, wb_scr, m_scr, l_scr, alpha_scr, p_scr, acc_scr, *, topk):
    i = pl.program_id(1)
    kf = float(topk)
    row = lax.broadcasted_iota(jnp.int32, (TQ, TQ), 0)
    col = lax.broadcasted_iota(jnp.int32, (TQ, TQ), 1)
    causal = col <= row

    iw = iw_ref[0]
    for h in range(IDX_HEADS):
        wb_scr[h] = jnp.broadcast_to(iw[:, h:h + 1], (TQ, TQ))
    iq2 = iq_ref[0].reshape(IDX_HEADS * TQ, IDX_DIM)

    def block_score(j):
        kj = ik_ref[0, pl.ds(pl.multiple_of(j * TQ, TQ), TQ), :]
        s = _dot_nt(iq2, kj)
        acc = jnp.zeros((TQ, TQ), F32)
        for h in range(IDX_HEADS):
            acc = acc + jnp.maximum(s[h * TQ:(h + 1) * TQ], 0.0) * wb_scr[h]
        return acc

    def p1_body(j, carry):
        rmax, rmin = carry
        sc = block_score(j)
        score_scr[j] = sc
        return jnp.maximum(rmax, sc), jnp.minimum(rmin, sc)

    rmax, rmin = lax.fori_loop(0, i, p1_body,
                               (jnp.full((TQ, TQ), -jnp.inf, F32), jnp.full((TQ, TQ), jnp.inf, F32)))
    sc = block_score(i)
    score_scr[i] = jnp.where(causal, sc, -jnp.inf)
    rmax = jnp.maximum(rmax, jnp.where(causal, sc, -jnp.inf))
    rmin = jnp.minimum(rmin, jnp.where(causal, sc, jnp.inf))
    hi0 = jnp.max(rmax, axis=-1, keepdims=True)
    lo0 = jnp.min(rmin, axis=-1, keepdims=True)

    nvalid = (i * TQ + lax.broadcasted_iota(jnp.int32, (TQ, 1), 0) + 1).astype(F32)
    needs_thr = nvalid > kf

    def count(pred_fn):
        def body(j, acc):
            return acc + jnp.where(pred_fn(score_scr[j]), 1.0, 0.0)
        acc = lax.fori_loop(0, i + 1, body, jnp.zeros((TQ, TQ), F32))
        return jnp.sum(acc, axis=-1, keepdims=True)

    def bisect(_, st):
        lo, hi, cnt_lo = st
        mid = 0.5 * lo + 0.5 * hi
        midb = jnp.broadcast_to(mid, (TQ, TQ))
        c = count(lambda s: s >= midb)
        ge = c >= kf
        return jnp.where(ge, mid, lo), jnp.where(ge, hi, mid), jnp.where(ge, c, cnt_lo)

    def refine(st):
        lo, hi, cnt_lo, _, _, _, it = st
        lo, hi, cnt_lo = lax.fori_loop(0, 8, bisect, (lo, hi, cnt_lo))
        lob = jnp.broadcast_to(lo, (TQ, TQ))

        def min_body(j, acc):
            s = score_scr[j]
            return jnp.minimum(acc, jnp.where(s >= lob, s, jnp.inf))
        thr = jnp.min(lax.fori_loop(0, i + 1, min_body, jnp.full((TQ, TQ), jnp.inf, F32)),
                      axis=-1, keepdims=True)
        thrb = jnp.broadcast_to(thr, (TQ, TQ))
        cgt = count(lambda s: s > thrb)
        unresolved = jnp.sum(jnp.where(needs_thr & (cgt >= kf), 1.0, 0.0))
        return lo, hi, cnt_lo, thr, cgt, unresolved, it + 1

    def refine_cond(st):
        return jnp.logical_and(st[5] > 0.0, st[6] < 40)

    st0 = (lo0, hi0, nvalid, lo0, nvalid, jnp.float32(1.0), jnp.int32(0))
    _, _, cnt_lo, thr, cgt, _, _ = lax.while_loop(refine_cond, refine, st0)

    tie = needs_thr & (cnt_lo > kf)
    n_tie = jnp.sum(jnp.where(tie, 1.0, 0.0))

    @pl.when(n_tie > 0.0)
    def _():
        upper = jnp.where(row <= col, 1.0, 0.0).astype(BF16)
        thrb = jnp.broadcast_to(thr, (TQ, TQ))
        tieb = jnp.broadcast_to(tie, (TQ, TQ))
        needb = jnp.broadcast_to(kf - cgt, (TQ, TQ))

        def body(j, carry):
            s = score_scr[j]
            eq = (s == thrb) & tieb
            pref = _dot(jnp.where(eq, 1.0, 0.0).astype(BF16), upper) + carry
            score_scr[j] = jnp.where(eq & (pref > needb), -jnp.inf, s)
            return jnp.broadcast_to(pref[:, TQ - 1:TQ], (TQ, TQ))
        lax.fori_loop(0, i + 1, body, jnp.zeros((TQ, TQ), F32))

    thr_eff = jnp.broadcast_to(jnp.where(needs_thr, thr, LOWEST), (TQ, TQ))

    m_scr[...] = jnp.full(m_scr.shape, NEG_BIG, F32)
    l_scr[...] = jnp.zeros(l_scr.shape, F32)
    acc_scr[...] = jnp.zeros(acc_scr.shape, F32)
    qa2 = qabs_ref[0].reshape(H_ATT * TQ, KV_LATENT)

    def attend(j, near):
        kv = ckv_ref[0, pl.ds(pl.multiple_of(j * TQ, TQ), TQ), :]
        logits = _dot_nt(qa2, kv)
        sel = score_scr[j] >= thr_eff
        for h in range(H_ATT):
            lg = logits[h * TQ:(h + 1) * TQ]
            if near is not None:
                lg = lg + bias_ref[near, h]
            lg = jnp.where(sel, lg, NEG_BIG)
            m_prev = m_scr[h]
            m_new = jnp.maximum(m_prev, jnp.max(lg, axis=-1, keepdims=True))
            alpha = jnp.exp(m_prev - m_new)
            p = jnp.exp(lg - m_new)
            l_scr[h] = alpha * l_scr[h] + jnp.sum(p, axis=-1, keepdims=True)
            m_scr[h] = m_new
            alpha_scr[h] = alpha
            p_scr[h * TQ:(h + 1) * TQ, :] = p.astype(BF16)
        pv = _dot(p_scr[...], kv)
        for h in range(H_ATT):
            sl = slice(h * TQ, (h + 1) * TQ)
            acc_scr[sl, :] = alpha_scr[h] * acc_scr[sl, :] + pv[sl]

    def far_body(j, c):
        attend(j, None)
        return c
    lax.fori_loop(0, jnp.maximum(i - 1, 0), far_body, 0)

    @pl.when(i >= 1)
    def _():
        attend(i - 1, 1)

    attend(i, 0)

    y = jnp.zeros((TQ, ATT_WIDTH), F32)
    for h in range(H_ATT):
        o = acc_scr[h * TQ:(h + 1) * TQ, :] / l_scr[h]
        y = y + _dot(o.astype(BF16), wuv_ref[h])
    y_ref[0] = y


def _dsa_call(iq, iw, qabs, ik, ckv, bias_tiles, wuv_pad, topk):
    B, _, T, _ = iq.shape
    NT = T // TQ
    return pl.pallas_call(
        functools.partial(_dsa_kernel, topk=topk),
        grid=(B, NT),
        in_specs=[pl.BlockSpec((1, IDX_HEADS, TQ, IDX_DIM), lambda b, i: (b, 0, i, 0)),
                  pl.BlockSpec((1, TQ, IDX_HEADS), lambda b, i: (b, i, 0)),
                  pl.BlockSpec((1, H_ATT, TQ, KV_LATENT), lambda b, i: (b, 0, i, 0)),
                  pl.BlockSpec((1, T, IDX_DIM), lambda b, i: (b, 0, 0)),
                  pl.BlockSpec((1, T, KV_LATENT), lambda b, i: (b, 0, 0)),
                  pl.BlockSpec((2, H_ATT, TQ, TQ), lambda b, i: (0, 0, 0, 0)),
                  pl.BlockSpec((H_ATT, KV_LATENT, ATT_WIDTH), lambda b, i: (0, 0, 0))],
        out_specs=pl.BlockSpec((1, TQ, ATT_WIDTH), lambda b, i: (b, i, 0)),
        out_shape=jax.ShapeDtypeStruct((B, T, ATT_WIDTH), F32),
        scratch_shapes=[pltpu.VMEM((NT, TQ, TQ), F32),
                        pltpu.VMEM((IDX_HEADS, TQ, TQ), F32),
                        pltpu.VMEM((H_ATT, TQ, TQ), F32),
                        pltpu.VMEM((H_ATT, TQ, TQ), F32),
                        pltpu.VMEM((H_ATT, TQ, TQ), F32),
                        pltpu.VMEM((H_ATT * TQ, TQ), BF16),
                        pltpu.VMEM((H_ATT * TQ, KV_LATENT), F32)],
        compiler_params=_cp(("parallel", "arbitrary")),
        name="dsa",
    )(iq, iw, qabs, ik, ckv, bias_tiles, wuv_pad)


def _rwkv_kernel(p_ref, prev_ref, mu_ref, w0_ref, wdec_ref, a0_ref, waaa_ref, wgate_ref,
                 kk_ref, ka_ref, rk_ref, lng_ref, lnb_ref, y_ref, h_scr):
    c = pl.program_id(1)
    C = CHUNK
    N = HD_RWKV

    @pl.when(c == 0)
    def _():
        h_scr[...] = jnp.zeros(h_scr.shape, F32)

    p = p_ref[0]
    prev_row = jnp.where(c > 0, prev_ref[0, 7:8, :], 0.0)
    rid = lax.broadcasted_iota(jnp.int32, (C, 1), 0)
    p_prev = jnp.where(rid == 0, prev_row, pltpu.roll(p, 1, 0))
    p = p + mu_ref[...] * (p_prev - p)

    W = RWKV_WIDTH
    r = p[:, 0:W]
    k = p[:, W:2 * W]
    v = p[:, 2 * W:3 * W]
    wd = p[:, OFF_WD:OFF_WD + DECAY_LORA]
    ad = p[:, OFF_AD:OFF_AD + AAA_LORA]
    gd = p[:, OFF_GD:OFF_GD + GATE_PAD]

    w_log = -jax.nn.softplus(-(w0_ref[...] + _dot(jnp.tanh(wd).astype(BF16), wdec_ref[...]))) - 0.5
    logw = -jnp.exp(w_log)
    a = jax.nn.sigmoid(a0_ref[...] + _dot(ad.astype(BF16), waaa_ref[...]))
    g = _dot(jax.nn.sigmoid(gd).astype(BF16), wgate_ref[...])

    r64 = lax.broadcasted_iota(jnp.int32, (C, C), 0)
    c64 = lax.broadcasted_iota(jnp.int32, (C, C), 1)
    tril_incl = r64 >= c64
    tril_strict = r64 > c64
    eye = r64 == c64
    cum = jnp.dot(jnp.where(tril_incl, 1.0, 0.0), logw, preferred_element_type=F32,
                  precision=lax.Precision.HIGHEST)
    cum_last = cum[C - 1:C, :]
    g_in = jnp.exp(cum - logw)
    g_t = jnp.exp(cum)
    g_inv = jnp.exp(-cum)
    g_end = jnp.exp(cum_last - cum)
    g_all = jnp.exp(cum_last)

    kkf = k * kk_ref[...]
    kmod = k * (1.0 + (a - 1.0) * ka_ref[...])
    rkr = r * kmod * rk_ref[...]

    H = range(H_RWKV)
    sls = [slice(h * N, (h + 1) * N) for h in H]
    lane2 = lax.broadcasted_iota(jnp.int32, (C, 2 * C), 1)
    row2 = lax.broadcasted_iota(jnp.int32, (C, 2 * C), 0)
    in_left = lane2 < C
    col2 = jnp.where(in_left, lane2, lane2 - C)
    strict2 = row2 > col2
    incl2 = row2 >= col2
    zpad = jnp.zeros((C, C), F32)

    at, rt, bh, kh, vb, g4 = [], [], [], [], [], []
    for h in H:
        sl = sls[h]
        kk_h = kkf[:, sl]
        nrm = jnp.sqrt(jnp.sum(kk_h * kk_h, axis=-1, keepdims=True))
        kk_h = kk_h / jnp.maximum(nrm, 1e-12)
        k_h = kmod[:, sl]
        bvec = kk_h * a[:, sl]
        at_h = (-kk_h) * g_in[:, sl]
        rt_h = r[:, sl] * g_t[:, sl]
        lhs = jnp.concatenate([at_h, rt_h], axis=0).astype(BF16)
        rhs = jnp.concatenate([bvec * g_inv[:, sl], k_h * g_inv[:, sl]], axis=0).astype(BF16)
        g4.append(_dot_nt(lhs, rhs))
        at.append(at_h)
        rt.append(rt_h.astype(BF16))
        bh.append((bvec * g_end[:, sl]).astype(BF16))
        kh.append((k_h * g_end[:, sl]).astype(BF16))
        vb.append(v[:, sl].astype(BF16))

    a_ab = [jnp.where(tril_strict, g4[h][0:C, 0:C], 0.0) for h in H]
    tinv = [jnp.where(eye, 1.0, 0.0) + a_ab[h] for h in H]
    pw = a_ab
    for _ in range(5):
        pwb = [pw[h].astype(BF16) for h in H]
        pw = [_dot(pwb[h], pwb[h]) for h in H]
        tinv = [tinv[h] + _dot(tinv[h].astype(BF16), pw[h].astype(BF16)) for h in H]

    hst = [h_scr[h] for h in H]
    hb = [hst[h].astype(BF16) for h in H]
    x = []
    for h in H:
        xl = jnp.where(in_left, jnp.concatenate([at[h], zpad], axis=1),
                       jnp.where(strict2, g4[h][0:C, :], 0.0)).astype(BF16)
        x.append(_dot(xl, jnp.concatenate([hb[h], vb[h]], axis=0)))
    ub = [_dot(tinv[h].astype(BF16), x[h].astype(BF16)).astype(BF16) for h in H]
    uv = [jnp.concatenate([ub[h], vb[h]], axis=0) for h in H]
    y = [_dot(rt[h], hb[h]) + _dot(jnp.where(incl2, g4[h][C:2 * C, :], 0.0).astype(BF16), uv[h]) for h in H]
    for h in H:
        g_col = jnp.sum(jnp.where(eye, jnp.broadcast_to(g_all[:, sls[h]], (N, N)), 0.0), axis=1, keepdims=True)
        h_scr[h] = g_col * hst[h] + _dot_tn(jnp.concatenate([bh[h], kh[h]], axis=0), uv[h])

    for h in H:
        sl = sls[h]
        mean = jnp.mean(y[h], axis=-1, keepdims=True)
        yc = y[h] - mean
        var = jnp.mean(yc * yc, axis=-1, keepdims=True)
        yn = yc * lax.rsqrt(var + GN_EPS) * lng_ref[:, sl] + lnb_ref[:, sl]
        bonus = jnp.sum(rkr[:, sl], axis=-1, keepdims=True) * v[:, sl]
        y_ref[0, :, sl] = (yn + bonus) * g[:, sl]


def _rwkv_call(rw, mu_p, w0, wdec, a0, waaa, wgate_p, k_k, k_a, r_k, lng, lnb):
    B, T, _ = rw.shape
    NC = T // CHUNK
    W = RWKV_WIDTH
    c2 = lambda b, c: (0, 0)
    rowspec = pl.BlockSpec((1, W), c2)
    return pl.pallas_call(
        _rwkv_kernel,
        grid=(B, NC),
        in_specs=[pl.BlockSpec((1, CHUNK, RW_COLS), lambda b, c: (b, c, 0)),
                  pl.BlockSpec((1, 8, RW_COLS), lambda b, c: (b, jnp.maximum(c * (CHUNK // 8) - 1, 0), 0)),
                  pl.BlockSpec((1, RW_COLS), c2),
                  rowspec,
                  pl.BlockSpec((DECAY_LORA, W), c2),
                  rowspec,
                  pl.BlockSpec((AAA_LORA, W), c2),
                  pl.BlockSpec((GATE_PAD, W), c2),
                  rowspec, rowspec, rowspec, rowspec, rowspec],
        out_specs=pl.BlockSpec((1, CHUNK, W), lambda b, c: (b, c, 0)),
        out_shape=jax.ShapeDtypeStruct((B, T, W), F32),
        scratch_shapes=[pltpu.VMEM((H_RWKV, HD_RWKV, HD_RWKV), F32)],
        compiler_params=_cp(("parallel", "arbitrary")),
        name="rwkv",
    )(rw, rw, mu_p, w0, wdec, a0, waaa, wgate_p, k_k, k_a, r_k, lng, lnb)


def _ffn_kernel(x_ref, ya_ref, yr_ref, mod_ref, gpost_ref, gpre_ref, gfpost_ref,
                wo_ref, wg_ref, wu_ref, wd_ref, o_ref, x1_scr, hf_scr, acc_scr):
    j = pl.program_id(2)

    @pl.when(j == 0)
    def _():
        mix = (_dot(ya_ref[0].astype(BF16), wo_ref[0:ATT_WIDTH, :])
               + _dot(yr_ref[0].astype(BF16), wo_ref[ATT_WIDTH:, :]))
        x1 = x_ref[0] + mod_ref[0, 2:3, :] * _rms(mix, gpost_ref[...])
        x1_scr[...] = x1
        hf = _rms(x1, gpre_ref[...]) * (1.0 + mod_ref[0, 4:5, :]) + mod_ref[0, 3:4, :]
        hf_scr[...] = hf.astype(BF16)
        acc_scr[...] = jnp.zeros(acc_scr.shape, F32)

    hf = hf_scr[...]
    gate = _dot(hf, wg_ref[...])
    up = _dot(hf, wu_ref[...])
    act = (gate * jax.nn.sigmoid(gate) * up).astype(BF16)
    acc_scr[...] += _dot(act, wd_ref[...])

    @pl.when(j == pl.num_programs(2) - 1)
    def _():
        o_ref[0] = x1_scr[...] + mod_ref[0, 5:6, :] * _rms(acc_scr[...], gfpost_ref[...])


def _ffn_call(x, y_att, y_rwkv, mod3, g_post, g_pre, g_fpost, wo, wg, wu, wd):
    B, T, D = x.shape
    TM = 512
    TF = D_FF // 2
    c3 = lambda b, i, j: (0, 0)
    return pl.pallas_call(
        _ffn_kernel,
        grid=(B, T // TM, D_FF // TF),
        in_specs=[pl.BlockSpec((1, TM, D), lambda b, i, j: (b, i, 0)),
                  pl.BlockSpec((1, TM, ATT_WIDTH), lambda b, i, j: (b, i, 0)),
                  pl.BlockSpec((1, TM, RWKV_WIDTH), lambda b, i, j: (b, i, 0)),
                  pl.BlockSpec((1, 6, D), lambda b, i, j: (b, 0, 0)),
                  pl.BlockSpec((1, D), c3), pl.BlockSpec((1, D), c3), pl.BlockSpec((1, D), c3),
                  pl.BlockSpec((D, D), c3),
                  pl.BlockSpec((D, TF), lambda b, i, j: (0, j)),
                  pl.BlockSpec((D, TF), lambda b, i, j: (0, j)),
                  pl.BlockSpec((TF, D), lambda b, i, j: (j, 0))],
        out_specs=pl.BlockSpec((1, TM, D), lambda b, i, j: (b, i, 0)),
        out_shape=jax.ShapeDtypeStruct((B, T, D), F32),
        scratch_shapes=[pltpu.VMEM((TM, D), F32), pltpu.VMEM((TM, D), BF16), pltpu.VMEM((TM, D), F32)],
        compiler_params=_cp(("parallel", "parallel", "arbitrary")),
        name="ffn",
    )(x, y_att, y_rwkv, mod3, g_post, g_pre, g_fpost, wo, wg, wu, wd)


def _pad_cols(w, n):
    return jnp.pad(w, ((0, 0), (0, n - w.shape[1])))


def _layer(x, mod3, rel_bias, mix_pre_norm, mix_post_norm, ffn_pre_norm, ffn_post_norm,
           w_in, q_norm, w_uq, w_idx_q, kv_norm, idx_k_norm, w_uk, w_uv,
           mu_shift, w0, w_decay_up, a0, w_aaa_up, w_gate_up, k_k, k_a, r_k, ln_x_gain, ln_x_bias,
           w_out, w_ffn_gate, w_ffn_up, w_ffn_down):
    B, T, D = x.shape
    att_cols = Q_LORA + KV_LATENT + IDX_DIM + IDX_HEADS
    n_main = 3 * RWKV_WIDTH + DECAY_LORA + AAA_LORA
    w_att = _pad_cols(w_in[:, :att_cols], ATT_PAD)
    w_rw = w_in[:, att_cols:]
    w_in_p = jnp.concatenate([w_att, w_rw[:, :n_main], _pad_cols(w_rw[:, n_main:], GATE_PAD)], axis=1).astype(BF16)
    mu_p = jnp.concatenate([mu_shift[:n_main], jnp.pad(mu_shift[n_main:], (0, GATE_PAD - GATE_LORA))]).reshape(1, RW_COLS)
    wgate_p = jnp.pad(w_gate_up, ((0, GATE_PAD - GATE_LORA), (0, 0))).astype(BF16)

    att, rw = _inproj_call(x, mod3, mix_pre_norm.reshape(1, D), w_in_p)

    wuq = w_uq.reshape(Q_LORA, ATT_WIDTH).astype(BF16)
    wiq = w_idx_q.reshape(Q_LORA, IDX_HEADS * IDX_DIM).astype(BF16)
    wukt = jnp.transpose(w_uk, (0, 2, 1)).astype(BF16)
    qabs, iq, ik, ckv, iw = _dsaprep_call(att, q_norm.reshape(1, -1), kv_norm.reshape(1, -1),
                                          idx_k_norm.reshape(1, -1), wuq, wiq, wukt)
    bias_tiles = _bias_call(rel_bias)
    wuv_pad = jnp.zeros((H_ATT, KV_LATENT, ATT_WIDTH), F32)
    for h in range(H_ATT):
        wuv_pad = wuv_pad.at[h, :, h * HD_ATT:(h + 1) * HD_ATT].set(w_uv[h])
    topk = min(TOPK_MAX, T // 4)
    y_att = _dsa_call(iq, iw, qabs, ik, ckv, bias_tiles, wuv_pad.astype(BF16), topk)

    row = lambda z: z.reshape(1, RWKV_WIDTH)
    y_rwkv = _rwkv_call(rw, mu_p, row(w0), w_decay_up.astype(BF16), row(a0), w_aaa_up.astype(BF16),
                        wgate_p, row(k_k), row(k_a), row(r_k), row(ln_x_gain), row(ln_x_bias))

    return _ffn_call(x, y_att, y_rwkv, mod3, mix_post_norm.reshape(1, D), ffn_pre_norm.reshape(1, D),
                     ffn_post_norm.reshape(1, D), w_out.astype(BF16), w_ffn_gate.astype(BF16),
                     w_ffn_up.astype(BF16), w_ffn_down.astype(BF16))


def kernel(x, c, rel_bias, ada_w, ada_b, mix_pre_norm, mix_post_norm, ffn_pre_norm, ffn_post_norm, w_in, q_norm, w_uq, w_idx_q, kv_norm, idx_k_norm, w_uk, w_uv, mu_shift, w0, w_decay_up, a0, w_aaa_up, w_gate_up, k_k, k_a, r_k, ln_x_gain, ln_x_bias, w_out, w_ffn_gate, w_ffn_up, w_ffn_down):
    B, T, D = x.shape
    assert D == D_MODEL and T % 512 == 0 and T // 4 >= 1
    layer_params = (mix_pre_norm, mix_post_norm, ffn_pre_norm, ffn_post_norm,
                    w_in, q_norm, w_uq, w_idx_q, kv_norm, idx_k_norm, w_uk, w_uv,
                    mu_shift, w0, w_decay_up, a0, w_aaa_up, w_gate_up, k_k, k_a, r_k, ln_x_gain, ln_x_bias,
                    w_out, w_ffn_gate, w_ffn_up, w_ffn_down)
    for l in range(ada_w.shape[0]):
        mod3 = _mod_call(c, ada_w[l], ada_b[l]).reshape(B, 6, D)
        x = _layer(x, mod3, rel_bias, *[p[l] for p in layer_params])
    return x
```

```python
import functools
import math

import numpy as np
import jax
import jax.numpy as jnp
from jax import lax
from jax.experimental import pallas as pl
from jax.experimental.pallas import tpu as pltpu

F32 = jnp.float32
BF16 = jnp.bfloat16

D_MODEL = 1024
ATT_WIDTH = 512
RWKV_WIDTH = 512
HD_ATT = 64
H_ATT = 8
HD_RWKV = 64
H_RWKV = 8
Q_LORA = 256
KV_LATENT = 128
IDX_HEADS = 8
IDX_DIM = 64
TOPK_MAX = 256
NUM_BUCKETS = 32
MAX_DISTANCE = 128
DECAY_LORA = 64
AAA_LORA = 64
GATE_LORA = 160
GN_EPS = 64e-5
D_FF = 2816
NORM_EPS = 1e-6

ATT_PAD = 512
GATE_PAD = 256
RW_COLS = 3 * RWKV_WIDTH + DECAY_LORA + AAA_LORA + GATE_PAD
OFF_WD = 3 * RWKV_WIDTH
OFF_AD = OFF_WD + DECAY_LORA
OFF_GD = OFF_AD + AAA_LORA

TQ = 128
CHUNK = 64
NEG_BIG = -1e30
LOWEST = -3.0e38
VMEM_LIMIT = 56 * 1024 * 1024


def _cp(sem):
    return pltpu.CompilerParams(dimension_semantics=sem, vmem_limit_bytes=VMEM_LIMIT)


def _dot(a, b):
    return jnp.dot(a, b, preferred_element_type=F32)


def _dot_nt(a, b):
    return lax.dot_general(a, b, (((1,), (1,)), ((), ())), preferred_element_type=F32)


def _dot_tn(a, b):
    return lax.dot_general(a, b, (((0,), (0,)), ((), ())), preferred_element_type=F32)


def _rms(z, gain):
    return z * lax.rsqrt(jnp.mean(z * z, axis=-1, keepdims=True) + NORM_EPS) * gain


def _mod_kernel(c_ref, w_ref, b_ref, o_ref):
    c = c_ref[...]
    s = c * jax.nn.sigmoid(c)
    o_ref[...] = _dot(s.astype(BF16), w_ref[...].astype(BF16)) + b_ref[...]


def _mod_call(c, ada_w, ada_b):
    B, D = c.shape
    N = ada_w.shape[1]
    TN = 1536
    return pl.pallas_call(
        _mod_kernel,
        grid=(N // TN,),
        in_specs=[pl.BlockSpec((B, D), lambda j: (0, 0)),
                  pl.BlockSpec((D, TN), lambda j: (0, j)),
                  pl.BlockSpec((1, TN), lambda j: (0, j))],
        out_specs=pl.BlockSpec((B, TN), lambda j: (0, j)),
        out_shape=jax.ShapeDtypeStruct((B, N), F32),
        compiler_params=_cp(("parallel",)),
        name="mod",
    )(c, ada_w, ada_b.reshape(1, N))


def _inproj_kernel(x_ref, mod_ref, g_ref, w_ref, att_ref, rw_ref):
    x = x_ref[0]
    y = _rms(x, g_ref[...])
    sh = mod_ref[0, 0:1, :]
    sc = mod_ref[0, 1:2, :]
    h = (y * (1.0 + sc) + sh).astype(BF16)
    att_ref[0] = _dot(h, w_ref[:, 0:ATT_PAD])
    for n0 in range(0, RW_COLS, 384):
        rw_ref[0, :, n0:n0 + 384] = _dot(h, w_ref[:, ATT_PAD + n0:ATT_PAD + n0 + 384])


def _inproj_call(x, mod3, gain, w_in_p):
    B, T, D = x.shape
    TM = 512
    NP = w_in_p.shape[1]
    return pl.pallas_call(
        _inproj_kernel,
        grid=(B, T // TM),
        in_specs=[pl.BlockSpec((1, TM, D), lambda b, i: (b, i, 0)),
                  pl.BlockSpec((1, 6, D), lambda b, i: (b, 0, 0)),
                  pl.BlockSpec((1, D), lambda b, i: (0, 0)),
                  pl.BlockSpec((D, NP), lambda b, i: (0, 0))],
        out_specs=[pl.BlockSpec((1, TM, ATT_PAD), lambda b, i: (b, i, 0)),
                   pl.BlockSpec((1, TM, RW_COLS), lambda b, i: (b, i, 0))],
        out_shape=[jax.ShapeDtypeStruct((B, T, ATT_PAD), F32),
                   jax.ShapeDtypeStruct((B, T, RW_COLS), F32)],
        compiler_params=_cp(("parallel", "parallel")),
        name="inproj",
    )(x, mod3, gain, w_in_p)


def _dsaprep_kernel(att_ref, qn_ref, kvn_ref, ikn_ref, wuq_ref, wiq_ref, wukt_ref,
                    qabs_ref, iq_ref, ik_ref, ckv_ref, ckvt_ref, iwt_ref):
    att = att_ref[0]
    tm = att.shape[0]
    o1 = Q_LORA
    o2 = o1 + KV_LATENT
    o3 = o2 + IDX_DIM
    cq = _rms(att[:, 0:o1], qn_ref[...]).astype(BF16)
    ckv = _rms(att[:, o1:o2], kvn_ref[...])
    ckv_ref[0] = ckv.astype(BF16)
    ik_ref[0] = _rms(att[:, o2:o3], ikn_ref[...]).astype(BF16)
    tail = att[:, o2:ATT_PAD]
    w_off = o3 - o2
    for sb in range(tm // TQ):
        rows = slice(sb * TQ, (sb + 1) * TQ)
        ckvt_ref[0, sb] = ckv[rows, :].T.astype(BF16)
        iwt_ref[0, :, rows] = tail[rows, :].T[w_off:w_off + IDX_HEADS, :] * (IDX_HEADS ** -0.5 * IDX_DIM ** -0.5)
    q = _dot(cq, wuq_ref[...])
    iqv = _dot(cq, wiq_ref[...])
    for h in range(H_ATT):
        qh = q[:, h * HD_ATT:(h + 1) * HD_ATT].astype(BF16)
        qabs_ref[0, h] = (_dot(qh, wukt_ref[h]) * (HD_ATT ** -0.5)).astype(BF16)
        iq_ref[0, h] = iqv[:, h * IDX_DIM:(h + 1) * IDX_DIM].astype(BF16)


def _dsaprep_call(att, q_norm, kv_norm, idx_k_norm, wuq, wiq, wukt):
    B, T, _ = att.shape
    TM = 512
    const2 = lambda b, i: (0, 0)
    return pl.pallas_call(
        _dsaprep_kernel,
        grid=(B, T // TM),
        in_specs=[pl.BlockSpec((1, TM, ATT_PAD), lambda b, i: (b, i, 0)),
                  pl.BlockSpec((1, Q_LORA), const2),
                  pl.BlockSpec((1, KV_LATENT), const2),
                  pl.BlockSpec((1, IDX_DIM), const2),
                  pl.BlockSpec((Q_LORA, ATT_WIDTH), const2),
                  pl.BlockSpec((Q_LORA, IDX_HEADS * IDX_DIM), const2),
                  pl.BlockSpec((H_ATT, HD_ATT, KV_LATENT), lambda b, i: (0, 0, 0))],
        out_specs=[pl.BlockSpec((1, H_ATT, TM, KV_LATENT), lambda b, i: (b, 0, i, 0)),
                   pl.BlockSpec((1, IDX_HEADS, TM, IDX_DIM), lambda b, i: (b, 0, i, 0)),
                   pl.BlockSpec((1, TM, IDX_DIM), lambda b, i: (b, i, 0)),
                   pl.BlockSpec((1, TM, KV_LATENT), lambda b, i: (b, i, 0)),
                   pl.BlockSpec((1, TM // TQ, KV_LATENT, TQ), lambda b, i: (b, i, 0, 0)),
                   pl.BlockSpec((1, IDX_HEADS, TM), lambda b, i: (b, 0, i))],
        out_shape=[jax.ShapeDtypeStruct((B, H_ATT, T, KV_LATENT), BF16),
                   jax.ShapeDtypeStruct((B, IDX_HEADS, T, IDX_DIM), BF16),
                   jax.ShapeDtypeStruct((B, T, IDX_DIM), BF16),
                   jax.ShapeDtypeStruct((B, T, KV_LATENT), BF16),
                   jax.ShapeDtypeStruct((B, T // TQ, KV_LATENT, TQ), BF16),
                   jax.ShapeDtypeStruct((B, IDX_HEADS, T), F32)],
        compiler_params=_cp(("parallel", "parallel")),
        name="dsaprep",
    )(att, q_norm, kv_norm, idx_k_norm, wuq, wiq, wukt)


def _t5_bucket_np(rel):
    max_exact = NUM_BUCKETS // 2
    nf = np.maximum(rel, 1).astype(np.float32)
    large = max_exact + (np.log(nf / max_exact) / math.log(MAX_DISTANCE / max_exact)
                         * (NUM_BUCKETS - max_exact)).astype(np.int32)
    large = np.minimum(large, NUM_BUCKETS - 1)
    return np.where(rel < max_exact, rel, large).astype(np.int32)


def _near_bucket_tiles():
    s = np.arange(TQ)[:, None]
    t = np.arange(TQ)[None, :]
    tiles = [_t5_bucket_np(np.maximum(t - s + TQ * d, 0)) for d in range(2)]
    assert _t5_bucket_np(np.array([TQ + 1]))[0] == NUM_BUCKETS - 1
    return np.stack(tiles)


def _bias_kernel(rb_ref, bk_ref, o_ref):
    for d in range(2):
        bk = bk_ref[d]
        for h in range(H_ATT):
            def body(b, acc):
                return jnp.where(bk == b, rb_ref[b, h], acc)
            acc = lax.fori_loop(0, NUM_BUCKETS, body, jnp.zeros((TQ, TQ), F32))
            o_ref[d, h] = acc - rb_ref[NUM_BUCKETS - 1, h]


def _bias_call(rel_bias):
    buckets = jnp.asarray(_near_bucket_tiles())
    return pl.pallas_call(
        _bias_kernel,
        in_specs=[pl.BlockSpec(memory_space=pltpu.SMEM),
                  pl.BlockSpec(memory_space=pltpu.VMEM)],
        out_specs=pl.BlockSpec(memory_space=pltpu.VMEM),
        out_shape=jax.ShapeDtypeStruct((2, H_ATT, TQ, TQ), F32),
        name="bias",
    )(rel_bias, buckets)


KV = TQ // 8
SEARCH_CAP = 1024


def _key_reduce(x, op):
    return op(op(x, axis=0), axis=0, keepdims=True)


def _dsa_kernel(iq_ref, iwt_ref, qabs_ref, ik_ref, ckv_ref, ckvt_ref, bias_ref, wuvt_ref, y_ref,
                score_scr, p_scr, acc_scr, *, topk):
    i = pl.program_id(1)
    kf = float(topk)
    kidx = (lax.broadcasted_iota(jnp.int32, (KV, 8, TQ), 0) * 8
            + lax.broadcasted_iota(jnp.int32, (KV, 8, TQ), 1))
    qidx = lax.broadcasted_iota(jnp.int32, (KV, 8, TQ), 2)
    causal = kidx <= qidx

    iwt = iwt_ref[0]
    w8 = [jnp.broadcast_to(iwt[h:h + 1, :], (8, TQ))[None] for h in range(IDX_HEADS)]
    iq2 = iq_ref[0].reshape(IDX_HEADS * TQ, IDX_DIM)

    def block_score(j):
        kj = ik_ref[0, pl.ds(pl.multiple_of(j * TQ, TQ), TQ), :]
        s = _dot_nt(kj, iq2)
        acc = jnp.zeros((KV, 8, TQ), F32)
        for h in range(IDX_HEADS):
            acc = acc + jnp.maximum(s[:, h * TQ:(h + 1) * TQ].reshape(KV, 8, TQ), 0.0) * w8[h]
        return acc

    def p1_body(j, carry):
        rmax, rmin = carry
        sc = block_score(j)
        score_scr[j] = sc
        return jnp.maximum(rmax, jnp.max(sc, axis=0)), jnp.minimum(rmin, jnp.min(sc, axis=0))

    rmax, rmin = lax.fori_loop(0, i, p1_body,
                               (jnp.full((8, TQ), -jnp.inf, F32), jnp.full((8, TQ), jnp.inf, F32)))
    sc = block_score(i)
    score_scr[i] = jnp.where(causal, sc, -jnp.inf)
    rmax = jnp.maximum(rmax, jnp.max(jnp.where(causal, sc, -jnp.inf), axis=0))
    rmin = jnp.minimum(rmin, jnp.min(jnp.where(causal, sc, jnp.inf), axis=0))
    smax = jnp.max(rmax, axis=0, keepdims=True)
    smin = jnp.min(rmin, axis=0, keepdims=True)

    nvalid = (i * TQ + lax.broadcasted_iota(jnp.int32, (1, TQ), 1) + 1).astype(F32)
    needs_thr = nvalid > kf

    def count_ge(v):
        vb = jnp.broadcast_to(v, (8, TQ))[None, None]

        def body(j, acc):
            hit = jnp.where(score_scr[j].reshape(4, KV // 4, 8, TQ) >= vb, 1.0, 0.0)
            return acc + jnp.sum(hit, axis=1)
        acc = lax.fori_loop(0, i + 1, body, jnp.zeros((4, 8, TQ), F32))
        return jnp.sum(jnp.sum(acc, axis=0), axis=0, keepdims=True)

    def logit(c):
        return jnp.log((c + 0.5) / (nvalid - c + 0.5))

    g_k = logit(jnp.full((1, TQ), kf - 0.5, F32))

    def can_split(lo, hi):
        mid = 0.5 * lo + 0.5 * hi
        return (mid > lo) & (mid < hi)

    def n_open(lo, hi, clo, chi):
        open_ = needs_thr & (clo - chi > 1.0) & can_split(lo, hi)
        return jnp.sum(jnp.where(open_, 1.0, 0.0))

    def search_step(st):
        lo, hi, clo, chi, slo, shi, _, it = st
        g_lo = logit(clo)
        g_hi = logit(chi)
        phi = jnp.clip((g_lo - g_k) / (g_lo - g_hi), 0.02, 0.98)
        stalled = (slo >= 2.0) | (shi >= 2.0)
        half = 0.5 * lo + 0.5 * hi
        mid = jnp.where(stalled, half, lo + (hi - lo) * phi)
        mid = jnp.where((mid > lo) & (mid < hi), mid, half)
        c = count_ge(mid)
        ge = c >= kf
        lo = jnp.where(ge, mid, lo)
        clo = jnp.where(ge, c, clo)
        hi = jnp.where(ge, hi, mid)
        chi = jnp.where(ge, chi, c)
        slo = jnp.where(stalled | ge, 0.0, slo + 1.0)
        shi = jnp.where(stalled | jnp.logical_not(ge), 0.0, shi + 1.0)
        return lo, hi, clo, chi, slo, shi, n_open(lo, hi, clo, chi), it + 1

    span = smax - smin
    hi0 = smax + jnp.maximum(jnp.maximum(jnp.abs(smax) * 1e-6, span * 1e-3), 1e-30)
    zero = jnp.zeros((1, TQ), F32)
    st0 = (smin, hi0, nvalid, zero, zero, zero, n_open(smin, hi0, nvalid, zero), jnp.int32(0))
    lo, _, clo, chi, _, _, _, _ = lax.while_loop(
        lambda st: jnp.logical_and(st[6] > 0.0, st[7] < SEARCH_CAP), search_step, st0)

    lob = jnp.broadcast_to(lo, (8, TQ))[None]

    def min_body(j, acc):
        s = score_scr[j]
        return jnp.minimum(acc, jnp.min(jnp.where(s >= lob, s, jnp.inf), axis=0))
    thr = jnp.min(lax.fori_loop(0, i + 1, min_body, jnp.full((8, TQ), jnp.inf, F32)), axis=0, keepdims=True)

    tie = needs_thr & (clo > kf)
    n_tie = jnp.sum(jnp.where(tie, 1.0, 0.0))

    @pl.when(n_tie > 0.0)
    def _():
        r2 = lax.broadcasted_iota(jnp.int32, (TQ, TQ), 0)
        c2 = lax.broadcasted_iota(jnp.int32, (TQ, TQ), 1)
        lower = jnp.where(c2 <= r2, 1.0, 0.0).astype(BF16)
        thrb = jnp.broadcast_to(thr, (TQ, TQ))
        tieb = jnp.broadcast_to(tie, (TQ, TQ))
        needb = jnp.broadcast_to(kf - chi, (TQ, TQ))

        def body(j, carry):
            s = score_scr[j].reshape(TQ, TQ)
            eq = (s == thrb) & tieb
            pref = _dot(lower, jnp.where(eq, 1.0, 0.0).astype(BF16)) + carry
            score_scr[j] = jnp.where(eq & (pref > needb), -jnp.inf, s).reshape(KV, 8, TQ)
            return jnp.broadcast_to(pref[TQ - 1:TQ, :], (TQ, TQ))
        lax.fori_loop(0, i + 1, body, jnp.zeros((TQ, TQ), F32))

    thr_eff = jnp.broadcast_to(jnp.where(needs_thr, thr, LOWEST), (8, TQ))[None]

    acc_scr[...] = jnp.zeros(acc_scr.shape, F32)
    qa2 = qabs_ref[0].reshape(H_ATT * TQ, KV_LATENT)

    def attend(j, near, m, l):
        kv = ckv_ref[0, pl.ds(pl.multiple_of(j * TQ, TQ), TQ), :]
        logits = _dot_nt(kv, qa2)
        sel = score_scr[j] >= thr_eff
        m_out, l_out, alphas = [], [], []
        for h in range(H_ATT):
            lg = logits[:, h * TQ:(h + 1) * TQ].reshape(KV, 8, TQ)
            if near is not None:
                lg = lg + bias_ref[near, h].reshape(KV, 8, TQ)
            lg = jnp.where(sel, lg, NEG_BIG)
            m_new = jnp.maximum(m[h], _key_reduce(lg, jnp.max))
            alpha = jnp.exp(m[h] - m_new)
            p = jnp.exp(lg - jnp.broadcast_to(m_new, (8, TQ))[None])
            l_out.append(alpha * l[h] + _key_reduce(p, jnp.sum))
            m_out.append(m_new)
            alphas.append(alpha)
            p_scr[:, h * TQ:(h + 1) * TQ] = p.reshape(TQ, TQ).astype(BF16)
        pv = _dot(ckvt_ref[0, j], p_scr[...])
        acc_scr[...] = jnp.concatenate(alphas, axis=1) * acc_scr[...] + pv
        return tuple(m_out), tuple(l_out)

    m0 = tuple(jnp.full((1, TQ), NEG_BIG, F32) for _ in range(H_ATT))
    l0 = tuple(jnp.zeros((1, TQ), F32) for _ in range(H_ATT))
    m, l = lax.fori_loop(0, jnp.maximum(i - 1, 0), lambda j, c: attend(j, None, *c), (m0, l0))
    m, l = lax.cond(i >= 1, lambda c: attend(i - 1, 1, *c), lambda c: c, (m, l))
    m, l = attend(i, 0, m, l)

    yt = []
    for h in range(H_ATT):
        o = acc_scr[:, h * TQ:(h + 1) * TQ] / l[h]
        yt.append(_dot(wuvt_ref[h], o.astype(BF16)))
    for g in range(ATT_WIDTH // TQ):
        per = TQ // HD_ATT
        y_ref[0, :, g * TQ:(g + 1) * TQ] = jnp.concatenate(yt[g * per:(g + 1) * per], axis=0).T


def _dsa_call(iq, iwt, qabs, ik, ckv, ckvt, bias_tiles, wuvt, topk):
    B, _, T, _ = iq.shape
    NT = T // TQ
    return pl.pallas_call(
        functools.partial(_dsa_kernel, topk=topk),
        grid=(B, NT),
        in_specs=[pl.BlockSpec((1, IDX_HEADS, TQ, IDX_DIM), lambda b, i: (b, 0, i, 0)),
                  pl.BlockSpec((1, IDX_HEADS, TQ), lambda b, i: (b, 0, i)),
                  pl.BlockSpec((1, H_ATT, TQ, KV_LATENT), lambda b, i: (b, 0, i, 0)),
                  pl.BlockSpec((1, T, IDX_DIM), lambda b, i: (b, 0, 0)),
                  pl.BlockSpec((1, T, KV_LATENT), lambda b, i: (b, 0, 0)),
                  pl.BlockSpec((1, NT, KV_LATENT, TQ), lambda b, i: (b, 0, 0, 0)),
                  pl.BlockSpec((2, H_ATT, TQ, TQ), lambda b, i: (0, 0, 0, 0)),
                  pl.BlockSpec((H_ATT, HD_ATT, KV_LATENT), lambda b, i: (0, 0, 0))],
        out_specs=pl.BlockSpec((1, TQ, ATT_WIDTH), lambda b, i: (b, i, 0)),
        out_shape=jax.ShapeDtypeStruct((B, T, ATT_WIDTH), F32),
        scratch_shapes=[pltpu.VMEM((NT, KV, 8, TQ), F32),
                        pltpu.VMEM((TQ, H_ATT * TQ), BF16),
                        pltpu.VMEM((KV_LATENT, H_ATT * TQ), F32)],
        compiler_params=_cp(("parallel", "arbitrary")),
        name="dsa",
    )(iq, iwt, qabs, ik, ckv, ckvt, bias_tiles, wuvt)


def _rwkv_kernel(p_ref, prev_ref, mu_ref, w0_ref, wdec_ref, a0_ref, waaa_ref, wgate_ref,
                 kk_ref, ka_ref, rk_ref, lng_ref, lnb_ref, y_ref, h_scr):
    c = pl.program_id(1)
    C = CHUNK
    N = HD_RWKV

    @pl.when(c == 0)
    def _():
        h_scr[...] = jnp.zeros(h_scr.shape, F32)

    p = p_ref[0]
    prev_row = jnp.where(c > 0, prev_ref[0, 7:8, :], 0.0)
    rid = lax.broadcasted_iota(jnp.int32, (C, 1), 0)
    p_prev = jnp.where(rid == 0, prev_row, pltpu.roll(p, 1, 0))
    p = p + mu_ref[...] * (p_prev - p)

    W = RWKV_WIDTH
    r = p[:, 0:W]
    k = p[:, W:2 * W]
    v = p[:, 2 * W:3 * W]
    wd = p[:, OFF_WD:OFF_WD + DECAY_LORA]
    ad = p[:, OFF_AD:OFF_AD + AAA_LORA]
    gd = p[:, OFF_GD:OFF_GD + GATE_PAD]

    w_log = -jax.nn.softplus(-(w0_ref[...] + _dot(jnp.tanh(wd).astype(BF16), wdec_ref[...]))) - 0.5
    logw = -jnp.exp(w_log)
    a = jax.nn.sigmoid(a0_ref[...] + _dot(ad.astype(BF16), waaa_ref[...]))
    g = _dot(jax.nn.sigmoid(gd).astype(BF16), wgate_ref[...])

    r64 = lax.broadcasted_iota(jnp.int32, (C, C), 0)
    c64 = lax.broadcasted_iota(jnp.int32, (C, C), 1)
    tril_incl = r64 >= c64
    tril_strict = r64 > c64
    eye = r64 == c64
    cum = jnp.dot(jnp.where(tril_incl, 1.0, 0.0), logw, preferred_element_type=F32,
                  precision=lax.Precision.HIGHEST)
    cum_last = cum[C - 1:C, :]
    g_in = jnp.exp(cum - logw)
    g_t = jnp.exp(cum)
    g_inv = jnp.exp(-cum)
    g_end = jnp.exp(cum_last - cum)
    g_all = jnp.exp(cum_last)

    kkf = k * kk_ref[...]
    kmod = k * (1.0 + (a - 1.0) * ka_ref[...])
    rkr = r * kmod * rk_ref[...]

    H = range(H_RWKV)
    sls = [slice(h * N, (h + 1) * N) for h in H]
    lane2 = lax.broadcasted_iota(jnp.int32, (C, 2 * C), 1)
    row2 = lax.broadcasted_iota(jnp.int32, (C, 2 * C), 0)
    in_left = lane2 < C
    col2 = jnp.where(in_left, lane2, lane2 - C)
    strict2 = row2 > col2
    incl2 = row2 >= col2
    zpad = jnp.zeros((C, C), F32)

    at, rt, bh, kh, vb, g4 = [], [], [], [], [], []
    for h in H:
        sl = sls[h]
        kk_h = kkf[:, sl]
        nrm = jnp.sqrt(jnp.sum(kk_h * kk_h, axis=-1, keepdims=True))
        kk_h = kk_h / jnp.maximum(nrm, 1e-12)
        k_h = kmod[:, sl]
        bvec = kk_h * a[:, sl]
        at_h = (-kk_h) * g_in[:, sl]
        rt_h = r[:, sl] * g_t[:, sl]
        lhs = jnp.concatenate([at_h, rt_h], axis=0).astype(BF16)
        rhs = jnp.concatenate([bvec * g_inv[:, sl], k_h * g_inv[:, sl]], axis=0).astype(BF16)
        g4.append(_dot_nt(lhs, rhs))
        at.append(at_h)
        rt.append(rt_h.astype(BF16))
        bh.append((bvec * g_end[:, sl]).astype(BF16))
        kh.append((k_h * g_end[:, sl]).astype(BF16))
        vb.append(v[:, sl].astype(BF16))

    a_ab = [jnp.where(tril_strict, g4[h][0:C, 0:C], 0.0) for h in H]
    tinv = [jnp.where(eye, 1.0, 0.0) + a_ab[h] for h in H]
    pw = a_ab
    for _ in range(5):
        pwb = [pw[h].astype(BF16) for h in H]
        pw = [_dot(pwb[h], pwb[h]) for h in H]
        tinv = [tinv[h] + _dot(tinv[h].astype(BF16), pw[h].astype(BF16)) for h in H]

    hst = [h_scr[h] for h in H]
    hb = [hst[h].astype(BF16) for h in H]
    x = []
    for h in H:
        xl = jnp.where(in_left, jnp.concatenate([at[h], zpad], axis=1),
                       jnp.where(strict2, g4[h][0:C, :], 0.0)).astype(BF16)
        x.append(_dot(xl, jnp.concatenate([hb[h], vb[h]], axis=0)))
    ub = [_dot(tinv[h].astype(BF16), x[h].astype(BF16)).astype(BF16) for h in H]
    uv = [jnp.concatenate([ub[h], vb[h]], axis=0) for h in H]
    y = [_dot(rt[h], hb[h]) + _dot(jnp.where(incl2, g4[h][C:2 * C, :], 0.0).astype(BF16), uv[h]) for h in H]
    for h in H:
        g_col = jnp.sum(jnp.where(eye, jnp.broadcast_to(g_all[:, sls[h]], (N, N)), 0.0), axis=1, keepdims=True)
        h_scr[h] = g_col * hst[h] + _dot_tn(jnp.concatenate([bh[h], kh[h]], axis=0), uv[h])

    for h in H:
        sl = sls[h]
        mean = jnp.mean(y[h], axis=-1, keepdims=True)
        yc = y[h] - mean
        var = jnp.mean(yc * yc, axis=-1, keepdims=True)
        yn = yc * lax.rsqrt(var + GN_EPS) * lng_ref[:, sl] + lnb_ref[:, sl]
        bonus = jnp.sum(rkr[:, sl], axis=-1, keepdims=True) * v[:, sl]
        y_ref[0, :, sl] = (yn + bonus) * g[:, sl]


def _rwkv_call(rw, mu_p, w0, wdec, a0, waaa, wgate_p, k_k, k_a, r_k, lng, lnb):
    B, T, _ = rw.shape
    NC = T // CHUNK
    W = RWKV_WIDTH
    c2 = lambda b, c: (0, 0)
    rowspec = pl.BlockSpec((1, W), c2)
    return pl.pallas_call(
        _rwkv_kernel,
        grid=(B, NC),
        in_specs=[pl.BlockSpec((1, CHUNK, RW_COLS), lambda b, c: (b, c, 0)),
                  pl.BlockSpec((1, 8, RW_COLS), lambda b, c: (b, jnp.maximum(c * (CHUNK // 8) - 1, 0), 0)),
                  pl.BlockSpec((1, RW_COLS), c2),
                  rowspec,
                  pl.BlockSpec((DECAY_LORA, W), c2),
                  rowspec,
                  pl.BlockSpec((AAA_LORA, W), c2),
                  pl.BlockSpec((GATE_PAD, W), c2),
                  rowspec, rowspec, rowspec, rowspec, rowspec],
        out_specs=pl.BlockSpec((1, CHUNK, W), lambda b, c: (b, c, 0)),
        out_shape=jax.ShapeDtypeStruct((B, T, W), F32),
        scratch_shapes=[pltpu.VMEM((H_RWKV, HD_RWKV, HD_RWKV), F32)],
        compiler_params=_cp(("parallel", "arbitrary")),
        name="rwkv",
    )(rw, rw, mu_p, w0, wdec, a0, waaa, wgate_p, k_k, k_a, r_k, lng, lnb)


def _ffn_kernel(x_ref, ya_ref, yr_ref, mod_ref, gpost_ref, gpre_ref, gfpost_ref,
                wo_ref, wg_ref, wu_ref, wd_ref, o_ref, x1_scr, hf_scr, acc_scr):
    j = pl.program_id(2)

    @pl.when(j == 0)
    def _():
        mix = (_dot(ya_ref[0].astype(BF16), wo_ref[0:ATT_WIDTH, :])
               + _dot(yr_ref[0].astype(BF16), wo_ref[ATT_WIDTH:, :]))
        x1 = x_ref[0] + mod_ref[0, 2:3, :] * _rms(mix, gpost_ref[...])
        x1_scr[...] = x1
        hf = _rms(x1, gpre_ref[...]) * (1.0 + mod_ref[0, 4:5, :]) + mod_ref[0, 3:4, :]
        hf_scr[...] = hf.astype(BF16)
        acc_scr[...] = jnp.zeros(acc_scr.shape, F32)

    hf = hf_scr[...]
    gate = _dot(hf, wg_ref[...])
    up = _dot(hf, wu_ref[...])
    act = (gate * jax.nn.sigmoid(gate) * up).astype(BF16)
    acc_scr[...] += _dot(act, wd_ref[...])

    @pl.when(j == pl.num_programs(2) - 1)
    def _():
        o_ref[0] = x1_scr[...] + mod_ref[0, 5:6, :] * _rms(acc_scr[...], gfpost_ref[...])


def _ffn_call(x, y_att, y_rwkv, mod3, g_post, g_pre, g_fpost, wo, wg, wu, wd):
    B, T, D = x.shape
    TM = 512
    TF = D_FF // 2
    c3 = lambda b, i, j: (0, 0)
    return pl.pallas_call(
        _ffn_kernel,
        grid=(B, T // TM, D_FF // TF),
        in_specs=[pl.BlockSpec((1, TM, D), lambda b, i, j: (b, i, 0)),
                  pl.BlockSpec((1, TM, ATT_WIDTH), lambda b, i, j: (b, i, 0)),
                  pl.BlockSpec((1, TM, RWKV_WIDTH), lambda b, i, j: (b, i, 0)),
                  pl.BlockSpec((1, 6, D), lambda b, i, j: (b, 0, 0)),
                  pl.BlockSpec((1, D), c3), pl.BlockSpec((1, D), c3), pl.BlockSpec((1, D), c3),
                  pl.BlockSpec((D, D), c3),
                  pl.BlockSpec((D, TF), lambda b, i, j: (0, j)),
                  pl.BlockSpec((D, TF), lambda b, i, j: (0, j)),
                  pl.BlockSpec((TF, D), lambda b, i, j: (j, 0))],
        out_specs=pl.BlockSpec((1, TM, D), lambda b, i, j: (b, i, 0)),
        out_shape=jax.ShapeDtypeStruct((B, T, D), F32),
        scratch_shapes=[pltpu.VMEM((TM, D), F32), pltpu.VMEM((TM, D), BF16), pltpu.VMEM((TM, D), F32)],
        compiler_params=_cp(("parallel", "parallel", "arbitrary")),
        name="ffn",
    )(x, y_att, y_rwkv, mod3, g_post, g_pre, g_fpost, wo, wg, wu, wd)


def _pad_cols(w, n):
    return jnp.pad(w, ((0, 0), (0, n - w.shape[1])))


def _layer(x, mod3, rel_bias, mix_pre_norm, mix_post_norm, ffn_pre_norm, ffn_post_norm,
           w_in, q_norm, w_uq, w_idx_q, kv_norm, idx_k_norm, w_uk, w_uv,
           mu_shift, w0, w_decay_up, a0, w_aaa_up, w_gate_up, k_k, k_a, r_k, ln_x_gain, ln_x_bias,
           w_out, w_ffn_gate, w_ffn_up, w_ffn_down):
    B, T, D = x.shape
    att_cols = Q_LORA + KV_LATENT + IDX_DIM + IDX_HEADS
    n_main = 3 * RWKV_WIDTH + DECAY_LORA + AAA_LORA
    w_att = _pad_cols(w_in[:, :att_cols], ATT_PAD)
    w_rw = w_in[:, att_cols:]
    w_in_p = jnp.concatenate([w_att, w_rw[:, :n_main], _pad_cols(w_rw[:, n_main:], GATE_PAD)], axis=1).astype(BF16)
    mu_p = jnp.concatenate([mu_shift[:n_main], jnp.pad(mu_shift[n_main:], (0, GATE_PAD - GATE_LORA))]).reshape(1, RW_COLS)
    wgate_p = jnp.pad(w_gate_up, ((0, GATE_PAD - GATE_LORA), (0, 0))).astype(BF16)

    att, rw = _inproj_call(x, mod3, mix_pre_norm.reshape(1, D), w_in_p)

    wuq = w_uq.reshape(Q_LORA, ATT_WIDTH).astype(BF16)
    wiq = w_idx_q.reshape(Q_LORA, IDX_HEADS * IDX_DIM).astype(BF16)
    wukt = jnp.transpose(w_uk, (0, 2, 1)).astype(BF16)
    qabs, iq, ik, ckv, ckvt, iwt = _dsaprep_call(att, q_norm.reshape(1, -1), kv_norm.reshape(1, -1),
                                                 idx_k_norm.reshape(1, -1), wuq, wiq, wukt)
    bias_tiles = _bias_call(rel_bias)
    wuvt = jnp.transpose(w_uv, (0, 2, 1)).astype(BF16)
    topk = min(TOPK_MAX, T // 4)
    y_att = _dsa_call(iq, iwt, qabs, ik, ckv, ckvt, bias_tiles, wuvt, topk)

    row = lambda z: z.reshape(1, RWKV_WIDTH)
    y_rwkv = _rwkv_call(rw, mu_p, row(w0), w_decay_up.astype(BF16), row(a0), w_aaa_up.astype(BF16),
                        wgate_p, row(k_k), row(k_a), row(r_k), row(ln_x_gain), row(ln_x_bias))

    return _ffn_call(x, y_att, y_rwkv, mod3, mix_post_norm.reshape(1, D), ffn_pre_norm.reshape(1, D),
                     ffn_post_norm.reshape(1, D), w_out.astype(BF16), w_ffn_gate.astype(BF16),
                     w_ffn_up.astype(BF16), w_ffn_down.astype(BF16))


def kernel(x, c, rel_bias, ada_w, ada_b, mix_pre_norm, mix_post_norm, ffn_pre_norm, ffn_post_norm, w_in, q_norm, w_uq, w_idx_q, kv_norm, idx_k_norm, w_uk, w_uv, mu_shift, w0, w_decay_up, a0, w_aaa_up, w_gate_up, k_k, k_a, r_k, ln_x_gain, ln_x_bias, w_out, w_ffn_gate, w_ffn_up, w_ffn_down):
    B, T, D = x.shape
    assert D == D_MODEL and T % 512 == 0 and T // 4 >= 1
    layer_params = (mix_pre_norm, mix_post_norm, ffn_pre_norm, ffn_post_norm,
                    w_in, q_norm, w_uq, w_idx_q, kv_norm, idx_k_norm, w_uk, w_uv,
                    mu_shift, w0, w_decay_up, a0, w_aaa_up, w_gate_up, k_k, k_a, r_k, ln_x_gain, ln_x_bias,
                    w_out, w_ffn_gate, w_ffn_up, w_ffn_down)
    for l in range(ada_w.shape[0]):
        mod3 = _mod_call(c, ada_w[l], ada_b[l]).reshape(B, 6, D)
        x = _layer(x, mod3, rel_bias, *[p[l] for p in layer_params])
    return x
```

```python
import functools
import math

import numpy as np
import jax
import jax.numpy as jnp
from jax import lax
from jax.experimental import pallas as pl
from jax.experimental.pallas import tpu as pltpu

F32 = jnp.float32
BF16 = jnp.bfloat16

D_MODEL = 1024
ATT_WIDTH = 512
RWKV_WIDTH = 512
HD_ATT = 64
H_ATT = 8
HD_RWKV = 64
H_RWKV = 8
Q_LORA = 256
KV_LATENT = 128
IDX_HEADS = 8
IDX_DIM = 64
TOPK_MAX = 256
NUM_BUCKETS = 32
MAX_DISTANCE = 128
DECAY_LORA = 64
AAA_LORA = 64
GATE_LORA = 160
GN_EPS = 64e-5
D_FF = 2816
NORM_EPS = 1e-6

ATT_PAD = 512
GATE_PAD = 256
RW_COLS = 3 * RWKV_WIDTH + DECAY_LORA + AAA_LORA + GATE_PAD
OFF_WD = 3 * RWKV_WIDTH
OFF_AD = OFF_WD + DECAY_LORA
OFF_GD = OFF_AD + AAA_LORA

TQ = 128
CHUNK = 64
NEG_BIG = -1e30
LOWEST = -3.0e38
VMEM_LIMIT = 56 * 1024 * 1024


def _cp(sem):
    return pltpu.CompilerParams(dimension_semantics=sem, vmem_limit_bytes=VMEM_LIMIT)


def _dot(a, b):
    return jnp.dot(a, b, preferred_element_type=F32)


def _dot_nt(a, b):
    return lax.dot_general(a, b, (((1,), (1,)), ((), ())), preferred_element_type=F32)


def _dot_tn(a, b):
    return lax.dot_general(a, b, (((0,), (0,)), ((), ())), preferred_element_type=F32)


def _rms(z, gain):
    return z * lax.rsqrt(jnp.mean(z * z, axis=-1, keepdims=True) + NORM_EPS) * gain


def _mod_kernel(c_ref, w_ref, b_ref, o_ref):
    c = c_ref[...]
    s = c * jax.nn.sigmoid(c)
    o_ref[...] = _dot(s.astype(BF16), w_ref[...].astype(BF16)) + b_ref[...]


def _mod_call(c, ada_w, ada_b):
    B, D = c.shape
    N = ada_w.shape[1]
    TN = 1536
    return pl.pallas_call(
        _mod_kernel,
        grid=(N // TN,),
        in_specs=[pl.BlockSpec((B, D), lambda j: (0, 0)),
                  pl.BlockSpec((D, TN), lambda j: (0, j)),
                  pl.BlockSpec((1, TN), lambda j: (0, j))],
        out_specs=pl.BlockSpec((B, TN), lambda j: (0, j)),
        out_shape=jax.ShapeDtypeStruct((B, N), F32),
        compiler_params=_cp(("parallel",)),
        name="mod",
    )(c, ada_w, ada_b.reshape(1, N))


def _inproj_kernel(x_ref, mod_ref, g_ref, w_ref, att_ref, rw_ref):
    x = x_ref[0]
    y = _rms(x, g_ref[...])
    sh = mod_ref[0, 0:1, :]
    sc = mod_ref[0, 1:2, :]
    h = (y * (1.0 + sc) + sh).astype(BF16)
    att_ref[0] = _dot(h, w_ref[:, 0:ATT_PAD])
    for n0 in range(0, RW_COLS, 384):
        rw_ref[0, :, n0:n0 + 384] = _dot(h, w_ref[:, ATT_PAD + n0:ATT_PAD + n0 + 384])


def _inproj_call(x, mod3, gain, w_in_p):
    B, T, D = x.shape
    TM = 512
    NP = w_in_p.shape[1]
    return pl.pallas_call(
        _inproj_kernel,
        grid=(B, T // TM),
        in_specs=[pl.BlockSpec((1, TM, D), lambda b, i: (b, i, 0)),
                  pl.BlockSpec((1, 6, D), lambda b, i: (b, 0, 0)),
                  pl.BlockSpec((1, D), lambda b, i: (0, 0)),
                  pl.BlockSpec((D, NP), lambda b, i: (0, 0))],
        out_specs=[pl.BlockSpec((1, TM, ATT_PAD), lambda b, i: (b, i, 0)),
                   pl.BlockSpec((1, TM, RW_COLS), lambda b, i: (b, i, 0))],
        out_shape=[jax.ShapeDtypeStruct((B, T, ATT_PAD), F32),
                   jax.ShapeDtypeStruct((B, T, RW_COLS), F32)],
        compiler_params=_cp(("parallel", "parallel")),
        name="inproj",
    )(x, mod3, gain, w_in_p)


def _dsaprep_kernel(att_ref, qn_ref, kvn_ref, ikn_ref, wuq_ref, wiq_ref, wukt_ref,
                    qabs_ref, iq_ref, ik_ref, ckv_ref, ckvt_ref, iwt_ref):
    att = att_ref[0]
    tm = att.shape[0]
    o1 = Q_LORA
    o2 = o1 + KV_LATENT
    o3 = o2 + IDX_DIM
    cq = _rms(att[:, 0:o1], qn_ref[...]).astype(BF16)
    ckv = _rms(att[:, o1:o2], kvn_ref[...])
    ckv_ref[0] = ckv.astype(BF16)
    ik_ref[0] = _rms(att[:, o2:o3], ikn_ref[...]).astype(BF16)
    tail = att[:, o2:ATT_PAD]
    w_off = o3 - o2
    for sb in range(tm // TQ):
        rows = slice(sb * TQ, (sb + 1) * TQ)
        ckvt_ref[0, sb] = ckv[rows, :].T.astype(BF16)
        iwt_ref[0, :, rows] = tail[rows, :].T[w_off:w_off + IDX_HEADS, :] * (IDX_HEADS ** -0.5 * IDX_DIM ** -0.5)
    q = _dot(cq, wuq_ref[...])
    iqv = _dot(cq, wiq_ref[...])
    for h in range(H_ATT):
        qh = q[:, h * HD_ATT:(h + 1) * HD_ATT].astype(BF16)
        qabs_ref[0, h] = (_dot(qh, wukt_ref[h]) * (HD_ATT ** -0.5)).astype(BF16)
        iq_ref[0, h] = iqv[:, h * IDX_DIM:(h + 1) * IDX_DIM].astype(BF16)


def _dsaprep_call(att, q_norm, kv_norm, idx_k_norm, wuq, wiq, wukt):
    B, T, _ = att.shape
    TM = 512
    const2 = lambda b, i: (0, 0)
    return pl.pallas_call(
        _dsaprep_kernel,
        grid=(B, T // TM),
        in_specs=[pl.BlockSpec((1, TM, ATT_PAD), lambda b, i: (b, i, 0)),
                  pl.BlockSpec((1, Q_LORA), const2),
                  pl.BlockSpec((1, KV_LATENT), const2),
                  pl.BlockSpec((1, IDX_DIM), const2),
                  pl.BlockSpec((Q_LORA, ATT_WIDTH), const2),
                  pl.BlockSpec((Q_LORA, IDX_HEADS * IDX_DIM), const2),
                  pl.BlockSpec((H_ATT, HD_ATT, KV_LATENT), lambda b, i: (0, 0, 0))],
        out_specs=[pl.BlockSpec((1, H_ATT, TM, KV_LATENT), lambda b, i: (b, 0, i, 0)),
                   pl.BlockSpec((1, IDX_HEADS, TM, IDX_DIM), lambda b, i: (b, 0, i, 0)),
                   pl.BlockSpec((1, TM, IDX_DIM), lambda b, i: (b, i, 0)),
                   pl.BlockSpec((1, TM, KV_LATENT), lambda b, i: (b, i, 0)),
                   pl.BlockSpec((1, TM // TQ, KV_LATENT, TQ), lambda b, i: (b, i, 0, 0)),
                   pl.BlockSpec((1, IDX_HEADS, TM), lambda b, i: (b, 0, i))],
        out_shape=[jax.ShapeDtypeStruct((B, H_ATT, T, KV_LATENT), BF16),
                   jax.ShapeDtypeStruct((B, IDX_HEADS, T, IDX_DIM), BF16),
                   jax.ShapeDtypeStruct((B, T, IDX_DIM), BF16),
                   jax.ShapeDtypeStruct((B, T, KV_LATENT), BF16),
                   jax.ShapeDtypeStruct((B, T // TQ, KV_LATENT, TQ), BF16),
                   jax.ShapeDtypeStruct((B, IDX_HEADS, T), F32)],
        compiler_params=_cp(("parallel", "parallel")),
        name="dsaprep",
    )(att, q_norm, kv_norm, idx_k_norm, wuq, wiq, wukt)


def _t5_bucket_np(rel):
    max_exact = NUM_BUCKETS // 2
    nf = np.maximum(rel, 1).astype(np.float32)
    large = max_exact + (np.log(nf / max_exact) / math.log(MAX_DISTANCE / max_exact)
                         * (NUM_BUCKETS - max_exact)).astype(np.int32)
    large = np.minimum(large, NUM_BUCKETS - 1)
    return np.where(rel < max_exact, rel, large).astype(np.int32)


def _near_bucket_tiles():
    s = np.arange(TQ)[:, None]
    t = np.arange(TQ)[None, :]
    tiles = [_t5_bucket_np(np.maximum(t - s + TQ * d, 0)) for d in range(2)]
    assert _t5_bucket_np(np.array([TQ + 1]))[0] == NUM_BUCKETS - 1
    return np.stack(tiles)


def _bias_kernel(rb_ref, bk_ref, o_ref):
    for d in range(2):
        bk = bk_ref[d]
        for h in range(H_ATT):
            def body(b, acc):
                return jnp.where(bk == b, rb_ref[b, h], acc)
            acc = lax.fori_loop(0, NUM_BUCKETS, body, jnp.zeros((TQ, TQ), F32))
            o_ref[d, h] = acc - rb_ref[NUM_BUCKETS - 1, h]
    o_ref[2] = jnp.zeros((H_ATT, TQ, TQ), F32)


def _bias_call(rel_bias):
    buckets = jnp.asarray(_near_bucket_tiles())
    return pl.pallas_call(
        _bias_kernel,
        in_specs=[pl.BlockSpec(memory_space=pltpu.SMEM),
                  pl.BlockSpec(memory_space=pltpu.VMEM)],
        out_specs=pl.BlockSpec(memory_space=pltpu.VMEM),
        out_shape=jax.ShapeDtypeStruct((3, H_ATT, TQ, TQ), F32),
        name="bias",
    )(rel_bias, buckets)


KV = TQ // 8


def _key_reduce(x, op):
    return op(op(x, axis=0), axis=0, keepdims=True)


def _dsa_kernel(iq_ref, iwt_ref, qabs_ref, ik_ref, ckv_ref, ckvt_ref, bias_ref, wuvt_ref, y_ref,
                score_scr, stage_a, stage_b, pv_a, pv_b, alpha_a, alpha_b, acc_scr, *, topk):
    i = pl.program_id(1)
    kf = float(topk)
    kidx = (lax.broadcasted_iota(jnp.int32, (KV, 8, TQ), 0) * 8
            + lax.broadcasted_iota(jnp.int32, (KV, 8, TQ), 1))
    qidx = lax.broadcasted_iota(jnp.int32, (KV, 8, TQ), 2)
    n_pairs = i // 2 + 1

    iwt = iwt_ref[0]
    w8 = [jnp.broadcast_to(iwt[h:h + 1, :], (8, TQ))[None] for h in range(IDX_HEADS)]
    iq2 = iq_ref[0].reshape(IDX_HEADS * TQ, IDX_DIM)

    n_trips = (n_pairs + 1) // 2
    stage = (stage_a, stage_b)

    def stage_pair(q, lhs_ref, rhs, dst):
        nxt = jnp.minimum(q, n_pairs - 1)
        rows = lhs_ref[0, pl.ds(pl.multiple_of(2 * nxt * TQ, 2 * TQ), 2 * TQ), :]
        dst[...] = _dot_nt(rows, rhs).reshape(2, TQ, H_ATT * TQ)

    def two_halves(half):
        def trip(r, carry):
            carry = half(2 * r, 0, carry)
            return lax.cond(2 * r + 1 < n_pairs, lambda c: half(2 * r + 1, 1, c), lambda c: c, carry)
        return trip

    def score_half(q, par, carry):
        rmax, rmin = carry
        stage_pair(q + 1, ik_ref, iq2, stage[1 - par])
        for blk in range(2):
            j = 2 * q + blk
            sc = jnp.zeros((KV, 8, TQ), F32)
            for h in range(IDX_HEADS):
                s = stage[par][blk, :, h * TQ:(h + 1) * TQ].reshape(KV, 8, TQ)
                sc = sc + jnp.maximum(s, 0.0) * w8[h]
            causal = kidx + (j - i) * TQ <= qidx
            score_scr[j] = jnp.where(causal, sc, -jnp.inf)
            rmax = jnp.maximum(rmax, jnp.max(jnp.where(causal, sc, -jnp.inf), axis=0))
            rmin = jnp.minimum(rmin, jnp.min(jnp.where(causal, sc, jnp.inf), axis=0))
        return rmax, rmin

    stage_pair(0, ik_ref, iq2, stage[0])
    rmax, rmin = lax.fori_loop(0, n_trips, two_halves(score_half),
                               (jnp.full((8, TQ), -jnp.inf, F32), jnp.full((8, TQ), jnp.inf, F32)))
    smax = jnp.max(rmax, axis=0, keepdims=True)
    smin = jnp.min(rmin, axis=0, keepdims=True)

    nvalid = (i * TQ + lax.broadcasted_iota(jnp.int32, (1, TQ), 1) + 1).astype(F32)
    needs_thr = nvalid > kf

    def probe(v):
        vb = jnp.broadcast_to(v, (8, TQ))[None, None]

        def body(q, st):
            cnt, dn, up = st
            for j in (2 * q, 2 * q + 1):
                s = score_scr[j].reshape(4, KV // 4, 8, TQ)
                ge = s >= vb
                cnt = cnt + jnp.sum(jnp.where(ge, 1.0, 0.0), axis=1)
                dn = jnp.minimum(dn, jnp.min(jnp.where(ge, s, jnp.inf), axis=1))
                up = jnp.maximum(up, jnp.max(jnp.where(ge, -jnp.inf, s), axis=1))
            return cnt, dn, up
        cnt, dn, up = lax.fori_loop(0, n_pairs, body, (jnp.zeros((4, 8, TQ), F32),
                                                       jnp.full((4, 8, TQ), jnp.inf, F32),
                                                       jnp.full((4, 8, TQ), -jnp.inf, F32)))
        return (jnp.sum(jnp.sum(cnt, axis=0), axis=0, keepdims=True),
                jnp.min(jnp.min(dn, axis=0), axis=0, keepdims=True),
                jnp.max(jnp.max(up, axis=0), axis=0, keepdims=True))

    def logit(c):
        return jnp.log((c + 0.5) / (nvalid - c + 0.5))

    g_k = logit(jnp.full((1, TQ), kf - 0.5, F32))

    def search_step(st):
        lo, hi, clo, ghi, slo, shi, _, it = st
        is_open = lo < hi
        g_lo = logit(clo)
        g_hi = logit(ghi)
        phi = jnp.clip((g_lo - g_k) / (g_lo - g_hi), 0.02, 0.98)
        stalled = (slo >= 2.0) | (shi >= 2.0)
        mid = jnp.where(stalled, 0.5 * lo + 0.5 * hi, lo + (hi - lo) * phi)
        mid = jnp.where((mid > lo) & (mid <= hi), mid, hi)
        c, dn, up = probe(mid)
        ge = c >= kf
        move_lo = is_open & ge
        move_hi = is_open & jnp.logical_not(ge)
        lo = jnp.where(move_lo, dn, lo)
        clo = jnp.where(move_lo, c, clo)
        hi = jnp.where(move_hi, up, hi)
        ghi = jnp.where(move_hi, c, ghi)
        slo = jnp.where(stalled | ge, 0.0, slo + 1.0)
        shi = jnp.where(stalled | jnp.logical_not(ge), 0.0, shi + 1.0)
        return lo, hi, clo, ghi, slo, shi, jnp.sum(jnp.where(lo < hi, 1.0, 0.0)), it + 1

    search_cap = ik_ref.shape[1] + 2
    zero = jnp.zeros((1, TQ), F32)
    hi0 = jnp.where(needs_thr, smax, smin)
    st0 = (smin, hi0, nvalid, zero, zero, zero, jnp.sum(jnp.where(smin < hi0, 1.0, 0.0)), jnp.int32(0))
    thr, _, clo, chi, _, _, _, _ = lax.while_loop(
        lambda st: jnp.logical_and(st[6] > 0.0, st[7] < search_cap), search_step, st0)

    tie = needs_thr & (clo > kf)
    n_tie = jnp.sum(jnp.where(tie, 1.0, 0.0))

    @pl.when(n_tie > 0.0)
    def _():
        r2 = lax.broadcasted_iota(jnp.int32, (TQ, TQ), 0)
        c2 = lax.broadcasted_iota(jnp.int32, (TQ, TQ), 1)
        lower = jnp.where(c2 <= r2, 1.0, 0.0).astype(BF16)
        thrb = jnp.broadcast_to(thr, (TQ, TQ))
        tieb = jnp.broadcast_to(tie, (TQ, TQ))
        needb = jnp.broadcast_to(kf - chi, (TQ, TQ))

        def body(j, carry):
            s = score_scr[j].reshape(TQ, TQ)
            eq = (s == thrb) & tieb
            pref = _dot(lower, jnp.where(eq, 1.0, 0.0).astype(BF16)) + carry
            score_scr[j] = jnp.where(eq & (pref > needb), -jnp.inf, s).reshape(KV, 8, TQ)
            return jnp.broadcast_to(pref[TQ - 1:TQ, :], (TQ, TQ))
        lax.fori_loop(0, i + 1, body, jnp.zeros((TQ, TQ), F32))

    thr_eff = jnp.broadcast_to(jnp.where(needs_thr, thr, LOWEST), (8, TQ))[None]

    acc_scr[...] = jnp.zeros(acc_scr.shape, F32)
    qa2 = qabs_ref[0].reshape(H_ATT * TQ, KV_LATENT)

    pv = (pv_a, pv_b)
    alpha_st = (alpha_a, alpha_b)

    def apply_pending(par):
        acc_scr[...] = alpha_st[par][...] * acc_scr[...] + pv[par][...]

    def attend_half(q, par, carry):
        m, l = list(carry[0]), list(carry[1])
        stage_pair(q + 1, ckv_ref, qa2, stage[1 - par])
        apply_pending(1 - par)
        js = (2 * q, 2 * q + 1)
        sel = [score_scr[j] >= thr_eff for j in js]
        d = [jnp.clip(i - j, 0, 2) for j in js]
        ps, alphas = [], []
        for h in range(H_ATT):
            lg = [jnp.where(sel[b], (stage[par][b, :, h * TQ:(h + 1) * TQ] + bias_ref[d[b], h]).reshape(KV, 8, TQ),
                            NEG_BIG) for b in range(2)]
            m_new = jnp.maximum(m[h], jnp.maximum(_key_reduce(lg[0], jnp.max), _key_reduce(lg[1], jnp.max)))
            alpha = jnp.exp(m[h] - m_new)
            mb = jnp.broadcast_to(m_new, (8, TQ))[None]
            p = [jnp.exp(lg[b] - mb) for b in range(2)]
            l[h] = alpha * l[h] + _key_reduce(p[0], jnp.sum) + _key_reduce(p[1], jnp.sum)
            m[h] = m_new
            alphas.append(alpha)
            ps.append(jnp.concatenate([p[0].reshape(TQ, TQ), p[1].reshape(TQ, TQ)], axis=0).astype(BF16))
        kvt = jnp.concatenate([ckvt_ref[0, js[0]], ckvt_ref[0, js[1]]], axis=1)
        pv[par][...] = _dot(kvt, jnp.concatenate(ps, axis=1))
        alpha_st[par][...] = jnp.concatenate(alphas, axis=1)
        return tuple(m), tuple(l)

    pv_b[...] = jnp.zeros(pv_b.shape, F32)
    alpha_b[...] = jnp.ones(alpha_b.shape, F32)
    stage_pair(0, ckv_ref, qa2, stage[0])
    m0 = tuple(jnp.full((1, TQ), NEG_BIG, F32) for _ in range(H_ATT))
    l0 = tuple(jnp.zeros((1, TQ), F32) for _ in range(H_ATT))
    m, l = lax.fori_loop(0, n_trips, two_halves(attend_half), (m0, l0))
    for par in range(2):
        @pl.when((n_pairs - 1) % 2 == par)
        def _():
            apply_pending(par)

    yt = []
    for h in range(H_ATT):
        o = acc_scr[:, h * TQ:(h + 1) * TQ] / l[h]
        yt.append(_dot(wuvt_ref[h], o.astype(BF16)))
    for g in range(ATT_WIDTH // TQ):
        per = TQ // HD_ATT
        y_ref[0, :, g * TQ:(g + 1) * TQ] = jnp.concatenate(yt[g * per:(g + 1) * per], axis=0).T


def _dsa_call(iq, iwt, qabs, ik, ckv, ckvt, bias_tiles, wuvt, topk):
    B, _, T, _ = iq.shape
    NT = T // TQ
    return pl.pallas_call(
        functools.partial(_dsa_kernel, topk=topk),
        grid=(B, NT),
        in_specs=[pl.BlockSpec((1, IDX_HEADS, TQ, IDX_DIM), lambda b, i: (b, 0, i, 0)),
                  pl.BlockSpec((1, IDX_HEADS, TQ), lambda b, i: (b, 0, i)),
                  pl.BlockSpec((1, H_ATT, TQ, KV_LATENT), lambda b, i: (b, 0, i, 0)),
                  pl.BlockSpec((1, T, IDX_DIM), lambda b, i: (b, 0, 0)),
                  pl.BlockSpec((1, T, KV_LATENT), lambda b, i: (b, 0, 0)),
                  pl.BlockSpec((1, NT, KV_LATENT, TQ), lambda b, i: (b, 0, 0, 0)),
                  pl.BlockSpec((3, H_ATT, TQ, TQ), lambda b, i: (0, 0, 0, 0)),
                  pl.BlockSpec((H_ATT, HD_ATT, KV_LATENT), lambda b, i: (0, 0, 0))],
        out_specs=pl.BlockSpec((1, TQ, ATT_WIDTH), lambda b, i: (b, i, 0)),
        out_shape=jax.ShapeDtypeStruct((B, T, ATT_WIDTH), F32),
        scratch_shapes=[pltpu.VMEM((NT, KV, 8, TQ), F32),
                        pltpu.VMEM((2, TQ, H_ATT * TQ), F32),
                        pltpu.VMEM((2, TQ, H_ATT * TQ), F32),
                        pltpu.VMEM((KV_LATENT, H_ATT * TQ), F32),
                        pltpu.VMEM((KV_LATENT, H_ATT * TQ), F32),
                        pltpu.VMEM((1, H_ATT * TQ), F32),
                        pltpu.VMEM((1, H_ATT * TQ), F32),
                        pltpu.VMEM((KV_LATENT, H_ATT * TQ), F32)],
        compiler_params=_cp(("parallel", "arbitrary")),
        name="dsa",
    )(iq, iwt, qabs, ik, ckv, ckvt, bias_tiles, wuvt)


def _rwkv_kernel(p_ref, prev_ref, mu_ref, w0_ref, wdec_ref, a0_ref, waaa_ref, wgate_ref,
                 kk_ref, ka_ref, rk_ref, lng_ref, lnb_ref, y_ref, h_scr):
    c = pl.program_id(1)
    C = CHUNK
    N = HD_RWKV

    @pl.when(c == 0)
    def _():
        h_scr[...] = jnp.zeros(h_scr.shape, F32)

    p = p_ref[0]
    prev_row = jnp.where(c > 0, prev_ref[0, 7:8, :], 0.0)
    rid = lax.broadcasted_iota(jnp.int32, (C, 1), 0)
    p_prev = jnp.where(rid == 0, prev_row, pltpu.roll(p, 1, 0))
    p = p + mu_ref[...] * (p_prev - p)

    W = RWKV_WIDTH
    r = p[:, 0:W]
    k = p[:, W:2 * W]
    v = p[:, 2 * W:3 * W]
    wd = p[:, OFF_WD:OFF_WD + DECAY_LORA]
    ad = p[:, OFF_AD:OFF_AD + AAA_LORA]
    gd = p[:, OFF_GD:OFF_GD + GATE_PAD]

    w_log = -jax.nn.softplus(-(w0_ref[...] + _dot(jnp.tanh(wd).astype(BF16), wdec_ref[...]))) - 0.5
    logw = -jnp.exp(w_log)
    a = jax.nn.sigmoid(a0_ref[...] + _dot(ad.astype(BF16), waaa_ref[...]))
    g = _dot(jax.nn.sigmoid(gd).astype(BF16), wgate_ref[...])

    r64 = lax.broadcasted_iota(jnp.int32, (C, C), 0)
    c64 = lax.broadcasted_iota(jnp.int32, (C, C), 1)
    tril_incl = r64 >= c64
    tril_strict = r64 > c64
    eye = r64 == c64
    cum = jnp.dot(jnp.where(tril_incl, 1.0, 0.0), logw, preferred_element_type=F32,
                  precision=lax.Precision.HIGHEST)
    cum_last = cum[C - 1:C, :]
    g_in = jnp.exp(cum - logw)
    g_t = jnp.exp(cum)
    g_inv = jnp.exp(-cum)
    g_end = jnp.exp(cum_last - cum)
    g_all = jnp.exp(cum_last)

    kkf = k * kk_ref[...]
    kmod = k * (1.0 + (a - 1.0) * ka_ref[...])
    rkr = r * kmod * rk_ref[...]

    H = range(H_RWKV)
    sls = [slice(h * N, (h + 1) * N) for h in H]
    lane2 = lax.broadcasted_iota(jnp.int32, (C, 2 * C), 1)
    row2 = lax.broadcasted_iota(jnp.int32, (C, 2 * C), 0)
    in_left = lane2 < C
    col2 = jnp.where(in_left, lane2, lane2 - C)
    strict2 = row2 > col2
    incl2 = row2 >= col2
    zpad = jnp.zeros((C, C), F32)

    at, rt, bh, kh, vb, g4 = [], [], [], [], [], []
    for h in H:
        sl = sls[h]
        kk_h = kkf[:, sl]
        nrm = jnp.sqrt(jnp.sum(kk_h * kk_h, axis=-1, keepdims=True))
        kk_h = kk_h / jnp.maximum(nrm, 1e-12)
        k_h = kmod[:, sl]
        bvec = kk_h * a[:, sl]
        at_h = (-kk_h) * g_in[:, sl]
        rt_h = r[:, sl] * g_t[:, sl]
        lhs = jnp.concatenate([at_h, rt_h], axis=0).astype(BF16)
        rhs = jnp.concatenate([bvec * g_inv[:, sl], k_h * g_inv[:, sl]], axis=0).astype(BF16)
        g4.append(_dot_nt(lhs, rhs))
        at.append(at_h)
        rt.append(rt_h.astype(BF16))
        bh.append((bvec * g_end[:, sl]).astype(BF16))
        kh.append((k_h * g_end[:, sl]).astype(BF16))
        vb.append(v[:, sl].astype(BF16))

    a_ab = [jnp.where(tril_strict, g4[h][0:C, 0:C], 0.0) for h in H]
    tinv = [jnp.where(eye, 1.0, 0.0) + a_ab[h] for h in H]
    pw = a_ab
    for _ in range(5):
        pwb = [pw[h].astype(BF16) for h in H]
        pw = [_dot(pwb[h], pwb[h]) for h in H]
        tinv = [tinv[h] + _dot(tinv[h].astype(BF16), pw[h].astype(BF16)) for h in H]

    hst = [h_scr[h] for h in H]
    hb = [hst[h].astype(BF16) for h in H]
    x = []
    for h in H:
        xl = jnp.where(in_left, jnp.concatenate([at[h], zpad], axis=1),
                       jnp.where(strict2, g4[h][0:C, :], 0.0)).astype(BF16)
        x.append(_dot(xl, jnp.concatenate([hb[h], vb[h]], axis=0)))
    ub = [_dot(tinv[h].astype(BF16), x[h].astype(BF16)).astype(BF16) for h in H]
    uv = [jnp.concatenate([ub[h], vb[h]], axis=0) for h in H]
    y = [_dot(rt[h], hb[h]) + _dot(jnp.where(incl2, g4[h][C:2 * C, :], 0.0).astype(BF16), uv[h]) for h in H]
    for h in H:
        g_col = jnp.sum(jnp.where(eye, jnp.broadcast_to(g_all[:, sls[h]], (N, N)), 0.0), axis=1, keepdims=True)
        h_scr[h] = g_col * hst[h] + _dot_tn(jnp.concatenate([bh[h], kh[h]], axis=0), uv[h])

    for h in H:
        sl = sls[h]
        mean = jnp.mean(y[h], axis=-1, keepdims=True)
        yc = y[h] - mean
        var = jnp.mean(yc * yc, axis=-1, keepdims=True)
        yn = yc * lax.rsqrt(var + GN_EPS) * lng_ref[:, sl] + lnb_ref[:, sl]
        bonus = jnp.sum(rkr[:, sl], axis=-1, keepdims=True) * v[:, sl]
        y_ref[0, :, sl] = (yn + bonus) * g[:, sl]


def _rwkv_call(rw, mu_p, w0, wdec, a0, waaa, wgate_p, k_k, k_a, r_k, lng, lnb):
    B, T, _ = rw.shape
    NC = T // CHUNK
    W = RWKV_WIDTH
    c2 = lambda b, c: (0, 0)
    rowspec = pl.BlockSpec((1, W), c2)
    return pl.pallas_call(
        _rwkv_kernel,
        grid=(B, NC),
        in_specs=[pl.BlockSpec((1, CHUNK, RW_COLS), lambda b, c: (b, c, 0)),
                  pl.BlockSpec((1, 8, RW_COLS), lambda b, c: (b, jnp.maximum(c * (CHUNK // 8) - 1, 0), 0)),
                  pl.BlockSpec((1, RW_COLS), c2),
                  rowspec,
                  pl.BlockSpec((DECAY_LORA, W), c2),
                  rowspec,
                  pl.BlockSpec((AAA_LORA, W), c2),
                  pl.BlockSpec((GATE_PAD, W), c2),
                  rowspec, rowspec, rowspec, rowspec, rowspec],
        out_specs=pl.BlockSpec((1, CHUNK, W), lambda b, c: (b, c, 0)),
        out_shape=jax.ShapeDtypeStruct((B, T, W), F32),
        scratch_shapes=[pltpu.VMEM((H_RWKV, HD_RWKV, HD_RWKV), F32)],
        compiler_params=_cp(("parallel", "arbitrary")),
        name="rwkv",
    )(rw, rw, mu_p, w0, wdec, a0, waaa, wgate_p, k_k, k_a, r_k, lng, lnb)


def _ffn_kernel(x_ref, ya_ref, yr_ref, mod_ref, gpost_ref, gpre_ref, gfpost_ref,
                wo_ref, wg_ref, wu_ref, wd_ref, o_ref, x1_scr, hf_scr, acc_scr):
    j = pl.program_id(2)

    @pl.when(j == 0)
    def _():
        mix = (_dot(ya_ref[0].astype(BF16), wo_ref[0:ATT_WIDTH, :])
               + _dot(yr_ref[0].astype(BF16), wo_ref[ATT_WIDTH:, :]))
        x1 = x_ref[0] + mod_ref[0, 2:3, :] * _rms(mix, gpost_ref[...])
        x1_scr[...] = x1
        hf = _rms(x1, gpre_ref[...]) * (1.0 + mod_ref[0, 4:5, :]) + mod_ref[0, 3:4, :]
        hf_scr[...] = hf.astype(BF16)
        acc_scr[...] = jnp.zeros(acc_scr.shape, F32)

    hf = hf_scr[...]
    gate = _dot(hf, wg_ref[...])
    up = _dot(hf, wu_ref[...])
    act = (gate * jax.nn.sigmoid(gate) * up).astype(BF16)
    acc_scr[...] += _dot(act, wd_ref[...])

    @pl.when(j == pl.num_programs(2) - 1)
    def _():
        o_ref[0] = x1_scr[...] + mod_ref[0, 5:6, :] * _rms(acc_scr[...], gfpost_ref[...])


def _ffn_call(x, y_att, y_rwkv, mod3, g_post, g_pre, g_fpost, wo, wg, wu, wd):
    B, T, D = x.shape
    TM = 512
    TF = D_FF // 2
    c3 = lambda b, i, j: (0, 0)
    return pl.pallas_call(
        _ffn_kernel,
        grid=(B, T // TM, D_FF // TF),
        in_specs=[pl.BlockSpec((1, TM, D), lambda b, i, j: (b, i, 0)),
                  pl.BlockSpec((1, TM, ATT_WIDTH), lambda b, i, j: (b, i, 0)),
                  pl.BlockSpec((1, TM, RWKV_WIDTH), lambda b, i, j: (b, i, 0)),
                  pl.BlockSpec((1, 6, D), lambda b, i, j: (b, 0, 0)),
                  pl.BlockSpec((1, D), c3), pl.BlockSpec((1, D), c3), pl.BlockSpec((1, D), c3),
                  pl.BlockSpec((D, D), c3),
                  pl.BlockSpec((D, TF), lambda b, i, j: (0, j)),
                  pl.BlockSpec((D, TF), lambda b, i, j: (0, j)),
                  pl.BlockSpec((TF, D), lambda b, i, j: (j, 0))],
        out_specs=pl.BlockSpec((1, TM, D), lambda b, i, j: (b, i, 0)),
        out_shape=jax.ShapeDtypeStruct((B, T, D), F32),
        scratch_shapes=[pltpu.VMEM((TM, D), F32), pltpu.VMEM((TM, D), BF16), pltpu.VMEM((TM, D), F32)],
        compiler_params=_cp(("parallel", "parallel", "arbitrary")),
        name="ffn",
    )(x, y_att, y_rwkv, mod3, g_post, g_pre, g_fpost, wo, wg, wu, wd)


def _pad_cols(w, n):
    return jnp.pad(w, ((0, 0), (0, n - w.shape[1])))


def _layer(x, mod3, rel_bias, mix_pre_norm, mix_post_norm, ffn_pre_norm, ffn_post_norm,
           w_in, q_norm, w_uq, w_idx_q, kv_norm, idx_k_norm, w_uk, w_uv,
           mu_shift, w0, w_decay_up, a0, w_aaa_up, w_gate_up, k_k, k_a, r_k, ln_x_gain, ln_x_bias,
           w_out, w_ffn_gate, w_ffn_up, w_ffn_down):
    B, T, D = x.shape
    att_cols = Q_LORA + KV_LATENT + IDX_DIM + IDX_HEADS
    n_main = 3 * RWKV_WIDTH + DECAY_LORA + AAA_LORA
    w_att = _pad_cols(w_in[:, :att_cols], ATT_PAD)
    w_rw = w_in[:, att_cols:]
    w_in_p = jnp.concatenate([w_att, w_rw[:, :n_main], _pad_cols(w_rw[:, n_main:], GATE_PAD)], axis=1).astype(BF16)
    mu_p = jnp.concatenate([mu_shift[:n_main], jnp.pad(mu_shift[n_main:], (0, GATE_PAD - GATE_LORA))]).reshape(1, RW_COLS)
    wgate_p = jnp.pad(w_gate_up, ((0, GATE_PAD - GATE_LORA), (0, 0))).astype(BF16)

    att, rw = _inproj_call(x, mod3, mix_pre_norm.reshape(1, D), w_in_p)

    wuq = w_uq.reshape(Q_LORA, ATT_WIDTH).astype(BF16)
    wiq = w_idx_q.reshape(Q_LORA, IDX_HEADS * IDX_DIM).astype(BF16)
    wukt = jnp.transpose(w_uk, (0, 2, 1)).astype(BF16)
    qabs, iq, ik, ckv, ckvt, iwt = _dsaprep_call(att, q_norm.reshape(1, -1), kv_norm.reshape(1, -1),
                                                 idx_k_norm.reshape(1, -1), wuq, wiq, wukt)
    bias_tiles = _bias_call(rel_bias)
    wuvt = jnp.transpose(w_uv, (0, 2, 1)).astype(BF16)
    topk = min(TOPK_MAX, T // 4)
    y_att = _dsa_call(iq, iwt, qabs, ik, ckv, ckvt, bias_tiles, wuvt, topk)

    row = lambda z: z.reshape(1, RWKV_WIDTH)
    y_rwkv = _rwkv_call(rw, mu_p, row(w0), w_decay_up.astype(BF16), row(a0), w_aaa_up.astype(BF16),
                        wgate_p, row(k_k), row(k_a), row(r_k), row(ln_x_gain), row(ln_x_bias))

    return _ffn_call(x, y_att, y_rwkv, mod3, mix_post_norm.reshape(1, D), ffn_pre_norm.reshape(1, D),
                     ffn_post_norm.reshape(1, D), w_out.astype(BF16), w_ffn_gate.astype(BF16),
                     w_ffn_up.astype(BF16), w_ffn_down.astype(BF16))


def kernel(x, c, rel_bias, ada_w, ada_b, mix_pre_norm, mix_post_norm, ffn_pre_norm, ffn_post_norm, w_in, q_norm, w_uq, w_idx_q, kv_norm, idx_k_norm, w_uk, w_uv, mu_shift, w0, w_decay_up, a0, w_aaa_up, w_gate_up, k_k, k_a, r_k, ln_x_gain, ln_x_bias, w_out, w_ffn_gate, w_ffn_up, w_ffn_down):
    B, T, D = x.shape
    assert D == D_MODEL and T % 512 == 0 and T // 4 >= 1
    layer_params = (mix_pre_norm, mix_post_norm, ffn_pre_norm, ffn_post_norm,
                    w_in, q_norm, w_uq, w_idx_q, kv_norm, idx_k_norm, w_uk, w_uv,
                    mu_shift, w0, w_decay_up, a0, w_aaa_up, w_gate_up, k_k, k_a, r_k, ln_x_gain, ln_x_bias,
                    w_out, w_ffn_gate, w_ffn_up, w_ffn_down)
    for l in range(ada_w.shape[0]):
        mod3 = _mod_call(c, ada_w[l], ada_b[l]).reshape(B, 6, D)
        x = _layer(x, mod3, rel_bias, *[p[l] for p in layer_params])
    return x
```

```python
import functools
import math

import numpy as np
import jax
import jax.numpy as jnp
from jax import lax
from jax.experimental import pallas as pl
from jax.experimental.pallas import tpu as pltpu

F32 = jnp.float32
BF16 = jnp.bfloat16

D_MODEL = 1024
ATT_WIDTH = 512
RWKV_WIDTH = 512
HD_ATT = 64
H_ATT = 8
HD_RWKV = 64
H_RWKV = 8
Q_LORA = 256
KV_LATENT = 128
IDX_HEADS = 8
IDX_DIM = 64
TOPK_MAX = 256
NUM_BUCKETS = 32
MAX_DISTANCE = 128
DECAY_LORA = 64
AAA_LORA = 64
GATE_LORA = 160
GN_EPS = 64e-5
D_FF = 2816
NORM_EPS = 1e-6

ATT_PAD = 512
GATE_PAD = 256
RW_COLS = 3 * RWKV_WIDTH + DECAY_LORA + AAA_LORA + GATE_PAD
OFF_WD = 3 * RWKV_WIDTH
OFF_AD = OFF_WD + DECAY_LORA
OFF_GD = OFF_AD + AAA_LORA

TQ = 128
CHUNK = 64
NEG_BIG = -1e30
LOWEST = -3.0e38
LOG2E = math.log2(math.e)
VMEM_LIMIT = 56 * 1024 * 1024


def _cp(sem):
    return pltpu.CompilerParams(dimension_semantics=sem, vmem_limit_bytes=VMEM_LIMIT)


def _dot(a, b):
    return jnp.dot(a, b, preferred_element_type=F32)


def _dot_nt(a, b):
    return lax.dot_general(a, b, (((1,), (1,)), ((), ())), preferred_element_type=F32)


def _dot_tn(a, b):
    return lax.dot_general(a, b, (((0,), (0,)), ((), ())), preferred_element_type=F32)


def _rms(z, gain):
    return z * lax.rsqrt(jnp.mean(z * z, axis=-1, keepdims=True) + NORM_EPS) * gain


def _mod_kernel(c_ref, w_ref, b_ref, o_ref):
    c = c_ref[...]
    s = c * jax.nn.sigmoid(c)
    o_ref[...] = _dot(s.astype(BF16), w_ref[...].astype(BF16)) + b_ref[...]


def _mod_call(c, ada_w, ada_b):
    B, D = c.shape
    N = ada_w.shape[1]
    TN = 1536
    return pl.pallas_call(
        _mod_kernel,
        grid=(N // TN,),
        in_specs=[pl.BlockSpec((B, D), lambda j: (0, 0)),
                  pl.BlockSpec((D, TN), lambda j: (0, j)),
                  pl.BlockSpec((1, TN), lambda j: (0, j))],
        out_specs=pl.BlockSpec((B, TN), lambda j: (0, j)),
        out_shape=jax.ShapeDtypeStruct((B, N), F32),
        compiler_params=_cp(("parallel",)),
        name="mod",
    )(c, ada_w, ada_b.reshape(1, N))


def _inproj_kernel(x_ref, mod_ref, g_ref, w_ref, att_ref, rw_ref):
    x = x_ref[0]
    y = _rms(x, g_ref[...])
    sh = mod_ref[0, 0:1, :]
    sc = mod_ref[0, 1:2, :]
    h = (y * (1.0 + sc) + sh).astype(BF16)
    att_ref[0] = _dot(h, w_ref[:, 0:ATT_PAD])
    for n0 in range(0, RW_COLS, 384):
        rw_ref[0, :, n0:n0 + 384] = _dot(h, w_ref[:, ATT_PAD + n0:ATT_PAD + n0 + 384])


def _inproj_call(x, mod3, gain, w_in_p):
    B, T, D = x.shape
    TM = 512
    NP = w_in_p.shape[1]
    return pl.pallas_call(
        _inproj_kernel,
        grid=(B, T // TM),
        in_specs=[pl.BlockSpec((1, TM, D), lambda b, i: (b, i, 0)),
                  pl.BlockSpec((1, 6, D), lambda b, i: (b, 0, 0)),
                  pl.BlockSpec((1, D), lambda b, i: (0, 0)),
                  pl.BlockSpec((D, NP), lambda b, i: (0, 0))],
        out_specs=[pl.BlockSpec((1, TM, ATT_PAD), lambda b, i: (b, i, 0)),
                   pl.BlockSpec((1, TM, RW_COLS), lambda b, i: (b, i, 0))],
        out_shape=[jax.ShapeDtypeStruct((B, T, ATT_PAD), F32),
                   jax.ShapeDtypeStruct((B, T, RW_COLS), F32)],
        compiler_params=_cp(("parallel", "parallel")),
        name="inproj",
    )(x, mod3, gain, w_in_p)


def _dsaprep_kernel(att_ref, qn_ref, kvn_ref, ikn_ref, wuq_ref, wiq_ref, wukt_ref,
                    qabs_ref, iq_ref, ik_ref, ckv_ref, ckvt_ref, iwt_ref):
    att = att_ref[0]
    tm = att.shape[0]
    o1 = Q_LORA
    o2 = o1 + KV_LATENT
    o3 = o2 + IDX_DIM
    cq = _rms(att[:, 0:o1], qn_ref[...]).astype(BF16)
    ckv = _rms(att[:, o1:o2], kvn_ref[...])
    ckv_ref[0] = ckv.astype(BF16)
    ik_ref[0] = _rms(att[:, o2:o3], ikn_ref[...]).astype(BF16)
    tail = att[:, o2:ATT_PAD]
    w_off = o3 - o2
    for sb in range(tm // TQ):
        rows = slice(sb * TQ, (sb + 1) * TQ)
        ckvt_ref[0, sb] = ckv[rows, :].T.astype(BF16)
        iwt_ref[0, :, rows] = tail[rows, :].T[w_off:w_off + IDX_HEADS, :] * (IDX_HEADS ** -0.5 * IDX_DIM ** -0.5)
    q = _dot(cq, wuq_ref[...])
    iqv = _dot(cq, wiq_ref[...])
    for h in range(H_ATT):
        qh = q[:, h * HD_ATT:(h + 1) * HD_ATT].astype(BF16)
        qabs_ref[0, h] = (_dot(qh, wukt_ref[h]) * (HD_ATT ** -0.5 * LOG2E)).astype(BF16)
        iq_ref[0, h] = iqv[:, h * IDX_DIM:(h + 1) * IDX_DIM].astype(BF16)


def _dsaprep_call(att, q_norm, kv_norm, idx_k_norm, wuq, wiq, wukt):
    B, T, _ = att.shape
    TM = 512
    const2 = lambda b, i: (0, 0)
    return pl.pallas_call(
        _dsaprep_kernel,
        grid=(B, T // TM),
        in_specs=[pl.BlockSpec((1, TM, ATT_PAD), lambda b, i: (b, i, 0)),
                  pl.BlockSpec((1, Q_LORA), const2),
                  pl.BlockSpec((1, KV_LATENT), const2),
                  pl.BlockSpec((1, IDX_DIM), const2),
                  pl.BlockSpec((Q_LORA, ATT_WIDTH), const2),
                  pl.BlockSpec((Q_LORA, IDX_HEADS * IDX_DIM), const2),
                  pl.BlockSpec((H_ATT, HD_ATT, KV_LATENT), lambda b, i: (0, 0, 0))],
        out_specs=[pl.BlockSpec((1, H_ATT, TM, KV_LATENT), lambda b, i: (b, 0, i, 0)),
                   pl.BlockSpec((1, IDX_HEADS, TM, IDX_DIM), lambda b, i: (b, 0, i, 0)),
                   pl.BlockSpec((1, TM, IDX_DIM), lambda b, i: (b, i, 0)),
                   pl.BlockSpec((1, TM, KV_LATENT), lambda b, i: (b, i, 0)),
                   pl.BlockSpec((1, TM // TQ, KV_LATENT, TQ), lambda b, i: (b, i, 0, 0)),
                   pl.BlockSpec((1, IDX_HEADS, TM), lambda b, i: (b, 0, i))],
        out_shape=[jax.ShapeDtypeStruct((B, H_ATT, T, KV_LATENT), BF16),
                   jax.ShapeDtypeStruct((B, IDX_HEADS, T, IDX_DIM), BF16),
                   jax.ShapeDtypeStruct((B, T, IDX_DIM), BF16),
                   jax.ShapeDtypeStruct((B, T, KV_LATENT), BF16),
                   jax.ShapeDtypeStruct((B, T // TQ, KV_LATENT, TQ), BF16),
                   jax.ShapeDtypeStruct((B, IDX_HEADS, T), F32)],
        compiler_params=_cp(("parallel", "parallel")),
        name="dsaprep",
    )(att, q_norm, kv_norm, idx_k_norm, wuq, wiq, wukt)


def _t5_bucket_np(rel):
    max_exact = NUM_BUCKETS // 2
    nf = np.maximum(rel, 1).astype(np.float32)
    large = max_exact + (np.log(nf / max_exact) / math.log(MAX_DISTANCE / max_exact)
                         * (NUM_BUCKETS - max_exact)).astype(np.int32)
    large = np.minimum(large, NUM_BUCKETS - 1)
    return np.where(rel < max_exact, rel, large).astype(np.int32)


def _near_bucket_tiles():
    s = np.arange(TQ)[:, None]
    t = np.arange(TQ)[None, :]
    tiles = [_t5_bucket_np(np.maximum(t - s + TQ * d, 0)) for d in range(2)]
    assert _t5_bucket_np(np.array([TQ + 1]))[0] == NUM_BUCKETS - 1
    return np.stack(tiles)


def _bias_kernel(rb_ref, bk_ref, o_ref):
    for d in range(2):
        bk = bk_ref[d]
        for h in range(H_ATT):
            def body(b, acc):
                return jnp.where(bk == b, rb_ref[b, h], acc)
            acc = lax.fori_loop(0, NUM_BUCKETS, body, jnp.zeros((TQ, TQ), F32))
            o_ref[d, h] = (acc - rb_ref[NUM_BUCKETS - 1, h]) * LOG2E
    o_ref[2] = jnp.zeros((H_ATT, TQ, TQ), F32)


def _bias_call(rel_bias):
    buckets = jnp.asarray(_near_bucket_tiles())
    return pl.pallas_call(
        _bias_kernel,
        in_specs=[pl.BlockSpec(memory_space=pltpu.SMEM),
                  pl.BlockSpec(memory_space=pltpu.VMEM)],
        out_specs=pl.BlockSpec(memory_space=pltpu.VMEM),
        out_shape=jax.ShapeDtypeStruct((3, H_ATT, TQ, TQ), F32),
        name="bias",
    )(rel_bias, buckets)


KV = TQ // 8
FREE_PROBES = 8


def _key_reduce(x, op):
    return op(op(x, axis=0), axis=0, keepdims=True)


def _dsa_kernel(iq_ref, iwt_ref, qabs_ref, ik_ref, ckv_ref, ckvt_ref, bias_ref, wuvt_ref, y_ref,
                score_scr, stage_a, stage_b, pv_a, pv_b, alpha_a, alpha_b, acc_scr, *, topk):
    i = pl.program_id(1)
    kf = float(topk)
    kidx = (lax.broadcasted_iota(jnp.int32, (KV, 8, TQ), 0) * 8
            + lax.broadcasted_iota(jnp.int32, (KV, 8, TQ), 1))
    qidx = lax.broadcasted_iota(jnp.int32, (KV, 8, TQ), 2)
    n_pairs = i // 2 + 1

    iwt = iwt_ref[0]
    w8 = [jnp.broadcast_to(iwt[h:h + 1, :], (8, TQ))[None] for h in range(IDX_HEADS)]
    iq2 = iq_ref[0].reshape(IDX_HEADS * TQ, IDX_DIM)

    n_trips = (n_pairs + 1) // 2
    stage = (stage_a, stage_b)

    def stage_pair(q, lhs_ref, rhs, dst):
        nxt = jnp.minimum(q, n_pairs - 1)
        rows = lhs_ref[0, pl.ds(pl.multiple_of(2 * nxt * TQ, 2 * TQ), 2 * TQ), :]
        dst[...] = _dot_nt(rows, rhs).reshape(2, TQ, H_ATT * TQ)

    def two_halves(half):
        def trip(r, carry):
            carry = half(2 * r, 0, carry)
            return lax.cond(2 * r + 1 < n_pairs, lambda c: half(2 * r + 1, 1, c), lambda c: c, carry)
        return trip

    def score_half(q, par, carry):
        rmax, rmin = carry
        stage_pair(q + 1, ik_ref, iq2, stage[1 - par])
        for blk in range(2):
            j = 2 * q + blk
            sc = jnp.zeros((KV, 8, TQ), F32)
            for h in range(IDX_HEADS):
                s = stage[par][blk, :, h * TQ:(h + 1) * TQ].reshape(KV, 8, TQ)
                sc = sc + jnp.maximum(s, 0.0) * w8[h]
            causal = kidx + (j - i) * TQ <= qidx
            score_scr[j] = jnp.where(causal, sc, -jnp.inf)
            rmax = jnp.maximum(rmax, jnp.max(jnp.where(causal, sc, -jnp.inf), axis=0))
            rmin = jnp.minimum(rmin, jnp.min(jnp.where(causal, sc, jnp.inf), axis=0))
        return rmax, rmin

    stage_pair(0, ik_ref, iq2, stage[0])
    rmax, rmin = lax.fori_loop(0, n_trips, two_halves(score_half),
                               (jnp.full((8, TQ), -jnp.inf, F32), jnp.full((8, TQ), jnp.inf, F32)))
    smax = jnp.max(rmax, axis=0, keepdims=True)
    smin = jnp.min(rmin, axis=0, keepdims=True)

    nvalid = (i * TQ + lax.broadcasted_iota(jnp.int32, (1, TQ), 1) + 1).astype(F32)
    needs_thr = nvalid > kf

    def probe(v):
        vb = jnp.broadcast_to(v, (8, TQ))[None, None]

        def body(q, st):
            cnt, dn, up = st
            for j in (2 * q, 2 * q + 1):
                s = score_scr[j].reshape(4, KV // 4, 8, TQ)
                ge = s >= vb
                cnt = cnt + jnp.sum(jnp.where(ge, 1.0, 0.0), axis=1)
                dn = jnp.minimum(dn, jnp.min(jnp.where(ge, s, jnp.inf), axis=1))
                up = jnp.maximum(up, jnp.max(jnp.where(ge, -jnp.inf, s), axis=1))
            return cnt, dn, up
        cnt, dn, up = lax.fori_loop(0, n_pairs, body, (jnp.zeros((4, 8, TQ), F32),
                                                       jnp.full((4, 8, TQ), jnp.inf, F32),
                                                       jnp.full((4, 8, TQ), -jnp.inf, F32)))
        return (jnp.sum(jnp.sum(cnt, axis=0), axis=0, keepdims=True),
                jnp.min(jnp.min(dn, axis=0), axis=0, keepdims=True),
                jnp.max(jnp.max(up, axis=0), axis=0, keepdims=True))

    def logit(c):
        return jnp.log((c + 0.5) / (nvalid - c + 0.5))

    g_k = logit(jnp.full((1, TQ), kf - 0.5, F32))

    def search_step(st):
        lo, hi, clo, ghi, slo, shi = st
        is_open = lo < hi
        g_lo = logit(clo)
        g_hi = logit(ghi)
        phi = jnp.clip((g_lo - g_k) / (g_lo - g_hi), 0.02, 0.98)
        stalled = (slo >= 2.0) | (shi >= 2.0)
        mid = jnp.where(stalled, 0.5 * lo + 0.5 * hi, lo + (hi - lo) * phi)
        mid = jnp.where((mid > lo) & (mid <= hi), mid, hi)
        c, dn, up = probe(mid)
        ge = c >= kf
        move_lo = is_open & ge
        move_hi = is_open & jnp.logical_not(ge)
        lo = jnp.where(move_lo, dn, lo)
        clo = jnp.where(move_lo, c, clo)
        hi = jnp.where(move_hi, up, hi)
        ghi = jnp.where(move_hi, c, ghi)
        slo = jnp.where(stalled | ge, 0.0, slo + 1.0)
        shi = jnp.where(stalled | jnp.logical_not(ge), 0.0, shi + 1.0)
        return lo, hi, clo, ghi, slo, shi

    def n_open(st):
        return jnp.sum(jnp.where(st[0] < st[1], 1.0, 0.0))

    def checked_step(carry):
        st = search_step(carry[0])
        return st, n_open(st), carry[2] + 1

    search_cap = ik_ref.shape[1] + 2
    zero = jnp.zeros((1, TQ), F32)
    st = (smin, jnp.where(needs_thr, smax, smin), nvalid, zero, zero, zero)
    st = lax.fori_loop(0, FREE_PROBES, lambda _, s: search_step(s), st)
    st, _, _ = lax.while_loop(lambda c: jnp.logical_and(c[1] > 0.0, c[2] < search_cap), checked_step,
                              (st, n_open(st), jnp.int32(0)))
    thr, _, clo, chi, _, _ = st

    tie = needs_thr & (clo > kf)
    n_tie = jnp.sum(jnp.where(tie, 1.0, 0.0))

    @pl.when(n_tie > 0.0)
    def _():
        r2 = lax.broadcasted_iota(jnp.int32, (TQ, TQ), 0)
        c2 = lax.broadcasted_iota(jnp.int32, (TQ, TQ), 1)
        lower = jnp.where(c2 <= r2, 1.0, 0.0).astype(BF16)
        thrb = jnp.broadcast_to(thr, (TQ, TQ))
        tieb = jnp.broadcast_to(tie, (TQ, TQ))
        needb = jnp.broadcast_to(kf - chi, (TQ, TQ))

        def body(j, carry):
            s = score_scr[j].reshape(TQ, TQ)
            eq = (s == thrb) & tieb
            pref = _dot(lower, jnp.where(eq, 1.0, 0.0).astype(BF16)) + carry
            score_scr[j] = jnp.where(eq & (pref > needb), -jnp.inf, s).reshape(KV, 8, TQ)
            return jnp.broadcast_to(pref[TQ - 1:TQ, :], (TQ, TQ))
        lax.fori_loop(0, i + 1, body, jnp.zeros((TQ, TQ), F32))

    thr_eff = jnp.broadcast_to(jnp.where(needs_thr, thr, LOWEST), (8, TQ))[None]

    acc_scr[...] = jnp.zeros(acc_scr.shape, F32)
    qa2 = qabs_ref[0].reshape(H_ATT * TQ, KV_LATENT)

    pv = (pv_a, pv_b)
    alpha_st = (alpha_a, alpha_b)

    def apply_pending(par):
        acc_scr[...] = alpha_st[par][...] * acc_scr[...] + pv[par][...]

    def attend_variant(q, par, carry, near):
        m, l = list(carry[0]), list(carry[1])
        stage_pair(q + 1, ckv_ref, qa2, stage[1 - par])
        js = (2 * q, 2 * q + 1)
        sel = [score_scr[j] >= thr_eff for j in js]
        d = [jnp.clip(i - j, 0, 2) for j in js]
        kvt = jnp.concatenate([ckvt_ref[0, js[0]], ckvt_ref[0, js[1]]], axis=1)
        ps, alphas = [], []
        for h in range(H_ATT):
            lg = [stage[par][b, :, h * TQ:(h + 1) * TQ] for b in range(2)]
            if near:
                lg = [lg[b] + bias_ref[d[b], h] for b in range(2)]
            lg = [jnp.where(sel[b], lg[b].reshape(KV, 8, TQ), NEG_BIG) for b in range(2)]
            m_new = jnp.maximum(m[h], jnp.maximum(_key_reduce(lg[0], jnp.max), _key_reduce(lg[1], jnp.max)))
            alpha = jnp.exp2(m[h] - m_new)
            mb = jnp.broadcast_to(m_new, (8, TQ))[None]
            p = [jnp.exp2(lg[b] - mb) for b in range(2)]
            l[h] = alpha * l[h] + _key_reduce(p[0], jnp.sum) + _key_reduce(p[1], jnp.sum)
            m[h] = m_new
            alphas.append(alpha)
            ps.append(jnp.concatenate([p[0].reshape(TQ, TQ), p[1].reshape(TQ, TQ)], axis=0).astype(BF16))
            if h % 2 == 1:
                cols = slice((h - 1) * TQ, (h + 1) * TQ)
                pv[par][:, cols] = _dot(kvt, jnp.concatenate(ps[h - 1:h + 1], axis=1))
        alpha_st[par][...] = jnp.concatenate(alphas, axis=1)
        apply_pending(1 - par)
        return tuple(m), tuple(l)

    def attend_half(q, par, carry):
        return lax.cond(2 * q + 1 >= i - 1,
                        lambda c: attend_variant(q, par, c, True),
                        lambda c: attend_variant(q, par, c, False), carry)

    pv_b[...] = jnp.zeros(pv_b.shape, F32)
    alpha_b[...] = jnp.ones(alpha_b.shape, F32)
    stage_pair(0, ckv_ref, qa2, stage[0])
    m0 = tuple(jnp.full((1, TQ), NEG_BIG, F32) for _ in range(H_ATT))
    l0 = tuple(jnp.zeros((1, TQ), F32) for _ in range(H_ATT))
    m, l = lax.fori_loop(0, n_trips, two_halves(attend_half), (m0, l0))
    for par in range(2):
        @pl.when((n_pairs - 1) % 2 == par)
        def _():
            apply_pending(par)

    yt = []
    for h in range(H_ATT):
        o = acc_scr[:, h * TQ:(h + 1) * TQ] / l[h]
        yt.append(_dot(wuvt_ref[h], o.astype(BF16)))
    for g in range(ATT_WIDTH // TQ):
        per = TQ // HD_ATT
        y_ref[0, :, g * TQ:(g + 1) * TQ] = jnp.concatenate(yt[g * per:(g + 1) * per], axis=0).T


def _dsa_call(iq, iwt, qabs, ik, ckv, ckvt, bias_tiles, wuvt, topk):
    B, _, T, _ = iq.shape
    NT = T // TQ
    return pl.pallas_call(
        functools.partial(_dsa_kernel, topk=topk),
        grid=(B, NT),
        in_specs=[pl.BlockSpec((1, IDX_HEADS, TQ, IDX_DIM), lambda b, i: (b, 0, i, 0)),
                  pl.BlockSpec((1, IDX_HEADS, TQ), lambda b, i: (b, 0, i)),
                  pl.BlockSpec((1, H_ATT, TQ, KV_LATENT), lambda b, i: (b, 0, i, 0)),
                  pl.BlockSpec((1, T, IDX_DIM), lambda b, i: (b, 0, 0)),
                  pl.BlockSpec((1, T, KV_LATENT), lambda b, i: (b, 0, 0)),
                  pl.BlockSpec((1, NT, KV_LATENT, TQ), lambda b, i: (b, 0, 0, 0)),
                  pl.BlockSpec((3, H_ATT, TQ, TQ), lambda b, i: (0, 0, 0, 0)),
                  pl.BlockSpec((H_ATT, HD_ATT, KV_LATENT), lambda b, i: (0, 0, 0))],
        out_specs=pl.BlockSpec((1, TQ, ATT_WIDTH), lambda b, i: (b, i, 0)),
        out_shape=jax.ShapeDtypeStruct((B, T, ATT_WIDTH), F32),
        scratch_shapes=[pltpu.VMEM((NT, KV, 8, TQ), F32),
                        pltpu.VMEM((2, TQ, H_ATT * TQ), F32),
                        pltpu.VMEM((2, TQ, H_ATT * TQ), F32),
                        pltpu.VMEM((KV_LATENT, H_ATT * TQ), F32),
                        pltpu.VMEM((KV_LATENT, H_ATT * TQ), F32),
                        pltpu.VMEM((1, H_ATT * TQ), F32),
                        pltpu.VMEM((1, H_ATT * TQ), F32),
                        pltpu.VMEM((KV_LATENT, H_ATT * TQ), F32)],
        compiler_params=_cp(("parallel", "arbitrary")),
        name="dsa",
    )(iq, iwt, qabs, ik, ckv, ckvt, bias_tiles, wuvt)


def _rwkv_kernel(p_ref, prev_ref, mu_ref, w0_ref, wdec_ref, a0_ref, waaa_ref, wgate_ref,
                 kk_ref, ka_ref, rk_ref, lng_ref, lnb_ref, y_ref, h_scr):
    c = pl.program_id(1)
    C = CHUNK
    N = HD_RWKV

    @pl.when(c == 0)
    def _():
        h_scr[...] = jnp.zeros(h_scr.shape, F32)

    NS = p_ref.shape[0]
    p = p_ref[...].reshape(NS * C, RW_COLS)
    rid = lax.broadcasted_iota(jnp.int32, (NS * C, 1), 0)
    p_prev = pltpu.roll(p, 1, 0)
    for s in range(NS):
        p_prev = jnp.where(rid == s * C, jnp.where(c > 0, prev_ref[s, 7:8, :], 0.0), p_prev)
    p = p + mu_ref[...] * (p_prev - p)

    W = RWKV_WIDTH
    r = p[:, 0:W]
    k = p[:, W:2 * W]
    v = p[:, 2 * W:3 * W]
    wd = p[:, OFF_WD:OFF_WD + DECAY_LORA]
    ad = p[:, OFF_AD:OFF_AD + AAA_LORA]
    gd = p[:, OFF_GD:OFF_GD + GATE_PAD]

    w_log = -jax.nn.softplus(-(w0_ref[...] + _dot(jnp.tanh(wd).astype(BF16), wdec_ref[...]))) - 0.5
    logw = -jnp.exp(w_log)
    a = jax.nn.sigmoid(a0_ref[...] + _dot(ad.astype(BF16), waaa_ref[...]))
    g = _dot(jax.nn.sigmoid(gd).astype(BF16), wgate_ref[...])

    r64 = lax.broadcasted_iota(jnp.int32, (C, C), 0)
    c64 = lax.broadcasted_iota(jnp.int32, (C, C), 1)
    tril_incl = r64 >= c64
    tril_strict = r64 > c64
    eye = r64 == c64
    ra = lax.broadcasted_iota(jnp.int32, (NS * C, NS * C), 0)
    ca = lax.broadcasted_iota(jnp.int32, (NS * C, NS * C), 1)
    same_seq = (ra >= ca) & (ra - ca <= ra % C)
    cum = jnp.dot(jnp.where(same_seq, 1.0, 0.0), logw, preferred_element_type=F32,
                  precision=lax.Precision.HIGHEST)
    cum_last = cum[C - 1:C, :]
    for s in range(1, NS):
        cum_last = jnp.where(rid >= s * C, cum[(s + 1) * C - 1:(s + 1) * C, :], cum_last)
    g_in = jnp.exp(cum - logw)
    g_t = jnp.exp(cum)
    g_inv = jnp.exp(-cum)
    g_end = jnp.exp(cum_last - cum)
    g_all = jnp.exp(cum_last)

    kkf = k * kk_ref[...]
    kmod = k * (1.0 + (a - 1.0) * ka_ref[...])
    rkr = r * kmod * rk_ref[...]

    H = range(NS * H_RWKV)
    rows = [slice((ch // H_RWKV) * C, (ch // H_RWKV + 1) * C) for ch in H]
    sls = [slice((ch % H_RWKV) * N, (ch % H_RWKV + 1) * N) for ch in H]
    lane2 = lax.broadcasted_iota(jnp.int32, (C, 2 * C), 1)
    row2 = lax.broadcasted_iota(jnp.int32, (C, 2 * C), 0)
    in_left = lane2 < C
    col2 = jnp.where(in_left, lane2, lane2 - C)
    strict2 = row2 > col2
    incl2 = row2 >= col2
    zpad = jnp.zeros((C, C), F32)

    at, rt, bh, kh, vb, g4 = [], [], [], [], [], []
    for h in H:
        rs, sl = rows[h], sls[h]
        kk_h = kkf[rs, sl]
        nrm = jnp.sqrt(jnp.sum(kk_h * kk_h, axis=-1, keepdims=True))
        kk_h = kk_h / jnp.maximum(nrm, 1e-12)
        k_h = kmod[rs, sl]
        bvec = kk_h * a[rs, sl]
        at_h = (-kk_h) * g_in[rs, sl]
        rt_h = r[rs, sl] * g_t[rs, sl]
        lhs = jnp.concatenate([at_h, rt_h], axis=0).astype(BF16)
        rhs = jnp.concatenate([bvec * g_inv[rs, sl], k_h * g_inv[rs, sl]], axis=0).astype(BF16)
        g4.append(_dot_nt(lhs, rhs))
        at.append(at_h)
        rt.append(rt_h.astype(BF16))
        bh.append((bvec * g_end[rs, sl]).astype(BF16))
        kh.append((k_h * g_end[rs, sl]).astype(BF16))
        vb.append(v[rs, sl].astype(BF16))

    a_ab = [jnp.where(tril_strict, g4[h][0:C, 0:C], 0.0) for h in H]
    tinv = [jnp.where(eye, 1.0, 0.0) + a_ab[h] for h in H]
    pw = a_ab
    for _ in range(5):
        pwb = [pw[h].astype(BF16) for h in H]
        pw = [_dot(pwb[h], pwb[h]) for h in H]
        tinv = [tinv[h] + _dot(tinv[h].astype(BF16), pw[h].astype(BF16)) for h in H]

    hst = [h_scr[h] for h in H]
    hb = [hst[h].astype(BF16) for h in H]
    x = []
    for h in H:
        xl = jnp.where(in_left, jnp.concatenate([at[h], zpad], axis=1),
                       jnp.where(strict2, g4[h][0:C, :], 0.0)).astype(BF16)
        x.append(_dot(xl, jnp.concatenate([hb[h], vb[h]], axis=0)))
    ub = [_dot(tinv[h].astype(BF16), x[h].astype(BF16)).astype(BF16) for h in H]
    uv = [jnp.concatenate([ub[h], vb[h]], axis=0) for h in H]
    y = [_dot(rt[h], hb[h]) + _dot(jnp.where(incl2, g4[h][C:2 * C, :], 0.0).astype(BF16), uv[h]) for h in H]
    for h in H:
        g_col = jnp.sum(jnp.where(eye, g_all[rows[h], sls[h]], 0.0), axis=1, keepdims=True)
        h_scr[h] = g_col * hst[h] + _dot_tn(jnp.concatenate([bh[h], kh[h]], axis=0), uv[h])

    for h in H:
        rs, sl = rows[h], sls[h]
        mean = jnp.mean(y[h], axis=-1, keepdims=True)
        yc = y[h] - mean
        var = jnp.mean(yc * yc, axis=-1, keepdims=True)
        yn = yc * lax.rsqrt(var + GN_EPS) * lng_ref[:, sl] + lnb_ref[:, sl]
        bonus = jnp.sum(rkr[rs, sl], axis=-1, keepdims=True) * v[rs, sl]
        y_ref[h // H_RWKV, :, sl] = (yn + bonus) * g[rs, sl]


def _rwkv_call(rw, mu_p, w0, wdec, a0, waaa, wgate_p, k_k, k_a, r_k, lng, lnb):
    B, T, _ = rw.shape
    NC = T // CHUNK
    W = RWKV_WIDTH
    c2 = lambda b, c: (0, 0)
    rowspec = pl.BlockSpec((1, W), c2)
    NS = 2 if B % 2 == 0 else 1
    return pl.pallas_call(
        _rwkv_kernel,
        grid=(B // NS, NC),
        in_specs=[pl.BlockSpec((NS, CHUNK, RW_COLS), lambda b, c: (b, c, 0)),
                  pl.BlockSpec((NS, 8, RW_COLS), lambda b, c: (b, jnp.maximum(c * (CHUNK // 8) - 1, 0), 0)),
                  pl.BlockSpec((1, RW_COLS), c2),
                  rowspec,
                  pl.BlockSpec((DECAY_LORA, W), c2),
                  rowspec,
                  pl.BlockSpec((AAA_LORA, W), c2),
                  pl.BlockSpec((GATE_PAD, W), c2),
                  rowspec, rowspec, rowspec, rowspec, rowspec],
        out_specs=pl.BlockSpec((NS, CHUNK, W), lambda b, c: (b, c, 0)),
        out_shape=jax.ShapeDtypeStruct((B, T, W), F32),
        scratch_shapes=[pltpu.VMEM((NS * H_RWKV, HD_RWKV, HD_RWKV), F32)],
        compiler_params=_cp(("parallel", "arbitrary")),
        name="rwkv",
    )(rw, rw, mu_p, w0, wdec, a0, waaa, wgate_p, k_k, k_a, r_k, lng, lnb)


def _ffn_kernel(x_ref, ya_ref, yr_ref, mod_ref, gpost_ref, gpre_ref, gfpost_ref,
                wo_ref, wg_ref, wu_ref, wd_ref, o_ref, x1_scr, hf_scr, acc_scr):
    j = pl.program_id(2)

    @pl.when(j == 0)
    def _():
        mix = (_dot(ya_ref[0].astype(BF16), wo_ref[0:ATT_WIDTH, :])
               + _dot(yr_ref[0].astype(BF16), wo_ref[ATT_WIDTH:, :]))
        x1 = x_ref[0] + mod_ref[0, 2:3, :] * _rms(mix, gpost_ref[...])
        x1_scr[...] = x1
        hf = _rms(x1, gpre_ref[...]) * (1.0 + mod_ref[0, 4:5, :]) + mod_ref[0, 3:4, :]
        hf_scr[...] = hf.astype(BF16)
        acc_scr[...] = jnp.zeros(acc_scr.shape, F32)

    hf = hf_scr[...]
    gate = _dot(hf, wg_ref[...])
    up = _dot(hf, wu_ref[...])
    act = (gate * jax.nn.sigmoid(gate) * up).astype(BF16)
    acc_scr[...] += _dot(act, wd_ref[...])

    @pl.when(j == pl.num_programs(2) - 1)
    def _():
        o_ref[0] = x1_scr[...] + mod_ref[0, 5:6, :] * _rms(acc_scr[...], gfpost_ref[...])


def _ffn_call(x, y_att, y_rwkv, mod3, g_post, g_pre, g_fpost, wo, wg, wu, wd):
    B, T, D = x.shape
    TM = 512
    TF = D_FF // 2
    c3 = lambda b, i, j: (0, 0)
    return pl.pallas_call(
        _ffn_kernel,
        grid=(B, T // TM, D_FF // TF),
        in_specs=[pl.BlockSpec((1, TM, D), lambda b, i, j: (b, i, 0)),
                  pl.BlockSpec((1, TM, ATT_WIDTH), lambda b, i, j: (b, i, 0)),
                  pl.BlockSpec((1, TM, RWKV_WIDTH), lambda b, i, j: (b, i, 0)),
                  pl.BlockSpec((1, 6, D), lambda b, i, j: (b, 0, 0)),
                  pl.BlockSpec((1, D), c3), pl.BlockSpec((1, D), c3), pl.BlockSpec((1, D), c3),
                  pl.BlockSpec((D, D), c3),
                  pl.BlockSpec((D, TF), lambda b, i, j: (0, j)),
                  pl.BlockSpec((D, TF), lambda b, i, j: (0, j)),
                  pl.BlockSpec((TF, D), lambda b, i, j: (j, 0))],
        out_specs=pl.BlockSpec((1, TM, D), lambda b, i, j: (b, i, 0)),
        out_shape=jax.ShapeDtypeStruct((B, T, D), F32),
        scratch_shapes=[pltpu.VMEM((TM, D), F32), pltpu.VMEM((TM, D), BF16), pltpu.VMEM((TM, D), F32)],
        compiler_params=_cp(("parallel", "parallel", "arbitrary")),
        name="ffn",
    )(x, y_att, y_rwkv, mod3, g_post, g_pre, g_fpost, wo, wg, wu, wd)


def _pad_cols(w, n):
    return jnp.pad(w, ((0, 0), (0, n - w.shape[1])))


def _layer(x, mod3, rel_bias, mix_pre_norm, mix_post_norm, ffn_pre_norm, ffn_post_norm,
           w_in, q_norm, w_uq, w_idx_q, kv_norm, idx_k_norm, w_uk, w_uv,
           mu_shift, w0, w_decay_up, a0, w_aaa_up, w_gate_up, k_k, k_a, r_k, ln_x_gain, ln_x_bias,
           w_out, w_ffn_gate, w_ffn_up, w_ffn_down):
    B, T, D = x.shape
    att_cols = Q_LORA + KV_LATENT + IDX_DIM + IDX_HEADS
    n_main = 3 * RWKV_WIDTH + DECAY_LORA + AAA_LORA
    w_att = _pad_cols(w_in[:, :att_cols], ATT_PAD)
    w_rw = w_in[:, att_cols:]
    w_in_p = jnp.concatenate([w_att, w_rw[:, :n_main], _pad_cols(w_rw[:, n_main:], GATE_PAD)], axis=1).astype(BF16)
    mu_p = jnp.concatenate([mu_shift[:n_main], jnp.pad(mu_shift[n_main:], (0, GATE_PAD - GATE_LORA))]).reshape(1, RW_COLS)
    wgate_p = jnp.pad(w_gate_up, ((0, GATE_PAD - GATE_LORA), (0, 0))).astype(BF16)

    att, rw = _inproj_call(x, mod3, mix_pre_norm.reshape(1, D), w_in_p)

    wuq = w_uq.reshape(Q_LORA, ATT_WIDTH).astype(BF16)
    wiq = w_idx_q.reshape(Q_LORA, IDX_HEADS * IDX_DIM).astype(BF16)
    wukt = jnp.transpose(w_uk, (0, 2, 1)).astype(BF16)
    qabs, iq, ik, ckv, ckvt, iwt = _dsaprep_call(att, q_norm.reshape(1, -1), kv_norm.reshape(1, -1),
                                                 idx_k_norm.reshape(1, -1), wuq, wiq, wukt)
    bias_tiles = _bias_call(rel_bias)
    wuvt = jnp.transpose(w_uv, (0, 2, 1)).astype(BF16)
    topk = min(TOPK_MAX, T // 4)
    y_att = _dsa_call(iq, iwt, qabs, ik, ckv, ckvt, bias_tiles, wuvt, topk)

    row = lambda z: z.reshape(1, RWKV_WIDTH)
    y_rwkv = _rwkv_call(rw, mu_p, row(w0), w_decay_up.astype(BF16), row(a0), w_aaa_up.astype(BF16),
                        wgate_p, row(k_k), row(k_a), row(r_k), row(ln_x_gain), row(ln_x_bias))

    return _ffn_call(x, y_att, y_rwkv, mod3, mix_post_norm.reshape(1, D), ffn_pre_norm.reshape(1, D),
                     ffn_post_norm.reshape(1, D), w_out.astype(BF16), w_ffn_gate.astype(BF16),
                     w_ffn_up.astype(BF16), w_ffn_down.astype(BF16))


def kernel(x, c, rel_bias, ada_w, ada_b, mix_pre_norm, mix_post_norm, ffn_pre_norm, ffn_post_norm, w_in, q_norm, w_uq, w_idx_q, kv_norm, idx_k_norm, w_uk, w_uv, mu_shift, w0, w_decay_up, a0, w_aaa_up, w_gate_up, k_k, k_a, r_k, ln_x_gain, ln_x_bias, w_out, w_ffn_gate, w_ffn_up, w_ffn_down):
    B, T, D = x.shape
    assert D == D_MODEL and T % 512 == 0 and T // 4 >= 1
    layer_params = (mix_pre_norm, mix_post_norm, ffn_pre_norm, ffn_post_norm,
                    w_in, q_norm, w_uq, w_idx_q, kv_norm, idx_k_norm, w_uk, w_uv,
                    mu_shift, w0, w_decay_up, a0, w_aaa_up, w_gate_up, k_k, k_a, r_k, ln_x_gain, ln_x_bias,
                    w_out, w_ffn_gate, w_ffn_up, w_ffn_down)
    for l in range(ada_w.shape[0]):
        mod3 = _mod_call(c, ada_w[l], ada_b[l]).reshape(B, 6, D)
        x = _layer(x, mod3, rel_bias, *[p[l] for p in layer_params])
    return x
```

```python
import functools
import math

import numpy as np
import jax
import jax.numpy as jnp
from jax import lax
from jax.experimental import pallas as pl
from jax.experimental.pallas import tpu as pltpu

F32 = jnp.float32
BF16 = jnp.bfloat16

D_MODEL = 1024
ATT_WIDTH = 512
RWKV_WIDTH = 512
HD_ATT = 64
H_ATT = 8
HD_RWKV = 64
H_RWKV = 8
Q_LORA = 256
KV_LATENT = 128
IDX_HEADS = 8
IDX_DIM = 64
TOPK_MAX = 256
NUM_BUCKETS = 32
MAX_DISTANCE = 128
DECAY_LORA = 64
AAA_LORA = 64
GATE_LORA = 160
GN_EPS = 64e-5
D_FF = 2816
NORM_EPS = 1e-6

ATT_PAD = 512
GATE_PAD = 256
RW_COLS = 3 * RWKV_WIDTH + DECAY_LORA + AAA_LORA + GATE_PAD
OFF_WD = 3 * RWKV_WIDTH
OFF_AD = OFF_WD + DECAY_LORA
OFF_GD = OFF_AD + AAA_LORA

TQ = 128
CHUNK = 64
NEG_BIG = -1e30
LOWEST = -3.0e38
LOG2E = math.log2(math.e)
VMEM_LIMIT = 56 * 1024 * 1024


def _cp(sem):
    return pltpu.CompilerParams(dimension_semantics=sem, vmem_limit_bytes=VMEM_LIMIT)


def _dot(a, b):
    return jnp.dot(a, b, preferred_element_type=F32)


def _dot_nt(a, b):
    return lax.dot_general(a, b, (((1,), (1,)), ((), ())), preferred_element_type=F32)


def _dot_tn(a, b):
    return lax.dot_general(a, b, (((0,), (0,)), ((), ())), preferred_element_type=F32)


def _rms(z, gain):
    return z * lax.rsqrt(jnp.mean(z * z, axis=-1, keepdims=True) + NORM_EPS) * gain


def _mod_kernel(c_ref, w_ref, b_ref, o_ref):
    c = c_ref[...]
    s = c * jax.nn.sigmoid(c)
    o_ref[...] = _dot(s.astype(BF16), w_ref[...].astype(BF16)) + b_ref[...]


def _mod_call(c, ada_w, ada_b):
    B, D = c.shape
    N = ada_w.shape[1]
    TN = 1536
    return pl.pallas_call(
        _mod_kernel,
        grid=(N // TN,),
        in_specs=[pl.BlockSpec((B, D), lambda j: (0, 0)),
                  pl.BlockSpec((D, TN), lambda j: (0, j)),
                  pl.BlockSpec((1, TN), lambda j: (0, j))],
        out_specs=pl.BlockSpec((B, TN), lambda j: (0, j)),
        out_shape=jax.ShapeDtypeStruct((B, N), F32),
        compiler_params=_cp(("parallel",)),
        name="mod",
    )(c, ada_w, ada_b.reshape(1, N))


def _inproj_kernel(x_ref, mod_ref, g_ref, w_ref, att_ref, rw_ref):
    x = x_ref[0]
    y = _rms(x, g_ref[...])
    sh = mod_ref[0, 0:1, :]
    sc = mod_ref[0, 1:2, :]
    h = (y * (1.0 + sc) + sh).astype(BF16)
    att_ref[0] = _dot(h, w_ref[:, 0:ATT_PAD])
    for n0 in range(0, RW_COLS, 384):
        rw_ref[0, :, n0:n0 + 384] = _dot(h, w_ref[:, ATT_PAD + n0:ATT_PAD + n0 + 384])


def _inproj_call(x, mod3, gain, w_in_p):
    B, T, D = x.shape
    TM = 512
    NP = w_in_p.shape[1]
    return pl.pallas_call(
        _inproj_kernel,
        grid=(B, T // TM),
        in_specs=[pl.BlockSpec((1, TM, D), lambda b, i: (b, i, 0)),
                  pl.BlockSpec((1, 6, D), lambda b, i: (b, 0, 0)),
                  pl.BlockSpec((1, D), lambda b, i: (0, 0)),
                  pl.BlockSpec((D, NP), lambda b, i: (0, 0))],
        out_specs=[pl.BlockSpec((1, TM, ATT_PAD), lambda b, i: (b, i, 0)),
                   pl.BlockSpec((1, TM, RW_COLS), lambda b, i: (b, i, 0))],
        out_shape=[jax.ShapeDtypeStruct((B, T, ATT_PAD), F32),
                   jax.ShapeDtypeStruct((B, T, RW_COLS), F32)],
        compiler_params=_cp(("parallel", "parallel")),
        name="inproj",
    )(x, mod3, gain, w_in_p)


def _dsaprep_kernel(att_ref, qn_ref, kvn_ref, ikn_ref, wuq_ref, wiq_ref, wukt_ref,
                    qabs_ref, iq_ref, ik_ref, ckv_ref, ckvt_ref, iwt_ref):
    att = att_ref[0]
    tm = att.shape[0]
    o1 = Q_LORA
    o2 = o1 + KV_LATENT
    o3 = o2 + IDX_DIM
    cq = _rms(att[:, 0:o1], qn_ref[...]).astype(BF16)
    ckv = _rms(att[:, o1:o2], kvn_ref[...])
    ckv_ref[0] = ckv.astype(BF16)
    ik_ref[0] = _rms(att[:, o2:o3], ikn_ref[...]).astype(BF16)
    tail = att[:, o2:ATT_PAD]
    w_off = o3 - o2
    for sb in range(tm // TQ):
        rows = slice(sb * TQ, (sb + 1) * TQ)
        ckvt_ref[0, sb] = ckv[rows, :].T.astype(BF16)
        iwt_ref[0, :, rows] = tail[rows, :].T[w_off:w_off + IDX_HEADS, :] * (IDX_HEADS ** -0.5 * IDX_DIM ** -0.5)
    q = _dot(cq, wuq_ref[...])
    iqv = _dot(cq, wiq_ref[...])
    for h in range(H_ATT):
        qh = q[:, h * HD_ATT:(h + 1) * HD_ATT].astype(BF16)
        qa = _dot(qh, wukt_ref[h]) * (HD_ATT ** -0.5 * LOG2E)
        for sb in range(tm // TQ):
            qabs_ref[0, sb, :, h * TQ:(h + 1) * TQ] = qa[sb * TQ:(sb + 1) * TQ, :].T.astype(BF16)
    per = TQ // IDX_DIM
    for g in range(IDX_HEADS // per):
        for sb in range(tm // TQ):
            blk = iqv[sb * TQ:(sb + 1) * TQ, g * TQ:(g + 1) * TQ].T.astype(BF16)
            for u in range(per):
                h = g * per + u
                iq_ref[0, sb, :, h * TQ:(h + 1) * TQ] = blk[u * IDX_DIM:(u + 1) * IDX_DIM, :]


def _dsaprep_call(att, q_norm, kv_norm, idx_k_norm, wuq, wiq, wukt):
    B, T, _ = att.shape
    TM = 512
    const2 = lambda b, i: (0, 0)
    return pl.pallas_call(
        _dsaprep_kernel,
        grid=(B, T // TM),
        in_specs=[pl.BlockSpec((1, TM, ATT_PAD), lambda b, i: (b, i, 0)),
                  pl.BlockSpec((1, Q_LORA), const2),
                  pl.BlockSpec((1, KV_LATENT), const2),
                  pl.BlockSpec((1, IDX_DIM), const2),
                  pl.BlockSpec((Q_LORA, ATT_WIDTH), const2),
                  pl.BlockSpec((Q_LORA, IDX_HEADS * IDX_DIM), const2),
                  pl.BlockSpec((H_ATT, HD_ATT, KV_LATENT), lambda b, i: (0, 0, 0))],
        out_specs=[pl.BlockSpec((1, TM // TQ, KV_LATENT, H_ATT * TQ), lambda b, i: (b, i, 0, 0)),
                   pl.BlockSpec((1, TM // TQ, IDX_DIM, IDX_HEADS * TQ), lambda b, i: (b, i, 0, 0)),
                   pl.BlockSpec((1, TM, IDX_DIM), lambda b, i: (b, i, 0)),
                   pl.BlockSpec((1, TM, KV_LATENT), lambda b, i: (b, i, 0)),
                   pl.BlockSpec((1, TM // TQ, KV_LATENT, TQ), lambda b, i: (b, i, 0, 0)),
                   pl.BlockSpec((1, IDX_HEADS, TM), lambda b, i: (b, 0, i))],
        out_shape=[jax.ShapeDtypeStruct((B, T // TQ, KV_LATENT, H_ATT * TQ), BF16),
                   jax.ShapeDtypeStruct((B, T // TQ, IDX_DIM, IDX_HEADS * TQ), BF16),
                   jax.ShapeDtypeStruct((B, T, IDX_DIM), BF16),
                   jax.ShapeDtypeStruct((B, T, KV_LATENT), BF16),
                   jax.ShapeDtypeStruct((B, T // TQ, KV_LATENT, TQ), BF16),
                   jax.ShapeDtypeStruct((B, IDX_HEADS, T), F32)],
        compiler_params=_cp(("parallel", "parallel")),
        name="dsaprep",
    )(att, q_norm, kv_norm, idx_k_norm, wuq, wiq, wukt)


def _t5_bucket_np(rel):
    max_exact = NUM_BUCKETS // 2
    nf = np.maximum(rel, 1).astype(np.float32)
    large = max_exact + (np.log(nf / max_exact) / math.log(MAX_DISTANCE / max_exact)
                         * (NUM_BUCKETS - max_exact)).astype(np.int32)
    large = np.minimum(large, NUM_BUCKETS - 1)
    return np.where(rel < max_exact, rel, large).astype(np.int32)


def _near_bucket_tiles():
    s = np.arange(TQ)[:, None]
    t = np.arange(TQ)[None, :]
    tiles = [_t5_bucket_np(np.maximum(t - s + TQ * d, 0)) for d in range(2)]
    assert _t5_bucket_np(np.array([TQ + 1]))[0] == NUM_BUCKETS - 1
    return np.stack(tiles)


def _bias_kernel(rb_ref, bk_ref, o_ref):
    for d in range(2):
        bk = bk_ref[d]
        for h in range(H_ATT):
            def body(b, acc):
                return jnp.where(bk == b, rb_ref[b, h], acc)
            acc = lax.fori_loop(0, NUM_BUCKETS, body, jnp.zeros((TQ, TQ), F32))
            o_ref[d, h] = (acc - rb_ref[NUM_BUCKETS - 1, h]) * LOG2E
    o_ref[2] = jnp.zeros((H_ATT, TQ, TQ), F32)


def _bias_call(rel_bias):
    buckets = jnp.asarray(_near_bucket_tiles())
    return pl.pallas_call(
        _bias_kernel,
        in_specs=[pl.BlockSpec(memory_space=pltpu.SMEM),
                  pl.BlockSpec(memory_space=pltpu.VMEM)],
        out_specs=pl.BlockSpec(memory_space=pltpu.VMEM),
        out_shape=jax.ShapeDtypeStruct((3, H_ATT, TQ, TQ), F32),
        name="bias",
    )(rel_bias, buckets)


KV = TQ // 8
FREE_PROBES = 8


def _key_reduce(x, op):
    return op(op(x, axis=0), axis=0, keepdims=True)


def _dsa_kernel(iq_ref, iwt_ref, qabs_ref, ik_ref, ckv_ref, ckvt_ref, bias_ref, wuvt_ref, y_ref,
                score_scr, stage_a, stage_b, pv_a, pv_b, alpha_a, alpha_b, acc_scr, *, topk):
    i = pl.program_id(1)
    kf = float(topk)
    kidx = (lax.broadcasted_iota(jnp.int32, (KV, 8, TQ), 0) * 8
            + lax.broadcasted_iota(jnp.int32, (KV, 8, TQ), 1))
    qidx = lax.broadcasted_iota(jnp.int32, (KV, 8, TQ), 2)
    n_pairs = i // 2 + 1

    iwt = iwt_ref[0]
    w8 = [jnp.broadcast_to(iwt[h:h + 1, :], (8, TQ))[None] for h in range(IDX_HEADS)]
    iq2 = iq_ref[0, 0]

    n_trips = (n_pairs + 1) // 2
    stage = (stage_a, stage_b)

    def stage_pair(q, lhs_ref, rhs, dst):
        nxt = jnp.minimum(q, n_pairs - 1)
        rows = lhs_ref[0, pl.ds(pl.multiple_of(2 * nxt * TQ, 2 * TQ), 2 * TQ), :]
        dst[...] = _dot(rows, rhs).reshape(2, TQ, H_ATT * TQ)

    def two_halves(half):
        def trip(r, carry):
            carry = half(2 * r, 0, carry)
            return lax.cond(2 * r + 1 < n_pairs, lambda c: half(2 * r + 1, 1, c), lambda c: c, carry)
        return trip

    def score_half(q, par, carry):
        rmax, rmin = carry
        stage_pair(q + 1, ik_ref, iq2, stage[1 - par])
        for blk in range(2):
            j = 2 * q + blk
            sc = jnp.zeros((KV, 8, TQ), F32)
            for h in range(IDX_HEADS):
                s = stage[par][blk, :, h * TQ:(h + 1) * TQ].reshape(KV, 8, TQ)
                sc = sc + jnp.maximum(s, 0.0) * w8[h]
            causal = kidx + (j - i) * TQ <= qidx
            score_scr[j] = jnp.where(causal, sc, -jnp.inf)
            rmax = jnp.maximum(rmax, jnp.max(jnp.where(causal, sc, -jnp.inf), axis=0))
            rmin = jnp.minimum(rmin, jnp.min(jnp.where(causal, sc, jnp.inf), axis=0))
        return rmax, rmin

    stage_pair(0, ik_ref, iq2, stage[0])
    rmax, rmin = lax.fori_loop(0, n_trips, two_halves(score_half),
                               (jnp.full((8, TQ), -jnp.inf, F32), jnp.full((8, TQ), jnp.inf, F32)))
    smax = jnp.max(rmax, axis=0, keepdims=True)
    smin = jnp.min(rmin, axis=0, keepdims=True)

    nvalid = (i * TQ + lax.broadcasted_iota(jnp.int32, (1, TQ), 1) + 1).astype(F32)
    needs_thr = nvalid > kf

    def probe(v):
        vb = jnp.broadcast_to(v, (8, TQ))[None, None]

        def body(q, st):
            cnt, dn, up = st
            for j in (2 * q, 2 * q + 1):
                s = score_scr[j].reshape(4, KV // 4, 8, TQ)
                ge = s >= vb
                cnt = cnt + jnp.sum(jnp.where(ge, 1.0, 0.0), axis=1)
                dn = jnp.minimum(dn, jnp.min(jnp.where(ge, s, jnp.inf), axis=1))
                up = jnp.maximum(up, jnp.max(jnp.where(ge, -jnp.inf, s), axis=1))
            return cnt, dn, up
        cnt, dn, up = lax.fori_loop(0, n_pairs, body, (jnp.zeros((4, 8, TQ), F32),
                                                       jnp.full((4, 8, TQ), jnp.inf, F32),
                                                       jnp.full((4, 8, TQ), -jnp.inf, F32)))
        return (jnp.sum(jnp.sum(cnt, axis=0), axis=0, keepdims=True),
                jnp.min(jnp.min(dn, axis=0), axis=0, keepdims=True),
                jnp.max(jnp.max(up, axis=0), axis=0, keepdims=True))

    def logit(c):
        return jnp.log((c + 0.5) / (nvalid - c + 0.5))

    g_k = logit(jnp.full((1, TQ), kf - 0.5, F32))

    def search_step(st):
        lo, hi, clo, ghi, slo, shi = st
        is_open = lo < hi
        g_lo = logit(clo)
        g_hi = logit(ghi)
        phi = jnp.clip((g_lo - g_k) / (g_lo - g_hi), 0.02, 0.98)
        stalled = (slo >= 2.0) | (shi >= 2.0)
        mid = jnp.where(stalled, 0.5 * lo + 0.5 * hi, lo + (hi - lo) * phi)
        mid = jnp.where((mid > lo) & (mid <= hi), mid, hi)
        c, dn, up = probe(mid)
        ge = c >= kf
        move_lo = is_open & ge
        move_hi = is_open & jnp.logical_not(ge)
        lo = jnp.where(move_lo, dn, lo)
        clo = jnp.where(move_lo, c, clo)
        hi = jnp.where(move_hi, up, hi)
        ghi = jnp.where(move_hi, c, ghi)
        slo = jnp.where(stalled | ge, 0.0, slo + 1.0)
        shi = jnp.where(stalled | jnp.logical_not(ge), 0.0, shi + 1.0)
        return lo, hi, clo, ghi, slo, shi

    def n_open(st):
        return jnp.sum(jnp.where(st[0] < st[1], 1.0, 0.0))

    def checked_step(carry):
        st = search_step(carry[0])
        return st, n_open(st), carry[2] + 1

    search_cap = ik_ref.shape[1] + 2
    zero = jnp.zeros((1, TQ), F32)
    st = (smin, jnp.where(needs_thr, smax, smin), nvalid, zero, zero, zero)
    st = lax.fori_loop(0, FREE_PROBES, lambda _, s: search_step(s), st)
    st, _, _ = lax.while_loop(lambda c: jnp.logical_and(c[1] > 0.0, c[2] < search_cap), checked_step,
                              (st, n_open(st), jnp.int32(0)))
    thr, _, clo, chi, _, _ = st

    tie = needs_thr & (clo > kf)
    n_tie = jnp.sum(jnp.where(tie, 1.0, 0.0))

    @pl.when(n_tie > 0.0)
    def _():
        r2 = lax.broadcasted_iota(jnp.int32, (TQ, TQ), 0)
        c2 = lax.broadcasted_iota(jnp.int32, (TQ, TQ), 1)
        lower = jnp.where(c2 <= r2, 1.0, 0.0).astype(BF16)
        thrb = jnp.broadcast_to(thr, (TQ, TQ))
        tieb = jnp.broadcast_to(tie, (TQ, TQ))
        needb = jnp.broadcast_to(kf - chi, (TQ, TQ))

        def body(j, carry):
            s = score_scr[j].reshape(TQ, TQ)
            eq = (s == thrb) & tieb
            pref = _dot(lower, jnp.where(eq, 1.0, 0.0).astype(BF16)) + carry
            score_scr[j] = jnp.where(eq & (pref > needb), -jnp.inf, s).reshape(KV, 8, TQ)
            return jnp.broadcast_to(pref[TQ - 1:TQ, :], (TQ, TQ))
        lax.fori_loop(0, i + 1, body, jnp.zeros((TQ, TQ), F32))

    thr_eff = jnp.broadcast_to(jnp.where(needs_thr, thr, LOWEST), (8, TQ))[None]

    acc_scr[...] = jnp.zeros(acc_scr.shape, F32)
    qa2 = qabs_ref[0, 0]

    pv = (pv_a, pv_b)
    alpha_st = (alpha_a, alpha_b)

    def apply_pending(par):
        acc_scr[...] = alpha_st[par][...] * acc_scr[...] + pv[par][...]

    def attend_variant(q, par, carry, near):
        m, l = list(carry[0]), list(carry[1])
        stage_pair(q + 1, ckv_ref, qa2, stage[1 - par])
        js = (2 * q, 2 * q + 1)
        sel = [score_scr[j] >= thr_eff for j in js]
        d = [jnp.clip(i - j, 0, 2) for j in js]
        kvt = jnp.concatenate([ckvt_ref[0, js[0]], ckvt_ref[0, js[1]]], axis=1)
        ps, alphas = [], []
        for h in range(H_ATT):
            lg = [stage[par][b, :, h * TQ:(h + 1) * TQ] for b in range(2)]
            if near:
                lg = [lg[b] + bias_ref[d[b], h] for b in range(2)]
            lg = [jnp.where(sel[b], lg[b].reshape(KV, 8, TQ), NEG_BIG) for b in range(2)]
            m_new = jnp.maximum(m[h], jnp.maximum(_key_reduce(lg[0], jnp.max), _key_reduce(lg[1], jnp.max)))
            alpha = jnp.exp2(m[h] - m_new)
            mb = jnp.broadcast_to(m_new, (8, TQ))[None]
            p = [jnp.exp2(lg[b] - mb) for b in range(2)]
            l[h] = alpha * l[h] + _key_reduce(p[0], jnp.sum) + _key_reduce(p[1], jnp.sum)
            m[h] = m_new
            alphas.append(alpha)
            ps.append(jnp.concatenate([p[0].reshape(TQ, TQ), p[1].reshape(TQ, TQ)], axis=0).astype(BF16))
            if h % 2 == 1:
                cols = slice((h - 1) * TQ, (h + 1) * TQ)
                pv[par][:, cols] = _dot(kvt, jnp.concatenate(ps[h - 1:h + 1], axis=1))
        alpha_st[par][...] = jnp.concatenate(alphas, axis=1)
        apply_pending(1 - par)
        return tuple(m), tuple(l)

    def attend_half(q, par, carry):
        return lax.cond(2 * q + 1 >= i - 1,
                        lambda c: attend_variant(q, par, c, True),
                        lambda c: attend_variant(q, par, c, False), carry)

    pv_b[...] = jnp.zeros(pv_b.shape, F32)
    alpha_b[...] = jnp.ones(alpha_b.shape, F32)
    stage_pair(0, ckv_ref, qa2, stage[0])
    m0 = tuple(jnp.full((1, TQ), NEG_BIG, F32) for _ in range(H_ATT))
    l0 = tuple(jnp.zeros((1, TQ), F32) for _ in range(H_ATT))
    m, l = lax.fori_loop(0, n_trips, two_halves(attend_half), (m0, l0))
    for par in range(2):
        @pl.when((n_pairs - 1) % 2 == par)
        def _():
            apply_pending(par)

    yt = []
    for h in range(H_ATT):
        o = acc_scr[:, h * TQ:(h + 1) * TQ] / l[h]
        yt.append(_dot(wuvt_ref[h], o.astype(BF16)))
    for g in range(ATT_WIDTH // TQ):
        per = TQ // HD_ATT
        y_ref[0, :, g * TQ:(g + 1) * TQ] = jnp.concatenate(yt[g * per:(g + 1) * per], axis=0).T


def _dsa_call(iq, iwt, qabs, ik, ckv, ckvt, bias_tiles, wuvt, topk):
    B, T, _ = ik.shape
    NT = T // TQ
    return pl.pallas_call(
        functools.partial(_dsa_kernel, topk=topk),
        grid=(B, NT),
        in_specs=[pl.BlockSpec((1, 1, IDX_DIM, IDX_HEADS * TQ), lambda b, i: (b, i, 0, 0)),
                  pl.BlockSpec((1, IDX_HEADS, TQ), lambda b, i: (b, 0, i)),
                  pl.BlockSpec((1, 1, KV_LATENT, H_ATT * TQ), lambda b, i: (b, i, 0, 0)),
                  pl.BlockSpec((1, T, IDX_DIM), lambda b, i: (b, 0, 0)),
                  pl.BlockSpec((1, T, KV_LATENT), lambda b, i: (b, 0, 0)),
                  pl.BlockSpec((1, NT, KV_LATENT, TQ), lambda b, i: (b, 0, 0, 0)),
                  pl.BlockSpec((3, H_ATT, TQ, TQ), lambda b, i: (0, 0, 0, 0)),
                  pl.BlockSpec((H_ATT, HD_ATT, KV_LATENT), lambda b, i: (0, 0, 0))],
        out_specs=pl.BlockSpec((1, TQ, ATT_WIDTH), lambda b, i: (b, i, 0)),
        out_shape=jax.ShapeDtypeStruct((B, T, ATT_WIDTH), F32),
        scratch_shapes=[pltpu.VMEM((NT, KV, 8, TQ), F32),
                        pltpu.VMEM((2, TQ, H_ATT * TQ), F32),
                        pltpu.VMEM((2, TQ, H_ATT * TQ), F32),
                        pltpu.VMEM((KV_LATENT, H_ATT * TQ), F32),
                        pltpu.VMEM((KV_LATENT, H_ATT * TQ), F32),
                        pltpu.VMEM((1, H_ATT * TQ), F32),
                        pltpu.VMEM((1, H_ATT * TQ), F32),
                        pltpu.VMEM((KV_LATENT, H_ATT * TQ), F32)],
        compiler_params=_cp(("parallel", "arbitrary")),
        name="dsa",
    )(iq, iwt, qabs, ik, ckv, ckvt, bias_tiles, wuvt)


def _rwkv_kernel(p_ref, prev_ref, mu_ref, w0_ref, wdec_ref, a0_ref, waaa_ref, wgate_ref,
                 kk_ref, ka_ref, rk_ref, lng_ref, lnb_ref, y_ref, h_scr):
    c = pl.program_id(1)
    C = CHUNK
    N = HD_RWKV

    @pl.when(c == 0)
    def _():
        h_scr[...] = jnp.zeros(h_scr.shape, F32)

    NS = p_ref.shape[0]
    p = p_ref[...].reshape(NS * C, RW_COLS)
    rid = lax.broadcasted_iota(jnp.int32, (NS * C, 1), 0)
    p_prev = pltpu.roll(p, 1, 0)
    for s in range(NS):
        p_prev = jnp.where(rid == s * C, jnp.where(c > 0, prev_ref[s, 7:8, :], 0.0), p_prev)
    p = p + mu_ref[...] * (p_prev - p)

    W = RWKV_WIDTH
    r = p[:, 0:W]
    k = p[:, W:2 * W]
    v = p[:, 2 * W:3 * W]
    wd = p[:, OFF_WD:OFF_WD + DECAY_LORA]
    ad = p[:, OFF_AD:OFF_AD + AAA_LORA]
    gd = p[:, OFF_GD:OFF_GD + GATE_PAD]

    w_log = -jax.nn.softplus(-(w0_ref[...] + _dot(jnp.tanh(wd).astype(BF16), wdec_ref[...]))) - 0.5
    logw = -jnp.exp(w_log)
    a = jax.nn.sigmoid(a0_ref[...] + _dot(ad.astype(BF16), waaa_ref[...]))
    g = _dot(jax.nn.sigmoid(gd).astype(BF16), wgate_ref[...])

    r64 = lax.broadcasted_iota(jnp.int32, (C, C), 0)
    c64 = lax.broadcasted_iota(jnp.int32, (C, C), 1)
    tril_incl = r64 >= c64
    tril_strict = r64 > c64
    eye = r64 == c64
    ra = lax.broadcasted_iota(jnp.int32, (NS * C, NS * C), 0)
    ca = lax.broadcasted_iota(jnp.int32, (NS * C, NS * C), 1)
    same_seq = (ra >= ca) & (ra - ca <= ra % C)
    cum = jnp.dot(jnp.where(same_seq, 1.0, 0.0), logw, preferred_element_type=F32,
                  precision=lax.Precision.HIGHEST)
    cum_last = cum[C - 1:C, :]
    for s in range(1, NS):
        cum_last = jnp.where(rid >= s * C, cum[(s + 1) * C - 1:(s + 1) * C, :], cum_last)
    g_in = jnp.exp(cum - logw)
    g_t = jnp.exp(cum)
    g_inv = jnp.exp(-cum)
    g_end = jnp.exp(cum_last - cum)
    g_all = jnp.exp(cum_last)

    kkf = k * kk_ref[...]
    kmod = k * (1.0 + (a - 1.0) * ka_ref[...])
    rkr = r * kmod * rk_ref[...]

    P = range(NS * H_RWKV // 2)
    rows = [slice((p // (H_RWKV // 2)) * C, (p // (H_RWKV // 2) + 1) * C) for p in P]
    sls = [slice((p % (H_RWKV // 2)) * 2 * N, (p % (H_RWKV // 2) + 1) * 2 * N) for p in P]
    lane2 = lax.broadcasted_iota(jnp.int32, (C, 2 * N), 1)
    row2 = lax.broadcasted_iota(jnp.int32, (C, 2 * N), 0)
    left = lane2 < N
    col2 = jnp.where(left, lane2, lane2 - N)
    strict2 = row2 > col2
    incl2 = row2 >= col2
    eye2 = row2 == col2

    def per_head(x, op):
        return jnp.where(left, op(jnp.where(left, x, 0.0), axis=-1, keepdims=True),
                         op(jnp.where(left, 0.0, x), axis=-1, keepdims=True))

    def block_diag(x):
        zero = jnp.zeros_like(x)
        return jnp.concatenate([jnp.where(left, x, zero), jnp.where(left, zero, x)], axis=0)

    at, rt, bh, kh, vb, g4 = [], [], [], [], [], []
    for p in P:
        rs, sl = rows[p], sls[p]
        kk_p = kkf[rs, sl]
        kk_p = kk_p / jnp.maximum(jnp.sqrt(per_head(kk_p * kk_p, jnp.sum)), 1e-12)
        k_p = kmod[rs, sl]
        bvec = kk_p * a[rs, sl]
        at_p = ((-kk_p) * g_in[rs, sl]).astype(BF16)
        rt_p = (r[rs, sl] * g_t[rs, sl]).astype(BF16)
        bt_p = (bvec * g_inv[rs, sl]).astype(BF16)
        kt_p = (k_p * g_inv[rs, sl]).astype(BF16)
        wt = jnp.concatenate([block_diag(bt_p), block_diag(kt_p)], axis=0)
        g4.append(_dot_nt(jnp.concatenate([at_p, rt_p], axis=0), wt))
        at.append(at_p)
        rt.append(rt_p)
        bh.append((bvec * g_end[rs, sl]).astype(BF16))
        kh.append((k_p * g_end[rs, sl]).astype(BF16))
        vb.append(v[rs, sl].astype(BF16))

    a_ab = [jnp.where(strict2, g4[p][0:C, 0:2 * C], 0.0) for p in P]
    a_ak = [jnp.where(strict2, g4[p][0:C, 2 * C:4 * C], 0.0).astype(BF16) for p in P]
    a_rb = [jnp.where(incl2, g4[p][C:2 * C, 0:2 * C], 0.0).astype(BF16) for p in P]
    a_rk = [jnp.where(incl2, g4[p][C:2 * C, 2 * C:4 * C], 0.0).astype(BF16) for p in P]

    pw = a_ab
    tinv = [jnp.where(eye2, 1.0, 0.0) for p in P]
    for level in range(6):
        pwb = [pw[p].astype(BF16) for p in P]
        bd = [block_diag(pwb[p]) for p in P]
        if level < 5:
            res = [_dot(jnp.concatenate([pwb[p], tinv[p].astype(BF16)], axis=0), bd[p]) for p in P]
            pw = [res[p][0:C] for p in P]
            tinv = [tinv[p] + res[p][C:2 * C] for p in P]
        else:
            tinv = [tinv[p] + _dot(tinv[p].astype(BF16), bd[p]) for p in P]

    hst = [h_scr[p] for p in P]
    hbd = [block_diag(hst[p].astype(BF16)) for p in P]
    vbd = [block_diag(vb[p]) for p in P]
    x = [_dot(jnp.concatenate([at[p], a_ak[p]], axis=1), jnp.concatenate([hbd[p], vbd[p]], axis=0)) for p in P]
    ub = [_dot(tinv[p].astype(BF16), block_diag(x[p].astype(BF16))).astype(BF16) for p in P]
    y = [_dot(jnp.concatenate([rt[p], a_rb[p], a_rk[p]], axis=1),
              jnp.concatenate([hbd[p], block_diag(ub[p]), vbd[p]], axis=0)) for p in P]
    for p in P:
        ga = g_all[rows[p], sls[p]]
        g_col = jnp.where(left, jnp.sum(jnp.where(row2 == lane2, ga, 0.0), axis=1, keepdims=True),
                          jnp.sum(jnp.where(row2 == lane2 - N, ga, 0.0), axis=1, keepdims=True))
        res = _dot_tn(jnp.concatenate([bh[p], kh[p]], axis=0), jnp.concatenate([ub[p], vb[p]], axis=0))
        h_scr[p] = g_col * hst[p] + jnp.where(left, res[0:N], res[N:2 * N])

    for p in P:
        rs, sl = rows[p], sls[p]
        mean = per_head(y[p], jnp.sum) * (1.0 / N)
        yc = y[p] - mean
        var = per_head(yc * yc, jnp.sum) * (1.0 / N)
        yn = yc * lax.rsqrt(var + GN_EPS) * lng_ref[:, sl] + lnb_ref[:, sl]
        bonus = per_head(rkr[rs, sl], jnp.sum) * v[rs, sl]
        y_ref[p // (H_RWKV // 2), :, sl] = (yn + bonus) * g[rs, sl]


def _rwkv_call(rw, mu_p, w0, wdec, a0, waaa, wgate_p, k_k, k_a, r_k, lng, lnb):
    B, T, _ = rw.shape
    NC = T // CHUNK
    W = RWKV_WIDTH
    c2 = lambda b, c: (0, 0)
    rowspec = pl.BlockSpec((1, W), c2)
    NS = 2 if B % 2 == 0 else 1
    return pl.pallas_call(
        _rwkv_kernel,
        grid=(B // NS, NC),
        in_specs=[pl.BlockSpec((NS, CHUNK, RW_COLS), lambda b, c: (b, c, 0)),
                  pl.BlockSpec((NS, 8, RW_COLS), lambda b, c: (b, jnp.maximum(c * (CHUNK // 8) - 1, 0), 0)),
                  pl.BlockSpec((1, RW_COLS), c2),
                  rowspec,
                  pl.BlockSpec((DECAY_LORA, W), c2),
                  rowspec,
                  pl.BlockSpec((AAA_LORA, W), c2),
                  pl.BlockSpec((GATE_PAD, W), c2),
                  rowspec, rowspec, rowspec, rowspec, rowspec],
        out_specs=pl.BlockSpec((NS, CHUNK, W), lambda b, c: (b, c, 0)),
        out_shape=jax.ShapeDtypeStruct((B, T, W), F32),
        scratch_shapes=[pltpu.VMEM((NS * H_RWKV // 2, HD_RWKV, 2 * HD_RWKV), F32)],
        compiler_params=_cp(("parallel", "arbitrary")),
        name="rwkv",
    )(rw, rw, mu_p, w0, wdec, a0, waaa, wgate_p, k_k, k_a, r_k, lng, lnb)


def _ffn_kernel(x_ref, ya_ref, yr_ref, mod_ref, gpost_ref, gpre_ref, gfpost_ref,
                wo_ref, wg_ref, wu_ref, wd_ref, o_ref, x1_scr, hf_scr, acc_scr):
    j = pl.program_id(2)

    @pl.when(j == 0)
    def _():
        mix = (_dot(ya_ref[0].astype(BF16), wo_ref[0:ATT_WIDTH, :])
               + _dot(yr_ref[0].astype(BF16), wo_ref[ATT_WIDTH:, :]))
        x1 = x_ref[0] + mod_ref[0, 2:3, :] * _rms(mix, gpost_ref[...])
        x1_scr[...] = x1
        hf = _rms(x1, gpre_ref[...]) * (1.0 + mod_ref[0, 4:5, :]) + mod_ref[0, 3:4, :]
        hf_scr[...] = hf.astype(BF16)
        acc_scr[...] = jnp.zeros(acc_scr.shape, F32)

    hf = hf_scr[...]
    gate = _dot(hf, wg_ref[...])
    up = _dot(hf, wu_ref[...])
    act = (gate * jax.nn.sigmoid(gate) * up).astype(BF16)
    acc_scr[...] += _dot(act, wd_ref[...])

    @pl.when(j == pl.num_programs(2) - 1)
    def _():
        o_ref[0] = x1_scr[...] + mod_ref[0, 5:6, :] * _rms(acc_scr[...], gfpost_ref[...])


def _ffn_call(x, y_att, y_rwkv, mod3, g_post, g_pre, g_fpost, wo, wg, wu, wd):
    B, T, D = x.shape
    TM = 512
    TF = D_FF // 2
    c3 = lambda b, i, j: (0, 0)
    return pl.pallas_call(
        _ffn_kernel,
        grid=(B, T // TM, D_FF // TF),
        in_specs=[pl.BlockSpec((1, TM, D), lambda b, i, j: (b, i, 0)),
                  pl.BlockSpec((1, TM, ATT_WIDTH), lambda b, i, j: (b, i, 0)),
                  pl.BlockSpec((1, TM, RWKV_WIDTH), lambda b, i, j: (b, i, 0)),
                  pl.BlockSpec((1, 6, D), lambda b, i, j: (b, 0, 0)),
                  pl.BlockSpec((1, D), c3), pl.BlockSpec((1, D), c3), pl.BlockSpec((1, D), c3),
                  pl.BlockSpec((D, D), c3),
                  pl.BlockSpec((D, TF), lambda b, i, j: (0, j)),
                  pl.BlockSpec((D, TF), lambda b, i, j: (0, j)),
                  pl.BlockSpec((TF, D), lambda b, i, j: (j, 0))],
        out_specs=pl.BlockSpec((1, TM, D), lambda b, i, j: (b, i, 0)),
        out_shape=jax.ShapeDtypeStruct((B, T, D), F32),
        scratch_shapes=[pltpu.VMEM((TM, D), F32), pltpu.VMEM((TM, D), BF16), pltpu.VMEM((TM, D), F32)],
        compiler_params=_cp(("parallel", "parallel", "arbitrary")),
        name="ffn",
    )(x, y_att, y_rwkv, mod3, g_post, g_pre, g_fpost, wo, wg, wu, wd)


def _pad_cols(w, n):
    return jnp.pad(w, ((0, 0), (0, n - w.shape[1])))


def _layer(x, mod3, rel_bias, mix_pre_norm, mix_post_norm, ffn_pre_norm, ffn_post_norm,
           w_in, q_norm, w_uq, w_idx_q, kv_norm, idx_k_norm, w_uk, w_uv,
           mu_shift, w0, w_decay_up, a0, w_aaa_up, w_gate_up, k_k, k_a, r_k, ln_x_gain, ln_x_bias,
           w_out, w_ffn_gate, w_ffn_up, w_ffn_down):
    B, T, D = x.shape
    att_cols = Q_LORA + KV_LATENT + IDX_DIM + IDX_HEADS
    n_main = 3 * RWKV_WIDTH + DECAY_LORA + AAA_LORA
    w_att = _pad_cols(w_in[:, :att_cols], ATT_PAD)
    w_rw = w_in[:, att_cols:]
    w_in_p = jnp.concatenate([w_att, w_rw[:, :n_main], _pad_cols(w_rw[:, n_main:], GATE_PAD)], axis=1).astype(BF16)
    mu_p = jnp.concatenate([mu_shift[:n_main], jnp.pad(mu_shift[n_main:], (0, GATE_PAD - GATE_LORA))]).reshape(1, RW_COLS)
    wgate_p = jnp.pad(w_gate_up, ((0, GATE_PAD - GATE_LORA), (0, 0))).astype(BF16)

    att, rw = _inproj_call(x, mod3, mix_pre_norm.reshape(1, D), w_in_p)

    wuq = w_uq.reshape(Q_LORA, ATT_WIDTH).astype(BF16)
    wiq = w_idx_q.reshape(Q_LORA, IDX_HEADS * IDX_DIM).astype(BF16)
    wukt = jnp.transpose(w_uk, (0, 2, 1)).astype(BF16)
    qabs, iq, ik, ckv, ckvt, iwt = _dsaprep_call(att, q_norm.reshape(1, -1), kv_norm.reshape(1, -1),
                                                 idx_k_norm.reshape(1, -1), wuq, wiq, wukt)
    bias_tiles = _bias_call(rel_bias)
    wuvt = jnp.transpose(w_uv, (0, 2, 1)).astype(BF16)
    topk = min(TOPK_MAX, T // 4)
    y_att = _dsa_call(iq, iwt, qabs, ik, ckv, ckvt, bias_tiles, wuvt, topk)

    row = lambda z: z.reshape(1, RWKV_WIDTH)
    y_rwkv = _rwkv_call(rw, mu_p, row(w0), w_decay_up.astype(BF16), row(a0), w_aaa_up.astype(BF16),
                        wgate_p, row(k_k), row(k_a), row(r_k), row(ln_x_gain), row(ln_x_bias))

    return _ffn_call(x, y_att, y_rwkv, mod3, mix_post_norm.reshape(1, D), ffn_pre_norm.reshape(1, D),
                     ffn_post_norm.reshape(1, D), w_out.astype(BF16), w_ffn_gate.astype(BF16),
                     w_ffn_up.astype(BF16), w_ffn_down.astype(BF16))


def kernel(x, c, rel_bias, ada_w, ada_b, mix_pre_norm, mix_post_norm, ffn_pre_norm, ffn_post_norm, w_in, q_norm, w_uq, w_idx_q, kv_norm, idx_k_norm, w_uk, w_uv, mu_shift, w0, w_decay_up, a0, w_aaa_up, w_gate_up, k_k, k_a, r_k, ln_x_gain, ln_x_bias, w_out, w_ffn_gate, w_ffn_up, w_ffn_down):
    B, T, D = x.shape
    assert D == D_MODEL and T % 512 == 0 and T // 4 >= 1
    layer_params = (mix_pre_norm, mix_post_norm, ffn_pre_norm, ffn_post_norm,
                    w_in, q_norm, w_uq, w_idx_q, kv_norm, idx_k_norm, w_uk, w_uv,
                    mu_shift, w0, w_decay_up, a0, w_aaa_up, w_gate_up, k_k, k_a, r_k, ln_x_gain, ln_x_bias,
                    w_out, w_ffn_gate, w_ffn_up, w_ffn_down)
    for l in range(ada_w.shape[0]):
        mod3 = _mod_call(c, ada_w[l], ada_b[l]).reshape(B, 6, D)
        x = _layer(x, mod3, rel_bias, *[p[l] for p in layer_params])
    return x
```

```python
import functools
import math

import numpy as np
import jax
import jax.numpy as jnp
from jax import lax
from jax.experimental import pallas as pl
from jax.experimental.pallas import tpu as pltpu

F32 = jnp.float32
BF16 = jnp.bfloat16

D_MODEL = 1024
ATT_WIDTH = 512
RWKV_WIDTH = 512
HD_ATT = 64
H_ATT = 8
HD_RWKV = 64
H_RWKV = 8
Q_LORA = 256
KV_LATENT = 128
IDX_HEADS = 8
IDX_DIM = 64
TOPK_MAX = 256
NUM_BUCKETS = 32
MAX_DISTANCE = 128
DECAY_LORA = 64
AAA_LORA = 64
GATE_LORA = 160
GN_EPS = 64e-5
D_FF = 2816
NORM_EPS = 1e-6

ATT_PAD = 512
GATE_PAD = 256
RW_COLS = 3 * RWKV_WIDTH + DECAY_LORA + AAA_LORA + GATE_PAD
OFF_WD = 3 * RWKV_WIDTH
OFF_AD = OFF_WD + DECAY_LORA
OFF_GD = OFF_AD + AAA_LORA

TQ = 128
CHUNK = 64
NEG_BIG = -1e30
LOWEST = -3.0e38
LOG2E = math.log2(math.e)
VMEM_LIMIT = 56 * 1024 * 1024


def _cp(sem):
    return pltpu.CompilerParams(dimension_semantics=sem, vmem_limit_bytes=VMEM_LIMIT)


def _dot(a, b):
    return jnp.dot(a, b, preferred_element_type=F32)


def _dot_nt(a, b):
    return lax.dot_general(a, b, (((1,), (1,)), ((), ())), preferred_element_type=F32)


def _dot_tn(a, b):
    return lax.dot_general(a, b, (((0,), (0,)), ((), ())), preferred_element_type=F32)


def _rms(z, gain):
    return z * lax.rsqrt(jnp.mean(z * z, axis=-1, keepdims=True) + NORM_EPS) * gain


def _mod_kernel(c_ref, w_ref, b_ref, o_ref):
    c = c_ref[...]
    s = c * jax.nn.sigmoid(c)
    o_ref[...] = _dot(s.astype(BF16), w_ref[...].astype(BF16)) + b_ref[...]


def _mod_call(c, ada_w, ada_b):
    B, D = c.shape
    N = ada_w.shape[1]
    TN = 1536
    return pl.pallas_call(
        _mod_kernel,
        grid=(N // TN,),
        in_specs=[pl.BlockSpec((B, D), lambda j: (0, 0)),
                  pl.BlockSpec((D, TN), lambda j: (0, j)),
                  pl.BlockSpec((1, TN), lambda j: (0, j))],
        out_specs=pl.BlockSpec((B, TN), lambda j: (0, j)),
        out_shape=jax.ShapeDtypeStruct((B, N), F32),
        compiler_params=_cp(("parallel",)),
        name="mod",
    )(c, ada_w, ada_b.reshape(1, N))


def _inproj_kernel(x_ref, mod_ref, g_ref, w_ref, att_ref, rw_ref):
    x = x_ref[0]
    y = _rms(x, g_ref[...])
    sh = mod_ref[0, 0:1, :]
    sc = mod_ref[0, 1:2, :]
    h = (y * (1.0 + sc) + sh).astype(BF16)
    att_ref[0] = _dot(h, w_ref[:, 0:ATT_PAD])
    for n0 in range(0, RW_COLS, 384):
        rw_ref[0, :, n0:n0 + 384] = _dot(h, w_ref[:, ATT_PAD + n0:ATT_PAD + n0 + 384])


def _inproj_call(x, mod3, gain, w_in_p):
    B, T, D = x.shape
    TM = 512
    NP = w_in_p.shape[1]
    return pl.pallas_call(
        _inproj_kernel,
        grid=(B, T // TM),
        in_specs=[pl.BlockSpec((1, TM, D), lambda b, i: (b, i, 0)),
                  pl.BlockSpec((1, 6, D), lambda b, i: (b, 0, 0)),
                  pl.BlockSpec((1, D), lambda b, i: (0, 0)),
                  pl.BlockSpec((D, NP), lambda b, i: (0, 0))],
        out_specs=[pl.BlockSpec((1, TM, ATT_PAD), lambda b, i: (b, i, 0)),
                   pl.BlockSpec((1, TM, RW_COLS), lambda b, i: (b, i, 0))],
        out_shape=[jax.ShapeDtypeStruct((B, T, ATT_PAD), F32),
                   jax.ShapeDtypeStruct((B, T, RW_COLS), F32)],
        compiler_params=_cp(("parallel", "parallel")),
        name="inproj",
    )(x, mod3, gain, w_in_p)


def _dsaprep_kernel(att_ref, qn_ref, kvn_ref, ikn_ref, wuq_ref, wiq_ref, wukt_ref,
                    qabs_ref, iq_ref, ik_ref, ckv_ref, ckvt_ref, iwt_ref):
    att = att_ref[0]
    tm = att.shape[0]
    o1 = Q_LORA
    o2 = o1 + KV_LATENT
    o3 = o2 + IDX_DIM
    cq = _rms(att[:, 0:o1], qn_ref[...]).astype(BF16)
    ckv = _rms(att[:, o1:o2], kvn_ref[...])
    ckv_ref[0] = ckv.astype(BF16)
    ik_ref[0] = _rms(att[:, o2:o3], ikn_ref[...]).astype(BF16)
    tail = att[:, o2:ATT_PAD]
    w_off = o3 - o2
    for sb in range(tm // TQ):
        rows = slice(sb * TQ, (sb + 1) * TQ)
        ckvt_ref[0, sb] = ckv[rows, :].T.astype(BF16)
        iwt_ref[0, :, rows] = tail[rows, :].T[w_off:w_off + IDX_HEADS, :] * (IDX_HEADS ** -0.5 * IDX_DIM ** -0.5)
    q = _dot(cq, wuq_ref[...])
    iqv = _dot(cq, wiq_ref[...])
    for h in range(H_ATT):
        qh = q[:, h * HD_ATT:(h + 1) * HD_ATT].astype(BF16)
        qa = _dot(qh, wukt_ref[h]) * (HD_ATT ** -0.5 * LOG2E)
        for sb in range(tm // TQ):
            qabs_ref[0, sb, :, h * TQ:(h + 1) * TQ] = qa[sb * TQ:(sb + 1) * TQ, :].T.astype(BF16)
    per = TQ // IDX_DIM
    for g in range(IDX_HEADS // per):
        for sb in range(tm // TQ):
            blk = iqv[sb * TQ:(sb + 1) * TQ, g * TQ:(g + 1) * TQ].T.astype(BF16)
            for u in range(per):
                h = g * per + u
                iq_ref[0, sb, :, h * TQ:(h + 1) * TQ] = blk[u * IDX_DIM:(u + 1) * IDX_DIM, :]


def _dsaprep_call(att, q_norm, kv_norm, idx_k_norm, wuq, wiq, wukt):
    B, T, _ = att.shape
    TM = 512
    const2 = lambda b, i: (0, 0)
    return pl.pallas_call(
        _dsaprep_kernel,
        grid=(B, T // TM),
        in_specs=[pl.BlockSpec((1, TM, ATT_PAD), lambda b, i: (b, i, 0)),
                  pl.BlockSpec((1, Q_LORA), const2),
                  pl.BlockSpec((1, KV_LATENT), const2),
                  pl.BlockSpec((1, IDX_DIM), const2),
                  pl.BlockSpec((Q_LORA, ATT_WIDTH), const2),
                  pl.BlockSpec((Q_LORA, IDX_HEADS * IDX_DIM), const2),
                  pl.BlockSpec((H_ATT, HD_ATT, KV_LATENT), lambda b, i: (0, 0, 0))],
        out_specs=[pl.BlockSpec((1, TM // TQ, KV_LATENT, H_ATT * TQ), lambda b, i: (b, i, 0, 0)),
                   pl.BlockSpec((1, TM // TQ, IDX_DIM, IDX_HEADS * TQ), lambda b, i: (b, i, 0, 0)),
                   pl.BlockSpec((1, TM, IDX_DIM), lambda b, i: (b, i, 0)),
                   pl.BlockSpec((1, TM, KV_LATENT), lambda b, i: (b, i, 0)),
                   pl.BlockSpec((1, TM // TQ, KV_LATENT, TQ), lambda b, i: (b, i, 0, 0)),
                   pl.BlockSpec((1, IDX_HEADS, TM), lambda b, i: (b, 0, i))],
        out_shape=[jax.ShapeDtypeStruct((B, T // TQ, KV_LATENT, H_ATT * TQ), BF16),
                   jax.ShapeDtypeStruct((B, T // TQ, IDX_DIM, IDX_HEADS * TQ), BF16),
                   jax.ShapeDtypeStruct((B, T, IDX_DIM), BF16),
                   jax.ShapeDtypeStruct((B, T, KV_LATENT), BF16),
                   jax.ShapeDtypeStruct((B, T // TQ, KV_LATENT, TQ), BF16),
                   jax.ShapeDtypeStruct((B, IDX_HEADS, T), F32)],
        compiler_params=_cp(("parallel", "parallel")),
        name="dsaprep",
    )(att, q_norm, kv_norm, idx_k_norm, wuq, wiq, wukt)


def _t5_bucket_np(rel):
    max_exact = NUM_BUCKETS // 2
    nf = np.maximum(rel, 1).astype(np.float32)
    large = max_exact + (np.log(nf / max_exact) / math.log(MAX_DISTANCE / max_exact)
                         * (NUM_BUCKETS - max_exact)).astype(np.int32)
    large = np.minimum(large, NUM_BUCKETS - 1)
    return np.where(rel < max_exact, rel, large).astype(np.int32)


def _near_bucket_tiles():
    s = np.arange(TQ)[:, None]
    t = np.arange(TQ)[None, :]
    tiles = [_t5_bucket_np(np.maximum(t - s + TQ * d, 0)) for d in range(2)]
    assert _t5_bucket_np(np.array([TQ + 1]))[0] == NUM_BUCKETS - 1
    return np.stack(tiles)


def _bias_kernel(rb_ref, bk_ref, o_ref):
    for d in range(2):
        bk = bk_ref[d]
        for h in range(H_ATT):
            def body(b, acc):
                return jnp.where(bk == b, rb_ref[b, h], acc)
            acc = lax.fori_loop(0, NUM_BUCKETS, body, jnp.zeros((TQ, TQ), F32))
            o_ref[d, h] = (acc - rb_ref[NUM_BUCKETS - 1, h]) * LOG2E
    o_ref[2] = jnp.zeros((H_ATT, TQ, TQ), F32)


def _bias_call(rel_bias):
    buckets = jnp.asarray(_near_bucket_tiles())
    return pl.pallas_call(
        _bias_kernel,
        in_specs=[pl.BlockSpec(memory_space=pltpu.SMEM),
                  pl.BlockSpec(memory_space=pltpu.VMEM)],
        out_specs=pl.BlockSpec(memory_space=pltpu.VMEM),
        out_shape=jax.ShapeDtypeStruct((3, H_ATT, TQ, TQ), F32),
        name="bias",
    )(rel_bias, buckets)


KV = TQ // 8
FREE_PROBES = 8
COUNT_PROBES = 6


def _key_reduce(x, op):
    return op(op(x, axis=0), axis=0, keepdims=True)


def _dsa_kernel(iq_ref, iwt_ref, qabs_ref, ik_ref, ckv_ref, ckvt_ref, bias_ref, wuvt_ref, y_ref,
                score_scr, stage_a, stage_b, pv_a, pv_b, alpha_a, alpha_b, acc_scr, *, topk):
    i = pl.program_id(1)
    kf = float(topk)
    kidx = (lax.broadcasted_iota(jnp.int32, (KV, 8, TQ), 0) * 8
            + lax.broadcasted_iota(jnp.int32, (KV, 8, TQ), 1))
    qidx = lax.broadcasted_iota(jnp.int32, (KV, 8, TQ), 2)
    n_pairs = i // 2 + 1

    iwt = iwt_ref[0]
    w8 = [jnp.broadcast_to(iwt[h:h + 1, :], (8, TQ))[None] for h in range(IDX_HEADS)]
    iq2 = iq_ref[0, 0]

    n_trips = (n_pairs + 1) // 2
    stage = (stage_a, stage_b)

    def stage_pair(q, lhs_ref, rhs, dst):
        nxt = jnp.minimum(q, n_pairs - 1)
        rows = lhs_ref[0, pl.ds(pl.multiple_of(2 * nxt * TQ, 2 * TQ), 2 * TQ), :]
        dst[...] = _dot(rows, rhs).reshape(2, TQ, H_ATT * TQ)

    def two_halves(half):
        def trip(r, carry):
            carry = half(2 * r, 0, carry)
            return lax.cond(2 * r + 1 < n_pairs, lambda c: half(2 * r + 1, 1, c), lambda c: c, carry)
        return trip

    def score_half(q, par, carry):
        rmax, rmin = carry
        stage_pair(q + 1, ik_ref, iq2, stage[1 - par])
        for blk in range(2):
            j = 2 * q + blk
            sc = jnp.zeros((KV, 8, TQ), F32)
            for h in range(IDX_HEADS):
                s = stage[par][blk, :, h * TQ:(h + 1) * TQ].reshape(KV, 8, TQ)
                sc = sc + jnp.maximum(s, 0.0) * w8[h]
            causal = kidx + (j - i) * TQ <= qidx
            score_scr[j] = jnp.where(causal, sc, -jnp.inf)
            rmax = jnp.maximum(rmax, jnp.max(jnp.where(causal, sc, -jnp.inf), axis=0))
            rmin = jnp.minimum(rmin, jnp.min(jnp.where(causal, sc, jnp.inf), axis=0))
        return rmax, rmin

    stage_pair(0, ik_ref, iq2, stage[0])
    rmax, rmin = lax.fori_loop(0, n_trips, two_halves(score_half),
                               (jnp.full((8, TQ), -jnp.inf, F32), jnp.full((8, TQ), jnp.inf, F32)))
    smax = jnp.max(rmax, axis=0, keepdims=True)
    smin = jnp.min(rmin, axis=0, keepdims=True)

    nvalid = (i * TQ + lax.broadcasted_iota(jnp.int32, (1, TQ), 1) + 1).astype(F32)
    needs_thr = nvalid > kf

    def probe(v):
        vb = jnp.broadcast_to(v, (8, TQ))[None, None]

        def body(q, st):
            cnt, dn, up = st
            for j in (2 * q, 2 * q + 1):
                s = score_scr[j].reshape(4, KV // 4, 8, TQ)
                ge = s >= vb
                cnt = cnt + jnp.sum(jnp.where(ge, 1.0, 0.0), axis=1)
                dn = jnp.minimum(dn, jnp.min(jnp.where(ge, s, jnp.inf), axis=1))
                up = jnp.maximum(up, jnp.max(jnp.where(ge, -jnp.inf, s), axis=1))
            return cnt, dn, up
        cnt, dn, up = lax.fori_loop(0, n_pairs, body, (jnp.zeros((4, 8, TQ), F32),
                                                       jnp.full((4, 8, TQ), jnp.inf, F32),
                                                       jnp.full((4, 8, TQ), -jnp.inf, F32)))
        return (jnp.sum(jnp.sum(cnt, axis=0), axis=0, keepdims=True),
                jnp.min(jnp.min(dn, axis=0), axis=0, keepdims=True),
                jnp.max(jnp.max(up, axis=0), axis=0, keepdims=True))

    def probe_count(v):
        vb = jnp.broadcast_to(v, (8, TQ))[None, None]

        def body(q, cnt):
            for j in (2 * q, 2 * q + 1):
                s = score_scr[j].reshape(4, KV // 4, 8, TQ)
                cnt = cnt + jnp.sum(jnp.where(s >= vb, 1.0, 0.0), axis=1)
            return cnt
        cnt = lax.fori_loop(0, n_pairs, body, jnp.zeros((4, 8, TQ), F32))
        return jnp.sum(jnp.sum(cnt, axis=0), axis=0, keepdims=True)

    def logit(c):
        return jnp.log((c + 0.5) / (nvalid - c + 0.5))

    g_k = logit(jnp.full((1, TQ), kf - 0.5, F32))

    def search_step(st, snap=True):
        lo, hi, clo, ghi, slo, shi = st
        is_open = lo < hi
        g_lo = logit(clo)
        g_hi = logit(ghi)
        phi = jnp.clip((g_lo - g_k) / (g_lo - g_hi), 0.02, 0.98)
        stalled = (slo >= 2.0) | (shi >= 2.0)
        mid = jnp.where(stalled, 0.5 * lo + 0.5 * hi, lo + (hi - lo) * phi)
        mid = jnp.where((mid > lo) & (mid <= hi), mid, hi)
        if snap:
            c, dn, up = probe(mid)
        else:
            c = probe_count(mid)
            dn = up = mid
        ge = c >= kf
        move_lo = is_open & ge
        move_hi = is_open & jnp.logical_not(ge)
        lo = jnp.where(move_lo, dn, lo)
        clo = jnp.where(move_lo, c, clo)
        hi = jnp.where(move_hi, up, hi)
        ghi = jnp.where(move_hi, c, ghi)
        slo = jnp.where(stalled | ge, 0.0, slo + 1.0)
        shi = jnp.where(stalled | jnp.logical_not(ge), 0.0, shi + 1.0)
        return lo, hi, clo, ghi, slo, shi

    def n_open(st):
        return jnp.sum(jnp.where(st[0] < st[1], 1.0, 0.0))

    def checked_step(carry):
        st = search_step(carry[0])
        return st, n_open(st), carry[2] + 1

    search_cap = ik_ref.shape[1] + 2
    zero = jnp.zeros((1, TQ), F32)
    st = (smin, jnp.where(needs_thr, smax, smin), nvalid, zero, zero, zero)
    st = lax.fori_loop(0, COUNT_PROBES, lambda _, s: search_step(s, snap=False), st)
    st = lax.fori_loop(COUNT_PROBES, FREE_PROBES, lambda _, s: search_step(s), st)
    st, _, _ = lax.while_loop(lambda c: jnp.logical_and(c[1] > 0.0, c[2] < search_cap), checked_step,
                              (st, n_open(st), jnp.int32(0)))
    thr, _, clo, chi, _, _ = st

    tie = needs_thr & (clo > kf)
    n_tie = jnp.sum(jnp.where(tie, 1.0, 0.0))

    @pl.when(n_tie > 0.0)
    def _():
        r2 = lax.broadcasted_iota(jnp.int32, (TQ, TQ), 0)
        c2 = lax.broadcasted_iota(jnp.int32, (TQ, TQ), 1)
        lower = jnp.where(c2 <= r2, 1.0, 0.0).astype(BF16)
        thrb = jnp.broadcast_to(thr, (TQ, TQ))
        tieb = jnp.broadcast_to(tie, (TQ, TQ))
        needb = jnp.broadcast_to(kf - chi, (TQ, TQ))

        def body(j, carry):
            s = score_scr[j].reshape(TQ, TQ)
            eq = (s == thrb) & tieb
            pref = _dot(lower, jnp.where(eq, 1.0, 0.0).astype(BF16)) + carry
            score_scr[j] = jnp.where(eq & (pref > needb), -jnp.inf, s).reshape(KV, 8, TQ)
            return jnp.broadcast_to(pref[TQ - 1:TQ, :], (TQ, TQ))
        lax.fori_loop(0, i + 1, body, jnp.zeros((TQ, TQ), F32))

    thr_eff = jnp.broadcast_to(jnp.where(needs_thr, thr, LOWEST), (8, TQ))[None]

    acc_scr[...] = jnp.zeros(acc_scr.shape, F32)
    qa2 = qabs_ref[0, 0]

    pv = (pv_a, pv_b)
    alpha_st = (alpha_a, alpha_b)

    def apply_pending(par):
        acc_scr[...] = alpha_st[par][...] * acc_scr[...] + pv[par][...]

    def attend_variant(q, par, carry, near):
        m, l = list(carry[0]), list(carry[1])
        stage_pair(q + 1, ckv_ref, qa2, stage[1 - par])
        js = (2 * q, 2 * q + 1)
        sel = [score_scr[j] >= thr_eff for j in js]
        d = [jnp.clip(i - j, 0, 2) for j in js]
        kvt = jnp.concatenate([ckvt_ref[0, js[0]], ckvt_ref[0, js[1]]], axis=1)
        ps, alphas = [], []
        for h in range(H_ATT):
            lg = [stage[par][b, :, h * TQ:(h + 1) * TQ] for b in range(2)]
            if near:
                lg = [lg[b] + bias_ref[d[b], h] for b in range(2)]
            lg = [jnp.where(sel[b], lg[b].reshape(KV, 8, TQ), NEG_BIG) for b in range(2)]
            m_new = jnp.maximum(m[h], jnp.maximum(_key_reduce(lg[0], jnp.max), _key_reduce(lg[1], jnp.max)))
            alpha = jnp.exp2(m[h] - m_new)
            mb = jnp.broadcast_to(m_new, (8, TQ))[None]
            p = [jnp.exp2(lg[b] - mb) for b in range(2)]
            l[h] = alpha * l[h] + _key_reduce(p[0], jnp.sum) + _key_reduce(p[1], jnp.sum)
            m[h] = m_new
            alphas.append(alpha)
            ps.append(jnp.concatenate([p[0].reshape(TQ, TQ), p[1].reshape(TQ, TQ)], axis=0).astype(BF16))
            if h % 2 == 1:
                cols = slice((h - 1) * TQ, (h + 1) * TQ)
                pv[par][:, cols] = _dot(kvt, jnp.concatenate(ps[h - 1:h + 1], axis=1))
        alpha_st[par][...] = jnp.concatenate(alphas, axis=1)
        apply_pending(1 - par)
        return tuple(m), tuple(l)

    def attend_half(q, par, carry):
        return lax.cond(2 * q + 1 >= i - 1,
                        lambda c: attend_variant(q, par, c, True),
                        lambda c: attend_variant(q, par, c, False), carry)

    pv_b[...] = jnp.zeros(pv_b.shape, F32)
    alpha_b[...] = jnp.ones(alpha_b.shape, F32)
    stage_pair(0, ckv_ref, qa2, stage[0])
    m0 = tuple(jnp.full((1, TQ), NEG_BIG, F32) for _ in range(H_ATT))
    l0 = tuple(jnp.zeros((1, TQ), F32) for _ in range(H_ATT))
    m, l = lax.fori_loop(0, n_trips, two_halves(attend_half), (m0, l0))
    for par in range(2):
        @pl.when((n_pairs - 1) % 2 == par)
        def _():
            apply_pending(par)

    yt = []
    for h in range(H_ATT):
        o = acc_scr[:, h * TQ:(h + 1) * TQ] / l[h]
        yt.append(_dot(wuvt_ref[h], o.astype(BF16)))
    for g in range(ATT_WIDTH // TQ):
        per = TQ // HD_ATT
        y_ref[0, :, g * TQ:(g + 1) * TQ] = jnp.concatenate(yt[g * per:(g + 1) * per], axis=0).T


def _dsa_call(iq, iwt, qabs, ik, ckv, ckvt, bias_tiles, wuvt, topk):
    B, T, _ = ik.shape
    NT = T // TQ
    return pl.pallas_call(
        functools.partial(_dsa_kernel, topk=topk),
        grid=(B, NT),
        in_specs=[pl.BlockSpec((1, 1, IDX_DIM, IDX_HEADS * TQ), lambda b, i: (b, i, 0, 0)),
                  pl.BlockSpec((1, IDX_HEADS, TQ), lambda b, i: (b, 0, i)),
                  pl.BlockSpec((1, 1, KV_LATENT, H_ATT * TQ), lambda b, i: (b, i, 0, 0)),
                  pl.BlockSpec((1, T, IDX_DIM), lambda b, i: (b, 0, 0)),
                  pl.BlockSpec((1, T, KV_LATENT), lambda b, i: (b, 0, 0)),
                  pl.BlockSpec((1, NT, KV_LATENT, TQ), lambda b, i: (b, 0, 0, 0)),
                  pl.BlockSpec((3, H_ATT, TQ, TQ), lambda b, i: (0, 0, 0, 0)),
                  pl.BlockSpec((H_ATT, HD_ATT, KV_LATENT), lambda b, i: (0, 0, 0))],
        out_specs=pl.BlockSpec((1, TQ, ATT_WIDTH), lambda b, i: (b, i, 0)),
        out_shape=jax.ShapeDtypeStruct((B, T, ATT_WIDTH), F32),
        scratch_shapes=[pltpu.VMEM((NT, KV, 8, TQ), F32),
                        pltpu.VMEM((2, TQ, H_ATT * TQ), F32),
                        pltpu.VMEM((2, TQ, H_ATT * TQ), F32),
                        pltpu.VMEM((KV_LATENT, H_ATT * TQ), F32),
                        pltpu.VMEM((KV_LATENT, H_ATT * TQ), F32),
                        pltpu.VMEM((1, H_ATT * TQ), F32),
                        pltpu.VMEM((1, H_ATT * TQ), F32),
                        pltpu.VMEM((KV_LATENT, H_ATT * TQ), F32)],
        compiler_params=_cp(("parallel", "arbitrary")),
        name="dsa",
    )(iq, iwt, qabs, ik, ckv, ckvt, bias_tiles, wuvt)


def _rwkv_kernel(p_ref, prev_ref, mu_ref, w0_ref, wdec_ref, a0_ref, waaa_ref, wgate_ref,
                 kk_ref, ka_ref, rk_ref, lng_ref, lnb_ref, y_ref, h_scr):
    c = pl.program_id(1)
    C = CHUNK
    N = HD_RWKV

    @pl.when(c == 0)
    def _():
        h_scr[...] = jnp.zeros(h_scr.shape, F32)

    NS = p_ref.shape[0]
    p = p_ref[...].reshape(NS * C, RW_COLS)
    rid = lax.broadcasted_iota(jnp.int32, (NS * C, 1), 0)
    p_prev = pltpu.roll(p, 1, 0)
    for s in range(NS):
        p_prev = jnp.where(rid == s * C, jnp.where(c > 0, prev_ref[s, 7:8, :], 0.0), p_prev)
    p = p + mu_ref[...] * (p_prev - p)

    W = RWKV_WIDTH
    r = p[:, 0:W]
    k = p[:, W:2 * W]
    v = p[:, 2 * W:3 * W]
    wd = p[:, OFF_WD:OFF_WD + DECAY_LORA]
    ad = p[:, OFF_AD:OFF_AD + AAA_LORA]
    gd = p[:, OFF_GD:OFF_GD + GATE_PAD]

    w_log = -jax.nn.softplus(-(w0_ref[...] + _dot(jnp.tanh(wd).astype(BF16), wdec_ref[...]))) - 0.5
    logw = -jnp.exp(w_log)
    a = jax.nn.sigmoid(a0_ref[...] + _dot(ad.astype(BF16), waaa_ref[...]))
    g = _dot(jax.nn.sigmoid(gd).astype(BF16), wgate_ref[...])

    r64 = lax.broadcasted_iota(jnp.int32, (C, C), 0)
    c64 = lax.broadcasted_iota(jnp.int32, (C, C), 1)
    tril_incl = r64 >= c64
    tril_strict = r64 > c64
    eye = r64 == c64
    ra = lax.broadcasted_iota(jnp.int32, (NS * C, NS * C), 0)
    ca = lax.broadcasted_iota(jnp.int32, (NS * C, NS * C), 1)
    same_seq = (ra >= ca) & (ra - ca <= ra % C)
    cum = jnp.dot(jnp.where(same_seq, 1.0, 0.0), logw, preferred_element_type=F32,
                  precision=lax.Precision.HIGHEST)
    cum_last = cum[C - 1:C, :]
    for s in range(1, NS):
        cum_last = jnp.where(rid >= s * C, cum[(s + 1) * C - 1:(s + 1) * C, :], cum_last)
    g_in = jnp.exp(cum - logw)
    g_t = jnp.exp(cum)
    g_inv = jnp.exp(-cum)
    g_end = jnp.exp(cum_last - cum)
    g_all = jnp.exp(cum_last)

    kkf = k * kk_ref[...]
    kmod = k * (1.0 + (a - 1.0) * ka_ref[...])
    rkr = r * kmod * rk_ref[...]

    P = range(NS * H_RWKV // 2)
    rows = [slice((p // (H_RWKV // 2)) * C, (p // (H_RWKV // 2) + 1) * C) for p in P]
    sls = [slice((p % (H_RWKV // 2)) * 2 * N, (p % (H_RWKV // 2) + 1) * 2 * N) for p in P]
    lane2 = lax.broadcasted_iota(jnp.int32, (C, 2 * N), 1)
    row2 = lax.broadcasted_iota(jnp.int32, (C, 2 * N), 0)
    left = lane2 < N
    col2 = jnp.where(left, lane2, lane2 - N)
    strict2 = row2 > col2
    incl2 = row2 >= col2
    eye2 = row2 == col2

    def per_head(x, op):
        return jnp.where(left, op(jnp.where(left, x, 0.0), axis=-1, keepdims=True),
                         op(jnp.where(left, 0.0, x), axis=-1, keepdims=True))

    def block_diag(x):
        zero = jnp.zeros_like(x)
        return jnp.concatenate([jnp.where(left, x, zero), jnp.where(left, zero, x)], axis=0)

    at, rt, bh, kh, vb, g4 = [], [], [], [], [], []
    for p in P:
        rs, sl = rows[p], sls[p]
        kk_p = kkf[rs, sl]
        kk_p = kk_p / jnp.maximum(jnp.sqrt(per_head(kk_p * kk_p, jnp.sum)), 1e-12)
        k_p = kmod[rs, sl]
        bvec = kk_p * a[rs, sl]
        at_p = ((-kk_p) * g_in[rs, sl]).astype(BF16)
        rt_p = (r[rs, sl] * g_t[rs, sl]).astype(BF16)
        bt_p = (bvec * g_inv[rs, sl]).astype(BF16)
        kt_p = (k_p * g_inv[rs, sl]).astype(BF16)
        wt = jnp.concatenate([block_diag(bt_p), block_diag(kt_p)], axis=0)
        g4.append(_dot_nt(jnp.concatenate([at_p, rt_p], axis=0), wt))
        at.append(at_p)
        rt.append(rt_p)
        bh.append((bvec * g_end[rs, sl]).astype(BF16))
        kh.append((k_p * g_end[rs, sl]).astype(BF16))
        vb.append(v[rs, sl].astype(BF16))

    a_ab = [jnp.where(strict2, g4[p][0:C, 0:2 * C], 0.0) for p in P]
    a_ak = [jnp.where(strict2, g4[p][0:C, 2 * C:4 * C], 0.0).astype(BF16) for p in P]
    a_rb = [jnp.where(incl2, g4[p][C:2 * C, 0:2 * C], 0.0).astype(BF16) for p in P]
    a_rk = [jnp.where(incl2, g4[p][C:2 * C, 2 * C:4 * C], 0.0).astype(BF16) for p in P]

    pw = a_ab
    tinv = [jnp.where(eye2, 1.0, 0.0) for p in P]
    for level in range(6):
        pwb = [pw[p].astype(BF16) for p in P]
        bd = [block_diag(pwb[p]) for p in P]
        if level < 5:
            res = [_dot(jnp.concatenate([pwb[p], tinv[p].astype(BF16)], axis=0), bd[p]) for p in P]
            pw = [res[p][0:C] for p in P]
            tinv = [tinv[p] + res[p][C:2 * C] for p in P]
        else:
            tinv = [tinv[p] + _dot(tinv[p].astype(BF16), bd[p]) for p in P]

    hst = [h_scr[p] for p in P]
    hbd = [block_diag(hst[p].astype(BF16)) for p in P]
    vbd = [block_diag(vb[p]) for p in P]
    x = [_dot(jnp.concatenate([at[p], a_ak[p]], axis=1), jnp.concatenate([hbd[p], vbd[p]], axis=0)) for p in P]
    ub = [_dot(tinv[p].astype(BF16), block_diag(x[p].astype(BF16))).astype(BF16) for p in P]
    y = [_dot(jnp.concatenate([rt[p], a_rb[p], a_rk[p]], axis=1),
              jnp.concatenate([hbd[p], block_diag(ub[p]), vbd[p]], axis=0)) for p in P]
    for p in P:
        ga = g_all[rows[p], sls[p]]
        g_col = jnp.where(left, jnp.sum(jnp.where(row2 == lane2, ga, 0.0), axis=1, keepdims=True),
                          jnp.sum(jnp.where(row2 == lane2 - N, ga, 0.0), axis=1, keepdims=True))
        res = _dot_tn(jnp.concatenate([bh[p], kh[p]], axis=0), jnp.concatenate([ub[p], vb[p]], axis=0))
        h_scr[p] = g_col * hst[p] + jnp.where(left, res[0:N], res[N:2 * N])

    for p in P:
        rs, sl = rows[p], sls[p]
        mean = per_head(y[p], jnp.sum) * (1.0 / N)
        yc = y[p] - mean
        var = per_head(yc * yc, jnp.sum) * (1.0 / N)
        yn = yc * lax.rsqrt(var + GN_EPS) * lng_ref[:, sl] + lnb_ref[:, sl]
        bonus = per_head(rkr[rs, sl], jnp.sum) * v[rs, sl]
        y_ref[p // (H_RWKV // 2), :, sl] = (yn + bonus) * g[rs, sl]


def _rwkv_call(rw, mu_p, w0, wdec, a0, waaa, wgate_p, k_k, k_a, r_k, lng, lnb):
    B, T, _ = rw.shape
    NC = T // CHUNK
    W = RWKV_WIDTH
    c2 = lambda b, c: (0, 0)
    rowspec = pl.BlockSpec((1, W), c2)
    NS = 2 if B % 2 == 0 else 1
    return pl.pallas_call(
        _rwkv_kernel,
        grid=(B // NS, NC),
        in_specs=[pl.BlockSpec((NS, CHUNK, RW_COLS), lambda b, c: (b, c, 0)),
                  pl.BlockSpec((NS, 8, RW_COLS), lambda b, c: (b, jnp.maximum(c * (CHUNK // 8) - 1, 0), 0)),
                  pl.BlockSpec((1, RW_COLS), c2),
                  rowspec,
                  pl.BlockSpec((DECAY_LORA, W), c2),
                  rowspec,
                  pl.BlockSpec((AAA_LORA, W), c2),
                  pl.BlockSpec((GATE_PAD, W), c2),
                  rowspec, rowspec, rowspec, rowspec, rowspec],
        out_specs=pl.BlockSpec((NS, CHUNK, W), lambda b, c: (b, c, 0)),
        out_shape=jax.ShapeDtypeStruct((B, T, W), F32),
        scratch_shapes=[pltpu.VMEM((NS * H_RWKV // 2, HD_RWKV, 2 * HD_RWKV), F32)],
        compiler_params=_cp(("parallel", "arbitrary")),
        name="rwkv",
    )(rw, rw, mu_p, w0, wdec, a0, waaa, wgate_p, k_k, k_a, r_k, lng, lnb)


def _ffn_kernel(x_ref, ya_ref, yr_ref, mod_ref, gpost_ref, gpre_ref, gfpost_ref,
                wo_ref, wg_ref, wu_ref, wd_ref, o_ref, x1_scr, hf_scr, acc_scr):
    j = pl.program_id(2)

    @pl.when(j == 0)
    def _():
        mix = (_dot(ya_ref[0].astype(BF16), wo_ref[0:ATT_WIDTH, :])
               + _dot(yr_ref[0].astype(BF16), wo_ref[ATT_WIDTH:, :]))
        x1 = x_ref[0] + mod_ref[0, 2:3, :] * _rms(mix, gpost_ref[...])
        x1_scr[...] = x1
        hf = _rms(x1, gpre_ref[...]) * (1.0 + mod_ref[0, 4:5, :]) + mod_ref[0, 3:4, :]
        hf_scr[...] = hf.astype(BF16)
        acc_scr[...] = jnp.zeros(acc_scr.shape, F32)

    hf = hf_scr[...]
    gate = _dot(hf, wg_ref[...])
    up = _dot(hf, wu_ref[...])
    act = (gate * jax.nn.sigmoid(gate) * up).astype(BF16)
    acc_scr[...] += _dot(act, wd_ref[...])

    @pl.when(j == pl.num_programs(2) - 1)
    def _():
        o_ref[0] = x1_scr[...] + mod_ref[0, 5:6, :] * _rms(acc_scr[...], gfpost_ref[...])


def _ffn_call(x, y_att, y_rwkv, mod3, g_post, g_pre, g_fpost, wo, wg, wu, wd):
    B, T, D = x.shape
    TM = 512
    TF = D_FF // 2
    c3 = lambda b, i, j: (0, 0)
    return pl.pallas_call(
        _ffn_kernel,
        grid=(B, T // TM, D_FF // TF),
        in_specs=[pl.BlockSpec((1, TM, D), lambda b, i, j: (b, i, 0)),
                  pl.BlockSpec((1, TM, ATT_WIDTH), lambda b, i, j: (b, i, 0)),
                  pl.BlockSpec((1, TM, RWKV_WIDTH), lambda b, i, j: (b, i, 0)),
                  pl.BlockSpec((1, 6, D), lambda b, i, j: (b, 0, 0)),
                  pl.BlockSpec((1, D), c3), pl.BlockSpec((1, D), c3), pl.BlockSpec((1, D), c3),
                  pl.BlockSpec((D, D), c3),
                  pl.BlockSpec((D, TF), lambda b, i, j: (0, j)),
                  pl.BlockSpec((D, TF), lambda b, i, j: (0, j)),
                  pl.BlockSpec((TF, D), lambda b, i, j: (j, 0))],
        out_specs=pl.BlockSpec((1, TM, D), lambda b, i, j: (b, i, 0)),
        out_shape=jax.ShapeDtypeStruct((B, T, D), F32),
        scratch_shapes=[pltpu.VMEM((TM, D), F32), pltpu.VMEM((TM, D), BF16), pltpu.VMEM((TM, D), F32)],
        compiler_params=_cp(("parallel", "parallel", "arbitrary")),
        name="ffn",
    )(x, y_att, y_rwkv, mod3, g_post, g_pre, g_fpost, wo, wg, wu, wd)


def _pad_cols(w, n):
    return jnp.pad(w, ((0, 0), (0, n - w.shape[1])))


def _layer(x, mod3, rel_bias, mix_pre_norm, mix_post_norm, ffn_pre_norm, ffn_post_norm,
           w_in, q_norm, w_uq, w_idx_q, kv_norm, idx_k_norm, w_uk, w_uv,
           mu_shift, w0, w_decay_up, a0, w_aaa_up, w_gate_up, k_k, k_a, r_k, ln_x_gain, ln_x_bias,
           w_out, w_ffn_gate, w_ffn_up, w_ffn_down):
    B, T, D = x.shape
    att_cols = Q_LORA + KV_LATENT + IDX_DIM + IDX_HEADS
    n_main = 3 * RWKV_WIDTH + DECAY_LORA + AAA_LORA
    w_att = _pad_cols(w_in[:, :att_cols], ATT_PAD)
    w_rw = w_in[:, att_cols:]
    w_in_p = jnp.concatenate([w_att, w_rw[:, :n_main], _pad_cols(w_rw[:, n_main:], GATE_PAD)], axis=1).astype(BF16)
    mu_p = jnp.concatenate([mu_shift[:n_main], jnp.pad(mu_shift[n_main:], (0, GATE_PAD - GATE_LORA))]).reshape(1, RW_COLS)
    wgate_p = jnp.pad(w_gate_up, ((0, GATE_PAD - GATE_LORA), (0, 0))).astype(BF16)

    att, rw = _inproj_call(x, mod3, mix_pre_norm.reshape(1, D), w_in_p)

    wuq = w_uq.reshape(Q_LORA, ATT_WIDTH).astype(BF16)
    wiq = w_idx_q.reshape(Q_LORA, IDX_HEADS * IDX_DIM).astype(BF16)
    wukt = jnp.transpose(w_uk, (0, 2, 1)).astype(BF16)
    qabs, iq, ik, ckv, ckvt, iwt = _dsaprep_call(att, q_norm.reshape(1, -1), kv_norm.reshape(1, -1),
                                                 idx_k_norm.reshape(1, -1), wuq, wiq, wukt)
    bias_tiles = _bias_call(rel_bias)
    wuvt = jnp.transpose(w_uv, (0, 2, 1)).astype(BF16)
    topk = min(TOPK_MAX, T // 4)
    y_att = _dsa_call(iq, iwt, qabs, ik, ckv, ckvt, bias_tiles, wuvt, topk)

    row = lambda z: z.reshape(1, RWKV_WIDTH)
    y_rwkv = _rwkv_call(rw, mu_p, row(w0), w_decay_up.astype(BF16), row(a0), w_aaa_up.astype(BF16),
                        wgate_p, row(k_k), row(k_a), row(r_k), row(ln_x_gain), row(ln_x_bias))

    return _ffn_call(x, y_att, y_rwkv, mod3, mix_post_norm.reshape(1, D), ffn_pre_norm.reshape(1, D),
                     ffn_post_norm.reshape(1, D), w_out.astype(BF16), w_ffn_gate.astype(BF16),
                     w_ffn_up.astype(BF16), w_ffn_down.astype(BF16))


def kernel(x, c, rel_bias, ada_w, ada_b, mix_pre_norm, mix_post_norm, ffn_pre_norm, ffn_post_norm, w_in, q_norm, w_uq, w_idx_q, kv_norm, idx_k_norm, w_uk, w_uv, mu_shift, w0, w_decay_up, a0, w_aaa_up, w_gate_up, k_k, k_a, r_k, ln_x_gain, ln_x_bias, w_out, w_ffn_gate, w_ffn_up, w_ffn_down):
    B, T, D = x.shape
    assert D == D_MODEL and T % 512 == 0 and T // 4 >= 1
    layer_params = (mix_pre_norm, mix_post_norm, ffn_pre_norm, ffn_post_norm,
                    w_in, q_norm, w_uq, w_idx_q, kv_norm, idx_k_norm, w_uk, w_uv,
                    mu_shift, w0, w_decay_up, a0, w_aaa_up, w_gate_up, k_k, k_a, r_k, ln_x_gain, ln_x_bias,
                    w_out, w_ffn_gate, w_ffn_up, w_ffn_down)
    for l in range(ada_w.shape[0]):
        mod3 = _mod_call(c, ada_w[l], ada_b[l]).reshape(B, 6, D)
        x = _layer(x, mod3, rel_bias, *[p[l] for p in layer_params])
    return x
```

```python
import functools
import math

import numpy as np
import jax
import jax.numpy as jnp
from jax import lax
from jax.experimental import pallas as pl
from jax.experimental.pallas import tpu as pltpu

F32 = jnp.float32
BF16 = jnp.bfloat16

D_MODEL = 1024
ATT_WIDTH = 512
RWKV_WIDTH = 512
HD_ATT = 64
H_ATT = 8
HD_RWKV = 64
H_RWKV = 8
Q_LORA = 256
KV_LATENT = 128
IDX_HEADS = 8
IDX_DIM = 64
TOPK_MAX = 256
NUM_BUCKETS = 32
MAX_DISTANCE = 128
DECAY_LORA = 64
AAA_LORA = 64
GATE_LORA = 160
GN_EPS = 64e-5
D_FF = 2816
NORM_EPS = 1e-6

ATT_PAD = 512
GATE_PAD = 256
RW_COLS = 3 * RWKV_WIDTH + DECAY_LORA + AAA_LORA + GATE_PAD
OFF_WD = 3 * RWKV_WIDTH
OFF_AD = OFF_WD + DECAY_LORA
OFF_GD = OFF_AD + AAA_LORA

TQ = 128
CHUNK = 64
NEG_BIG = -1e30
LOWEST = -3.0e38
LOG2E = math.log2(math.e)
VMEM_LIMIT = 56 * 1024 * 1024


def _cp(sem):
    return pltpu.CompilerParams(dimension_semantics=sem, vmem_limit_bytes=VMEM_LIMIT)


def _dot(a, b):
    return jnp.dot(a, b, preferred_element_type=F32)


def _dot_nt(a, b):
    return lax.dot_general(a, b, (((1,), (1,)), ((), ())), preferred_element_type=F32)


def _dot_tn(a, b):
    return lax.dot_general(a, b, (((0,), (0,)), ((), ())), preferred_element_type=F32)


def _rms(z, gain):
    return z * lax.rsqrt(jnp.mean(z * z, axis=-1, keepdims=True) + NORM_EPS) * gain


def _row_chunks(n, chunk=256):
    return [slice(r, r + chunk) for r in range(0, n, chunk)]


def _mod_kernel(c_ref, w_ref, b_ref, o_ref):
    c = c_ref[...]
    s = c * jax.nn.sigmoid(c)
    o_ref[...] = _dot(s.astype(BF16), w_ref[...].astype(BF16)) + b_ref[...]


def _mod_call(c, ada_w, ada_b):
    B, D = c.shape
    N = ada_w.shape[1]
    TN = 1536
    return pl.pallas_call(
        _mod_kernel,
        grid=(N // TN,),
        in_specs=[pl.BlockSpec((B, D), lambda j: (0, 0)),
                  pl.BlockSpec((D, TN), lambda j: (0, j)),
                  pl.BlockSpec((1, TN), lambda j: (0, j))],
        out_specs=pl.BlockSpec((B, TN), lambda j: (0, j)),
        out_shape=jax.ShapeDtypeStruct((B, N), F32),
        compiler_params=_cp(("parallel",)),
        name="mod",
    )(c, ada_w, ada_b.reshape(1, N))


def _inproj_kernel(x_ref, mod_ref, g_ref, w_ref, att_ref, rw_ref):
    sh = mod_ref[0, 0:1, :]
    sc = mod_ref[0, 1:2, :]
    for rows in _row_chunks(x_ref.shape[1]):
        h = (_rms(x_ref[0, rows, :], g_ref[...]) * (1.0 + sc) + sh).astype(BF16)
        att_ref[0, rows, :] = _dot(h, w_ref[:, 0:ATT_PAD])
        for n0 in range(0, RW_COLS, 384):
            rw_ref[0, rows, n0:n0 + 384] = _dot(h, w_ref[:, ATT_PAD + n0:ATT_PAD + n0 + 384])


def _inproj_call(x, mod3, gain, w_in_p):
    B, T, D = x.shape
    TM = 512
    NP = w_in_p.shape[1]
    return pl.pallas_call(
        _inproj_kernel,
        grid=(B, T // TM),
        in_specs=[pl.BlockSpec((1, TM, D), lambda b, i: (b, i, 0)),
                  pl.BlockSpec((1, 6, D), lambda b, i: (b, 0, 0)),
                  pl.BlockSpec((1, D), lambda b, i: (0, 0)),
                  pl.BlockSpec((D, NP), lambda b, i: (0, 0))],
        out_specs=[pl.BlockSpec((1, TM, ATT_PAD), lambda b, i: (b, i, 0)),
                   pl.BlockSpec((1, TM, RW_COLS), lambda b, i: (b, i, 0))],
        out_shape=[jax.ShapeDtypeStruct((B, T, ATT_PAD), F32),
                   jax.ShapeDtypeStruct((B, T, RW_COLS), F32)],
        compiler_params=_cp(("parallel", "parallel")),
        name="inproj",
    )(x, mod3, gain, w_in_p)


def _dsaprep_kernel(att_ref, qn_ref, kvn_ref, ikn_ref, wuq_ref, wiq_ref, wukt_ref,
                    qabs_ref, iq_ref, ik_ref, ckv_ref, ckvt_ref, iwt_ref):
    att = att_ref[0]
    tm = att.shape[0]
    o1 = Q_LORA
    o2 = o1 + KV_LATENT
    o3 = o2 + IDX_DIM
    cq = _rms(att[:, 0:o1], qn_ref[...]).astype(BF16)
    ckv = _rms(att[:, o1:o2], kvn_ref[...])
    ckv_ref[0] = ckv.astype(BF16)
    ik_ref[0] = _rms(att[:, o2:o3], ikn_ref[...]).astype(BF16)
    tail = att[:, o2:ATT_PAD]
    w_off = o3 - o2
    for sb in range(tm // TQ):
        rows = slice(sb * TQ, (sb + 1) * TQ)
        ckvt_ref[0, sb] = ckv[rows, :].T.astype(BF16)
        iwt_ref[0, :, rows] = tail[rows, :].T[w_off:w_off + IDX_HEADS, :] * (IDX_HEADS ** -0.5 * IDX_DIM ** -0.5)
    q = _dot(cq, wuq_ref[...])
    iqv = _dot(cq, wiq_ref[...])
    for h in range(H_ATT):
        qh = q[:, h * HD_ATT:(h + 1) * HD_ATT].astype(BF16)
        qa = _dot(qh, wukt_ref[h]) * (HD_ATT ** -0.5 * LOG2E)
        for sb in range(tm // TQ):
            qabs_ref[0, sb, :, h * TQ:(h + 1) * TQ] = qa[sb * TQ:(sb + 1) * TQ, :].T.astype(BF16)
    per = TQ // IDX_DIM
    for g in range(IDX_HEADS // per):
        for sb in range(tm // TQ):
            blk = iqv[sb * TQ:(sb + 1) * TQ, g * TQ:(g + 1) * TQ].T.astype(BF16)
            for u in range(per):
                h = g * per + u
                iq_ref[0, sb, :, h * TQ:(h + 1) * TQ] = blk[u * IDX_DIM:(u + 1) * IDX_DIM, :]


def _dsaprep_call(att, q_norm, kv_norm, idx_k_norm, wuq, wiq, wukt):
    B, T, _ = att.shape
    TM = 512
    const2 = lambda b, i: (0, 0)
    return pl.pallas_call(
        _dsaprep_kernel,
        grid=(B, T // TM),
        in_specs=[pl.BlockSpec((1, TM, ATT_PAD), lambda b, i: (b, i, 0)),
                  pl.BlockSpec((1, Q_LORA), const2),
                  pl.BlockSpec((1, KV_LATENT), const2),
                  pl.BlockSpec((1, IDX_DIM), const2),
                  pl.BlockSpec((Q_LORA, ATT_WIDTH), const2),
                  pl.BlockSpec((Q_LORA, IDX_HEADS * IDX_DIM), const2),
                  pl.BlockSpec((H_ATT, HD_ATT, KV_LATENT), lambda b, i: (0, 0, 0))],
        out_specs=[pl.BlockSpec((1, TM // TQ, KV_LATENT, H_ATT * TQ), lambda b, i: (b, i, 0, 0)),
                   pl.BlockSpec((1, TM // TQ, IDX_DIM, IDX_HEADS * TQ), lambda b, i: (b, i, 0, 0)),
                   pl.BlockSpec((1, TM, IDX_DIM), lambda b, i: (b, i, 0)),
                   pl.BlockSpec((1, TM, KV_LATENT), lambda b, i: (b, i, 0)),
                   pl.BlockSpec((1, TM // TQ, KV_LATENT, TQ), lambda b, i: (b, i, 0, 0)),
                   pl.BlockSpec((1, IDX_HEADS, TM), lambda b, i: (b, 0, i))],
        out_shape=[jax.ShapeDtypeStruct((B, T // TQ, KV_LATENT, H_ATT * TQ), BF16),
                   jax.ShapeDtypeStruct((B, T // TQ, IDX_DIM, IDX_HEADS * TQ), BF16),
                   jax.ShapeDtypeStruct((B, T, IDX_DIM), BF16),
                   jax.ShapeDtypeStruct((B, T, KV_LATENT), BF16),
                   jax.ShapeDtypeStruct((B, T // TQ, KV_LATENT, TQ), BF16),
                   jax.ShapeDtypeStruct((B, IDX_HEADS, T), F32)],
        compiler_params=_cp(("parallel", "parallel")),
        name="dsaprep",
    )(att, q_norm, kv_norm, idx_k_norm, wuq, wiq, wukt)


def _t5_bucket_np(rel):
    max_exact = NUM_BUCKETS // 2
    nf = np.maximum(rel, 1).astype(np.float32)
    large = max_exact + (np.log(nf / max_exact) / math.log(MAX_DISTANCE / max_exact)
                         * (NUM_BUCKETS - max_exact)).astype(np.int32)
    large = np.minimum(large, NUM_BUCKETS - 1)
    return np.where(rel < max_exact, rel, large).astype(np.int32)


def _near_bucket_tiles():
    s = np.arange(TQ)[:, None]
    t = np.arange(TQ)[None, :]
    tiles = [_t5_bucket_np(np.maximum(t - s + TQ * d, 0)) for d in range(2)]
    assert _t5_bucket_np(np.array([TQ + 1]))[0] == NUM_BUCKETS - 1
    return np.stack(tiles)


def _bias_kernel(rb_ref, bk_ref, o_ref):
    for d in range(2):
        bk = bk_ref[d]
        for h in range(H_ATT):
            def body(b, acc):
                return jnp.where(bk == b, rb_ref[b, h], acc)
            acc = lax.fori_loop(0, NUM_BUCKETS, body, jnp.zeros((TQ, TQ), F32))
            o_ref[d, h] = (acc - rb_ref[NUM_BUCKETS - 1, h]) * LOG2E
    o_ref[2] = jnp.zeros((H_ATT, TQ, TQ), F32)


def _bias_call(rel_bias):
    buckets = jnp.asarray(_near_bucket_tiles())
    return pl.pallas_call(
        _bias_kernel,
        in_specs=[pl.BlockSpec(memory_space=pltpu.SMEM),
                  pl.BlockSpec(memory_space=pltpu.VMEM)],
        out_specs=pl.BlockSpec(memory_space=pltpu.VMEM),
        out_shape=jax.ShapeDtypeStruct((3, H_ATT, TQ, TQ), F32),
        name="bias",
    )(rel_bias, buckets)


KV = TQ // 8
FREE_PROBES = 10
COUNT_PROBES = 6


def _key_reduce(x, op):
    return op(op(x, axis=0), axis=0, keepdims=True)


def _dsa_kernel(iq_ref, iwt_ref, qabs_ref, ik_ref, ckv_ref, ckvt_ref, bias_ref, wuvt_ref, y_ref,
                score_scr, stage_a, stage_b, pv_a, pv_b, alpha_a, alpha_b, acc_scr, *, topk):
    i = pl.program_id(1)
    kf = float(topk)
    kidx = (lax.broadcasted_iota(jnp.int32, (KV, 8, TQ), 0) * 8
            + lax.broadcasted_iota(jnp.int32, (KV, 8, TQ), 1))
    qidx = lax.broadcasted_iota(jnp.int32, (KV, 8, TQ), 2)
    n_pairs = i // 2 + 1

    iwt = iwt_ref[0]
    w8 = [jnp.broadcast_to(iwt[h:h + 1, :], (8, TQ))[None] for h in range(IDX_HEADS)]
    iq2 = iq_ref[0, 0]

    n_trips = (n_pairs + 1) // 2
    stage = (stage_a, stage_b)

    def stage_pair(q, lhs_ref, rhs, dst):
        nxt = jnp.minimum(q, n_pairs - 1)
        rows = lhs_ref[0, pl.ds(pl.multiple_of(2 * nxt * TQ, 2 * TQ), 2 * TQ), :]
        dst[...] = _dot(rows, rhs).reshape(2, TQ, H_ATT * TQ)

    def two_halves(half):
        def trip(r, carry):
            carry = half(2 * r, 0, carry)
            return lax.cond(2 * r + 1 < n_pairs, lambda c: half(2 * r + 1, 1, c), lambda c: c, carry)
        return trip

    def score_half(q, par, carry):
        rmax, rmin = carry
        stage_pair(q + 1, ik_ref, iq2, stage[1 - par])
        for blk in range(2):
            j = 2 * q + blk
            sc = jnp.zeros((KV, 8, TQ), F32)
            for h in range(IDX_HEADS):
                s = stage[par][blk, :, h * TQ:(h + 1) * TQ].reshape(KV, 8, TQ)
                sc = sc + jnp.maximum(s, 0.0) * w8[h]
            causal = kidx + (j - i) * TQ <= qidx
            score_scr[j] = jnp.where(causal, sc, -jnp.inf)
            rmax = jnp.maximum(rmax, jnp.max(jnp.where(causal, sc, -jnp.inf), axis=0))
            rmin = jnp.minimum(rmin, jnp.min(jnp.where(causal, sc, jnp.inf), axis=0))
        return rmax, rmin

    stage_pair(0, ik_ref, iq2, stage[0])
    rmax, rmin = lax.fori_loop(0, n_trips, two_halves(score_half),
                               (jnp.full((8, TQ), -jnp.inf, F32), jnp.full((8, TQ), jnp.inf, F32)))
    smax = jnp.max(rmax, axis=0, keepdims=True)
    smin = jnp.min(rmin, axis=0, keepdims=True)

    nvalid = (i * TQ + lax.broadcasted_iota(jnp.int32, (1, TQ), 1) + 1).astype(F32)
    needs_thr = nvalid > kf

    def probe(v):
        vb = jnp.broadcast_to(v, (8, TQ))[None, None]

        def body(q, st):
            cnt, dn, up = st
            for j in (2 * q, 2 * q + 1):
                s = score_scr[j].reshape(4, KV // 4, 8, TQ)
                ge = s >= vb
                cnt = cnt + jnp.sum(jnp.where(ge, 1.0, 0.0), axis=1)
                dn = jnp.minimum(dn, jnp.min(jnp.where(ge, s, jnp.inf), axis=1))
                up = jnp.maximum(up, jnp.max(jnp.where(ge, -jnp.inf, s), axis=1))
            return cnt, dn, up
        cnt, dn, up = lax.fori_loop(0, n_pairs, body, (jnp.zeros((4, 8, TQ), F32),
                                                       jnp.full((4, 8, TQ), jnp.inf, F32),
                                                       jnp.full((4, 8, TQ), -jnp.inf, F32)))
        return (jnp.sum(jnp.sum(cnt, axis=0), axis=0, keepdims=True),
                jnp.min(jnp.min(dn, axis=0), axis=0, keepdims=True),
                jnp.max(jnp.max(up, axis=0), axis=0, keepdims=True))

    def probe_count(v):
        vb = jnp.broadcast_to(v, (8, TQ))[None, None]

        def body(q, cnt):
            for j in (2 * q, 2 * q + 1):
                s = score_scr[j].reshape(4, KV // 4, 8, TQ)
                cnt = cnt + jnp.sum(jnp.where(s >= vb, 1.0, 0.0), axis=1)
            return cnt
        cnt = lax.fori_loop(0, n_pairs, body, jnp.zeros((4, 8, TQ), F32))
        return jnp.sum(jnp.sum(cnt, axis=0), axis=0, keepdims=True)

    def logit(c):
        return jnp.log((c + 0.5) / (nvalid - c + 0.5))

    g_k = logit(jnp.full((1, TQ), kf - 0.5, F32))

    def search_step(st, snap=True):
        lo, hi, clo, ghi, slo, shi = st
        is_open = lo < hi
        g_lo = logit(clo)
        g_hi = logit(ghi)
        phi = jnp.clip((g_lo - g_k) / (g_lo - g_hi), 0.02, 0.98)
        stalled = (slo >= 2.0) | (shi >= 2.0)
        mid = jnp.where(stalled, 0.5 * lo + 0.5 * hi, lo + (hi - lo) * phi)
        mid = jnp.where((mid > lo) & (mid <= hi), mid, hi)
        if snap:
            c, dn, up = probe(mid)
        else:
            c = probe_count(mid)
            dn = up = mid
        ge = c >= kf
        move_lo = is_open & ge
        move_hi = is_open & jnp.logical_not(ge)
        lo = jnp.where(move_lo, dn, lo)
        clo = jnp.where(move_lo, c, clo)
        hi = jnp.where(move_hi, up, hi)
        ghi = jnp.where(move_hi, c, ghi)
        slo = jnp.where(stalled | ge, 0.0, slo + 1.0)
        shi = jnp.where(stalled | jnp.logical_not(ge), 0.0, shi + 1.0)
        return lo, hi, clo, ghi, slo, shi

    def n_open(st):
        return jnp.sum(jnp.where(st[0] < st[1], 1.0, 0.0))

    def checked_step(carry):
        st = search_step(carry[0])
        return st, n_open(st), carry[2] + 1

    search_cap = ik_ref.shape[1] + 2
    zero = jnp.zeros((1, TQ), F32)
    st = (smin, jnp.where(needs_thr, smax, smin), nvalid, zero, zero, zero)
    st = lax.fori_loop(0, COUNT_PROBES, lambda _, s: search_step(s, snap=False), st)
    st = lax.fori_loop(COUNT_PROBES, FREE_PROBES, lambda _, s: search_step(s), st)
    st, _, _ = lax.while_loop(lambda c: jnp.logical_and(c[1] > 0.0, c[2] < search_cap), checked_step,
                              (st, n_open(st), jnp.int32(0)))
    thr, _, clo, chi, _, _ = st

    tie = needs_thr & (clo > kf)
    n_tie = jnp.sum(jnp.where(tie, 1.0, 0.0))

    @pl.when(n_tie > 0.0)
    def _():
        r2 = lax.broadcasted_iota(jnp.int32, (TQ, TQ), 0)
        c2 = lax.broadcasted_iota(jnp.int32, (TQ, TQ), 1)
        lower = jnp.where(c2 <= r2, 1.0, 0.0).astype(BF16)
        thrb = jnp.broadcast_to(thr, (TQ, TQ))
        tieb = jnp.broadcast_to(tie, (TQ, TQ))
        needb = jnp.broadcast_to(kf - chi, (TQ, TQ))

        def body(j, carry):
            s = score_scr[j].reshape(TQ, TQ)
            eq = (s == thrb) & tieb
            pref = _dot(lower, jnp.where(eq, 1.0, 0.0).astype(BF16)) + carry
            score_scr[j] = jnp.where(eq & (pref > needb), -jnp.inf, s).reshape(KV, 8, TQ)
            return jnp.broadcast_to(pref[TQ - 1:TQ, :], (TQ, TQ))
        lax.fori_loop(0, i + 1, body, jnp.zeros((TQ, TQ), F32))

    thr_eff = jnp.broadcast_to(jnp.where(needs_thr, thr, LOWEST), (8, TQ))[None]

    acc_scr[...] = jnp.zeros(acc_scr.shape, F32)
    qa2 = qabs_ref[0, 0]

    pv = (pv_a, pv_b)
    alpha_st = (alpha_a, alpha_b)

    def apply_pending(par):
        acc_scr[...] = alpha_st[par][...] * acc_scr[...] + pv[par][...]

    def attend_variant(q, par, carry, near):
        m, l = list(carry[0]), list(carry[1])
        stage_pair(q + 1, ckv_ref, qa2, stage[1 - par])
        js = (2 * q, 2 * q + 1)
        sel = [score_scr[j] >= thr_eff for j in js]
        d = [jnp.clip(i - j, 0, 2) for j in js]
        kvt = jnp.concatenate([ckvt_ref[0, js[0]], ckvt_ref[0, js[1]]], axis=1)
        ps, alphas = [], []
        for h in range(H_ATT):
            lg = [stage[par][b, :, h * TQ:(h + 1) * TQ] for b in range(2)]
            if near:
                lg = [lg[b] + bias_ref[d[b], h] for b in range(2)]
            lg = [jnp.where(sel[b], lg[b].reshape(KV, 8, TQ), NEG_BIG) for b in range(2)]
            m_new = jnp.maximum(m[h], jnp.maximum(_key_reduce(lg[0], jnp.max), _key_reduce(lg[1], jnp.max)))
            alpha = jnp.exp2(m[h] - m_new)
            mb = jnp.broadcast_to(m_new, (8, TQ))[None]
            p = [jnp.exp2(lg[b] - mb) for b in range(2)]
            l[h] = alpha * l[h] + _key_reduce(p[0], jnp.sum) + _key_reduce(p[1], jnp.sum)
            m[h] = m_new
            alphas.append(alpha)
            ps.append(jnp.concatenate([p[0].reshape(TQ, TQ), p[1].reshape(TQ, TQ)], axis=0).astype(BF16))
            if h % 2 == 1:
                cols = slice((h - 1) * TQ, (h + 1) * TQ)
                pv[par][:, cols] = _dot(kvt, jnp.concatenate(ps[h - 1:h + 1], axis=1))
        alpha_st[par][...] = jnp.concatenate(alphas, axis=1)
        apply_pending(1 - par)
        return tuple(m), tuple(l)

    def attend_half(q, par, carry):
        return lax.cond(2 * q + 1 >= i - 1,
                        lambda c: attend_variant(q, par, c, True),
                        lambda c: attend_variant(q, par, c, False), carry)

    pv_b[...] = jnp.zeros(pv_b.shape, F32)
    alpha_b[...] = jnp.ones(alpha_b.shape, F32)
    stage_pair(0, ckv_ref, qa2, stage[0])
    m0 = tuple(jnp.full((1, TQ), NEG_BIG, F32) for _ in range(H_ATT))
    l0 = tuple(jnp.zeros((1, TQ), F32) for _ in range(H_ATT))
    m, l = lax.fori_loop(0, n_trips, two_halves(attend_half), (m0, l0))
    for par in range(2):
        @pl.when((n_pairs - 1) % 2 == par)
        def _():
            apply_pending(par)

    yt = []
    for h in range(H_ATT):
        o = acc_scr[:, h * TQ:(h + 1) * TQ] / l[h]
        yt.append(_dot(wuvt_ref[h], o.astype(BF16)))
    for g in range(ATT_WIDTH // TQ):
        per = TQ // HD_ATT
        y_ref[0, :, g * TQ:(g + 1) * TQ] = jnp.concatenate(yt[g * per:(g + 1) * per], axis=0).T


def _dsa_call(iq, iwt, qabs, ik, ckv, ckvt, bias_tiles, wuvt, topk):
    B, T, _ = ik.shape
    NT = T // TQ
    return pl.pallas_call(
        functools.partial(_dsa_kernel, topk=topk),
        grid=(B, NT),
        in_specs=[pl.BlockSpec((1, 1, IDX_DIM, IDX_HEADS * TQ), lambda b, i: (b, i, 0, 0)),
                  pl.BlockSpec((1, IDX_HEADS, TQ), lambda b, i: (b, 0, i)),
                  pl.BlockSpec((1, 1, KV_LATENT, H_ATT * TQ), lambda b, i: (b, i, 0, 0)),
                  pl.BlockSpec((1, T, IDX_DIM), lambda b, i: (b, 0, 0)),
                  pl.BlockSpec((1, T, KV_LATENT), lambda b, i: (b, 0, 0)),
                  pl.BlockSpec((1, NT, KV_LATENT, TQ), lambda b, i: (b, 0, 0, 0)),
                  pl.BlockSpec((3, H_ATT, TQ, TQ), lambda b, i: (0, 0, 0, 0)),
                  pl.BlockSpec((H_ATT, HD_ATT, KV_LATENT), lambda b, i: (0, 0, 0))],
        out_specs=pl.BlockSpec((1, TQ, ATT_WIDTH), lambda b, i: (b, i, 0)),
        out_shape=jax.ShapeDtypeStruct((B, T, ATT_WIDTH), F32),
        scratch_shapes=[pltpu.VMEM((NT, KV, 8, TQ), F32),
                        pltpu.VMEM((2, TQ, H_ATT * TQ), F32),
                        pltpu.VMEM((2, TQ, H_ATT * TQ), F32),
                        pltpu.VMEM((KV_LATENT, H_ATT * TQ), F32),
                        pltpu.VMEM((KV_LATENT, H_ATT * TQ), F32),
                        pltpu.VMEM((1, H_ATT * TQ), F32),
                        pltpu.VMEM((1, H_ATT * TQ), F32),
                        pltpu.VMEM((KV_LATENT, H_ATT * TQ), F32)],
        compiler_params=_cp(("parallel", "arbitrary")),
        name="dsa",
    )(iq, iwt, qabs, ik, ckv, ckvt, bias_tiles, wuvt)


def _rwkv_kernel(p_ref, prev_ref, mu_ref, w0_ref, wdec_ref, a0_ref, waaa_ref, wgate_ref,
                 kk_ref, ka_ref, rk_ref, lng_ref, lnb_ref, y_ref, h_scr):
    c = pl.program_id(1)
    C = CHUNK
    N = HD_RWKV

    @pl.when(c == 0)
    def _():
        h_scr[...] = jnp.zeros(h_scr.shape, F32)

    NS = p_ref.shape[0]
    p = p_ref[...].reshape(NS * C, RW_COLS)
    rid = lax.broadcasted_iota(jnp.int32, (NS * C, 1), 0)
    p_prev = pltpu.roll(p, 1, 0)
    for s in range(NS):
        p_prev = jnp.where(rid == s * C, jnp.where(c > 0, prev_ref[s, 7:8, :], 0.0), p_prev)
    p = p + mu_ref[...] * (p_prev - p)

    W = RWKV_WIDTH
    r = p[:, 0:W]
    k = p[:, W:2 * W]
    v = p[:, 2 * W:3 * W]
    wd = p[:, OFF_WD:OFF_WD + DECAY_LORA]
    ad = p[:, OFF_AD:OFF_AD + AAA_LORA]
    gd = p[:, OFF_GD:OFF_GD + GATE_PAD]

    w_log = -jax.nn.softplus(-(w0_ref[...] + _dot(jnp.tanh(wd).astype(BF16), wdec_ref[...]))) - 0.5
    logw = -jnp.exp(w_log)
    a = jax.nn.sigmoid(a0_ref[...] + _dot(ad.astype(BF16), waaa_ref[...]))
    g = _dot(jax.nn.sigmoid(gd).astype(BF16), wgate_ref[...])

    r64 = lax.broadcasted_iota(jnp.int32, (C, C), 0)
    c64 = lax.broadcasted_iota(jnp.int32, (C, C), 1)
    tril_incl = r64 >= c64
    tril_strict = r64 > c64
    eye = r64 == c64
    ra = lax.broadcasted_iota(jnp.int32, (NS * C, NS * C), 0)
    ca = lax.broadcasted_iota(jnp.int32, (NS * C, NS * C), 1)
    same_seq = (ra >= ca) & (ra - ca <= ra % C)
    cum = jnp.dot(jnp.where(same_seq, 1.0, 0.0), logw, preferred_element_type=F32,
                  precision=lax.Precision.HIGHEST)
    cum_last = cum[C - 1:C, :]
    for s in range(1, NS):
        cum_last = jnp.where(rid >= s * C, cum[(s + 1) * C - 1:(s + 1) * C, :], cum_last)
    g_in = jnp.exp(cum - logw)
    g_t = jnp.exp(cum)
    g_inv = jnp.exp(-cum)
    g_end = jnp.exp(cum_last - cum)
    g_all = jnp.exp(cum_last)

    kkf = k * kk_ref[...]
    kmod = k * (1.0 + (a - 1.0) * ka_ref[...])
    rkr = r * kmod * rk_ref[...]

    P = range(NS * H_RWKV // 2)
    rows = [slice((p // (H_RWKV // 2)) * C, (p // (H_RWKV // 2) + 1) * C) for p in P]
    sls = [slice((p % (H_RWKV // 2)) * 2 * N, (p % (H_RWKV // 2) + 1) * 2 * N) for p in P]
    lane2 = lax.broadcasted_iota(jnp.int32, (C, 2 * N), 1)
    row2 = lax.broadcasted_iota(jnp.int32, (C, 2 * N), 0)
    left = lane2 < N
    col2 = jnp.where(left, lane2, lane2 - N)
    strict2 = row2 > col2
    incl2 = row2 >= col2
    eye2 = row2 == col2

    def per_head(x, op):
        return jnp.where(left, op(jnp.where(left, x, 0.0), axis=-1, keepdims=True),
                         op(jnp.where(left, 0.0, x), axis=-1, keepdims=True))

    def block_diag(x):
        zero = jnp.zeros_like(x)
        return jnp.concatenate([jnp.where(left, x, zero), jnp.where(left, zero, x)], axis=0)

    at, rt, bh, kh, vb, g4 = [], [], [], [], [], []
    for p in P:
        rs, sl = rows[p], sls[p]
        kk_p = kkf[rs, sl]
        kk_p = kk_p / jnp.maximum(jnp.sqrt(per_head(kk_p * kk_p, jnp.sum)), 1e-12)
        k_p = kmod[rs, sl]
        bvec = kk_p * a[rs, sl]
        at_p = ((-kk_p) * g_in[rs, sl]).astype(BF16)
        rt_p = (r[rs, sl] * g_t[rs, sl]).astype(BF16)
        bt_p = (bvec * g_inv[rs, sl]).astype(BF16)
        kt_p = (k_p * g_inv[rs, sl]).astype(BF16)
        wt = jnp.concatenate([block_diag(bt_p), block_diag(kt_p)], axis=0)
        g4.append(_dot_nt(jnp.concatenate([at_p, rt_p], axis=0), wt))
        at.append(at_p)
        rt.append(rt_p)
        bh.append((bvec * g_end[rs, sl]).astype(BF16))
        kh.append((k_p * g_end[rs, sl]).astype(BF16))
        vb.append(v[rs, sl].astype(BF16))

    a_ab = [jnp.where(strict2, g4[p][0:C, 0:2 * C], 0.0) for p in P]
    a_ak = [jnp.where(strict2, g4[p][0:C, 2 * C:4 * C], 0.0).astype(BF16) for p in P]
    a_rb = [jnp.where(incl2, g4[p][C:2 * C, 0:2 * C], 0.0).astype(BF16) for p in P]
    a_rk = [jnp.where(incl2, g4[p][C:2 * C, 2 * C:4 * C], 0.0).astype(BF16) for p in P]

    pw = a_ab
    tinv = [jnp.where(eye2, 1.0, 0.0) for p in P]
    for level in range(6):
        pwb = [pw[p].astype(BF16) for p in P]
        bd = [block_diag(pwb[p]) for p in P]
        if level < 5:
            res = [_dot(jnp.concatenate([pwb[p], tinv[p].astype(BF16)], axis=0), bd[p]) for p in P]
            pw = [res[p][0:C] for p in P]
            tinv = [tinv[p] + res[p][C:2 * C] for p in P]
        else:
            tinv = [tinv[p] + _dot(tinv[p].astype(BF16), bd[p]) for p in P]

    hst = [h_scr[p] for p in P]
    hbd = [block_diag(hst[p].astype(BF16)) for p in P]
    vbd = [block_diag(vb[p]) for p in P]
    x = [_dot(jnp.concatenate([at[p], a_ak[p]], axis=1), jnp.concatenate([hbd[p], vbd[p]], axis=0)) for p in P]
    ub = [_dot(tinv[p].astype(BF16), block_diag(x[p].astype(BF16))).astype(BF16) for p in P]
    y = [_dot(jnp.concatenate([rt[p], a_rb[p], a_rk[p]], axis=1),
              jnp.concatenate([hbd[p], block_diag(ub[p]), vbd[p]], axis=0)) for p in P]
    for p in P:
        ga = g_all[rows[p], sls[p]]
        g_col = jnp.where(left, jnp.sum(jnp.where(row2 == lane2, ga, 0.0), axis=1, keepdims=True),
                          jnp.sum(jnp.where(row2 == lane2 - N, ga, 0.0), axis=1, keepdims=True))
        res = _dot_tn(jnp.concatenate([bh[p], kh[p]], axis=0), jnp.concatenate([ub[p], vb[p]], axis=0))
        h_scr[p] = g_col * hst[p] + jnp.where(left, res[0:N], res[N:2 * N])

    for p in P:
        rs, sl = rows[p], sls[p]
        mean = per_head(y[p], jnp.sum) * (1.0 / N)
        yc = y[p] - mean
        var = per_head(yc * yc, jnp.sum) * (1.0 / N)
        yn = yc * lax.rsqrt(var + GN_EPS) * lng_ref[:, sl] + lnb_ref[:, sl]
        bonus = per_head(rkr[rs, sl], jnp.sum) * v[rs, sl]
        y_ref[p // (H_RWKV // 2), :, sl] = (yn + bonus) * g[rs, sl]


def _rwkv_call(rw, mu_p, w0, wdec, a0, waaa, wgate_p, k_k, k_a, r_k, lng, lnb):
    B, T, _ = rw.shape
    NC = T // CHUNK
    W = RWKV_WIDTH
    c2 = lambda b, c: (0, 0)
    rowspec = pl.BlockSpec((1, W), c2)
    NS = next(n for n in (4, 2, 1) if B % n == 0)
    return pl.pallas_call(
        _rwkv_kernel,
        grid=(B // NS, NC),
        in_specs=[pl.BlockSpec((NS, CHUNK, RW_COLS), lambda b, c: (b, c, 0)),
                  pl.BlockSpec((NS, 8, RW_COLS), lambda b, c: (b, jnp.maximum(c * (CHUNK // 8) - 1, 0), 0)),
                  pl.BlockSpec((1, RW_COLS), c2),
                  rowspec,
                  pl.BlockSpec((DECAY_LORA, W), c2),
                  rowspec,
                  pl.BlockSpec((AAA_LORA, W), c2),
                  pl.BlockSpec((GATE_PAD, W), c2),
                  rowspec, rowspec, rowspec, rowspec, rowspec],
        out_specs=pl.BlockSpec((NS, CHUNK, W), lambda b, c: (b, c, 0)),
        out_shape=jax.ShapeDtypeStruct((B, T, W), F32),
        scratch_shapes=[pltpu.VMEM((NS * H_RWKV // 2, HD_RWKV, 2 * HD_RWKV), F32)],
        compiler_params=_cp(("parallel", "arbitrary")),
        name="rwkv",
    )(rw, rw, mu_p, w0, wdec, a0, waaa, wgate_p, k_k, k_a, r_k, lng, lnb)


def _ffn_kernel(x_ref, ya_ref, yr_ref, mod_ref, gpost_ref, gpre_ref, gfpost_ref,
                wo_ref, wg_ref, wu_ref, wd_ref, o_ref, x1_scr, hf_scr, acc_scr):
    j = pl.program_id(2)

    @pl.when(j == 0)
    def _():
        for rows in _row_chunks(x_ref.shape[1]):
            mix = (_dot(ya_ref[0, rows, :].astype(BF16), wo_ref[0:ATT_WIDTH, :])
                   + _dot(yr_ref[0, rows, :].astype(BF16), wo_ref[ATT_WIDTH:, :]))
            x1 = x_ref[0, rows, :] + mod_ref[0, 2:3, :] * _rms(mix, gpost_ref[...])
            x1_scr[rows, :] = x1
            hf = _rms(x1, gpre_ref[...]) * (1.0 + mod_ref[0, 4:5, :]) + mod_ref[0, 3:4, :]
            hf_scr[rows, :] = hf.astype(BF16)
        acc_scr[...] = jnp.zeros(acc_scr.shape, F32)

    hf = hf_scr[...]
    gate = _dot(hf, wg_ref[...])
    up = _dot(hf, wu_ref[...])
    act = (gate * jax.nn.sigmoid(gate) * up).astype(BF16)
    acc_scr[...] += _dot(act, wd_ref[...])

    @pl.when(j == pl.num_programs(2) - 1)
    def _():
        o_ref[0] = x1_scr[...] + mod_ref[0, 5:6, :] * _rms(acc_scr[...], gfpost_ref[...])


def _ffn_call(x, y_att, y_rwkv, mod3, g_post, g_pre, g_fpost, wo, wg, wu, wd):
    B, T, D = x.shape
    TM = 512
    TF = D_FF // 2
    c3 = lambda b, i, j: (0, 0)
    return pl.pallas_call(
        _ffn_kernel,
        grid=(B, T // TM, D_FF // TF),
        in_specs=[pl.BlockSpec((1, TM, D), lambda b, i, j: (b, i, 0)),
                  pl.BlockSpec((1, TM, ATT_WIDTH), lambda b, i, j: (b, i, 0)),
                  pl.BlockSpec((1, TM, RWKV_WIDTH), lambda b, i, j: (b, i, 0)),
                  pl.BlockSpec((1, 6, D), lambda b, i, j: (b, 0, 0)),
                  pl.BlockSpec((1, D), c3), pl.BlockSpec((1, D), c3), pl.BlockSpec((1, D), c3),
                  pl.BlockSpec((D, D), c3),
                  pl.BlockSpec((D, TF), lambda b, i, j: (0, j)),
                  pl.BlockSpec((D, TF), lambda b, i, j: (0, j)),
                  pl.BlockSpec((TF, D), lambda b, i, j: (j, 0))],
        out_specs=pl.BlockSpec((1, TM, D), lambda b, i, j: (b, i, 0)),
        out_shape=jax.ShapeDtypeStruct((B, T, D), F32),
        scratch_shapes=[pltpu.VMEM((TM, D), F32), pltpu.VMEM((TM, D), BF16), pltpu.VMEM((TM, D), F32)],
        compiler_params=_cp(("parallel", "parallel", "arbitrary")),
        name="ffn",
    )(x, y_att, y_rwkv, mod3, g_post, g_pre, g_fpost, wo, wg, wu, wd)


def _pad_cols(w, n):
    return jnp.pad(w, ((0, 0), (0, n - w.shape[1])))


def _layer(x, mod3, rel_bias, mix_pre_norm, mix_post_norm, ffn_pre_norm, ffn_post_norm,
           w_in, q_norm, w_uq, w_idx_q, kv_norm, idx_k_norm, w_uk, w_uv,
           mu_shift, w0, w_decay_up, a0, w_aaa_up, w_gate_up, k_k, k_a, r_k, ln_x_gain, ln_x_bias,
           w_out, w_ffn_gate, w_ffn_up, w_ffn_down):
    B, T, D = x.shape
    att_cols = Q_LORA + KV_LATENT + IDX_DIM + IDX_HEADS
    n_main = 3 * RWKV_WIDTH + DECAY_LORA + AAA_LORA
    w_att = _pad_cols(w_in[:, :att_cols], ATT_PAD)
    w_rw = w_in[:, att_cols:]
    w_in_p = jnp.concatenate([w_att, w_rw[:, :n_main], _pad_cols(w_rw[:, n_main:], GATE_PAD)], axis=1).astype(BF16)
    mu_p = jnp.concatenate([mu_shift[:n_main], jnp.pad(mu_shift[n_main:], (0, GATE_PAD - GATE_LORA))]).reshape(1, RW_COLS)
    wgate_p = jnp.pad(w_gate_up, ((0, GATE_PAD - GATE_LORA), (0, 0))).astype(BF16)

    att, rw = _inproj_call(x, mod3, mix_pre_norm.reshape(1, D), w_in_p)

    wuq = w_uq.reshape(Q_LORA, ATT_WIDTH).astype(BF16)
    wiq = w_idx_q.reshape(Q_LORA, IDX_HEADS * IDX_DIM).astype(BF16)
    wukt = jnp.transpose(w_uk, (0, 2, 1)).astype(BF16)
    qabs, iq, ik, ckv, ckvt, iwt = _dsaprep_call(att, q_norm.reshape(1, -1), kv_norm.reshape(1, -1),
                                                 idx_k_norm.reshape(1, -1), wuq, wiq, wukt)
    bias_tiles = _bias_call(rel_bias)
    wuvt = jnp.transpose(w_uv, (0, 2, 1)).astype(BF16)
    topk = min(TOPK_MAX, T // 4)
    y_att = _dsa_call(iq, iwt, qabs, ik, ckv, ckvt, bias_tiles, wuvt, topk)

    row = lambda z: z.reshape(1, RWKV_WIDTH)
    y_rwkv = _rwkv_call(rw, mu_p, row(w0), w_decay_up.astype(BF16), row(a0), w_aaa_up.astype(BF16),
                        wgate_p, row(k_k), row(k_a), row(r_k), row(ln_x_gain), row(ln_x_bias))

    return _ffn_call(x, y_att, y_rwkv, mod3, mix_post_norm.reshape(1, D), ffn_pre_norm.reshape(1, D),
                     ffn_post_norm.reshape(1, D), w_out.astype(BF16), w_ffn_gate.astype(BF16),
                     w_ffn_up.astype(BF16), w_ffn_down.astype(BF16))


def kernel(x, c, rel_bias, ada_w, ada_b, mix_pre_norm, mix_post_norm, ffn_pre_norm, ffn_post_norm, w_in, q_norm, w_uq, w_idx_q, kv_norm, idx_k_norm, w_uk, w_uv, mu_shift, w0, w_decay_up, a0, w_aaa_up, w_gate_up, k_k, k_a, r_k, ln_x_gain, ln_x_bias, w_out, w_ffn_gate, w_ffn_up, w_ffn_down):
    B, T, D = x.shape
    assert D == D_MODEL and T % 512 == 0 and T // 4 >= 1
    layer_params = (mix_pre_norm, mix_post_norm, ffn_pre_norm, ffn_post_norm,
                    w_in, q_norm, w_uq, w_idx_q, kv_norm, idx_k_norm, w_uk, w_uv,
                    mu_shift, w0, w_decay_up, a0, w_aaa_up, w_gate_up, k_k, k_a, r_k, ln_x_gain, ln_x_bias,
                    w_out, w_ffn_gate, w_ffn_up, w_ffn_down)
    for l in range(ada_w.shape[0]):
        mod3 = _mod_call(c, ada_w[l], ada_b[l]).reshape(B, 6, D)
        x = _layer(x, mod3, rel_bias, *[p[l] for p in layer_params])
    return x
```

```python
import functools
import math

import numpy as np
import jax
import jax.numpy as jnp
from jax import lax
from jax.experimental import pallas as pl
from jax.experimental.pallas import tpu as pltpu

F32 = jnp.float32
BF16 = jnp.bfloat16

D_MODEL = 1024
ATT_WIDTH = 512
RWKV_WIDTH = 512
HD_ATT = 64
H_ATT = 8
HD_RWKV = 64
H_RWKV = 8
Q_LORA = 256
KV_LATENT = 128
IDX_HEADS = 8
IDX_DIM = 64
TOPK_MAX = 256
NUM_BUCKETS = 32
MAX_DISTANCE = 128
DECAY_LORA = 64
AAA_LORA = 64
GATE_LORA = 160
GN_EPS = 64e-5
D_FF = 2816
NORM_EPS = 1e-6

ATT_PAD = 512
GATE_PAD = 256
RW_COLS = 3 * RWKV_WIDTH + DECAY_LORA + AAA_LORA + GATE_PAD
OFF_WD = 3 * RWKV_WIDTH
OFF_AD = OFF_WD + DECAY_LORA
OFF_GD = OFF_AD + AAA_LORA

TQ = 128
CHUNK = 64
NEG_BIG = -1e30
LOWEST = -3.0e38
LOG2E = math.log2(math.e)
VMEM_LIMIT = 56 * 1024 * 1024


def _cp(sem):
    return pltpu.CompilerParams(dimension_semantics=sem, vmem_limit_bytes=VMEM_LIMIT)


def _dot(a, b):
    return jnp.dot(a, b, preferred_element_type=F32)


def _dot_nt(a, b):
    return lax.dot_general(a, b, (((1,), (1,)), ((), ())), preferred_element_type=F32)


def _dot_tn(a, b):
    return lax.dot_general(a, b, (((0,), (0,)), ((), ())), preferred_element_type=F32)


def _rms(z, gain):
    return z * lax.rsqrt(jnp.mean(z * z, axis=-1, keepdims=True) + NORM_EPS) * gain


def _row_chunks(n, chunk=256):
    return [slice(r, r + chunk) for r in range(0, n, chunk)]


def _mod_kernel(c_ref, w_ref, b_ref, o_ref):
    c = c_ref[...]
    s = c * jax.nn.sigmoid(c)
    o_ref[...] = _dot(s.astype(BF16), w_ref[...].astype(BF16)) + b_ref[...]


def _mod_call(c, ada_w, ada_b):
    B, D = c.shape
    N = ada_w.shape[1]
    TN = 1536
    return pl.pallas_call(
        _mod_kernel,
        grid=(N // TN,),
        in_specs=[pl.BlockSpec((B, D), lambda j: (0, 0)),
                  pl.BlockSpec((D, TN), lambda j: (0, j)),
                  pl.BlockSpec((1, TN), lambda j: (0, j))],
        out_specs=pl.BlockSpec((B, TN), lambda j: (0, j)),
        out_shape=jax.ShapeDtypeStruct((B, N), F32),
        compiler_params=_cp(("parallel",)),
        name="mod",
    )(c, ada_w, ada_b.reshape(1, N))


def _inproj_kernel(x_ref, mod_ref, g_ref, w_ref, att_ref, rw_ref):
    sh = mod_ref[0, 0:1, :]
    sc = mod_ref[0, 1:2, :]
    for rows in _row_chunks(x_ref.shape[1]):
        h = (_rms(x_ref[0, rows, :], g_ref[...]) * (1.0 + sc) + sh).astype(BF16)
        att_ref[0, rows, :] = _dot(h, w_ref[:, 0:ATT_PAD])
        for n0 in range(0, RW_COLS, 384):
            rw_ref[0, rows, n0:n0 + 384] = _dot(h, w_ref[:, ATT_PAD + n0:ATT_PAD + n0 + 384])


def _inproj_call(x, mod3, gain, w_in_p):
    B, T, D = x.shape
    TM = 512
    NP = w_in_p.shape[1]
    return pl.pallas_call(
        _inproj_kernel,
        grid=(B, T // TM),
        in_specs=[pl.BlockSpec((1, TM, D), lambda b, i: (b, i, 0)),
                  pl.BlockSpec((1, 6, D), lambda b, i: (b, 0, 0)),
                  pl.BlockSpec((1, D), lambda b, i: (0, 0)),
                  pl.BlockSpec((D, NP), lambda b, i: (0, 0))],
        out_specs=[pl.BlockSpec((1, TM, ATT_PAD), lambda b, i: (b, i, 0)),
                   pl.BlockSpec((1, TM, RW_COLS), lambda b, i: (b, i, 0))],
        out_shape=[jax.ShapeDtypeStruct((B, T, ATT_PAD), F32),
                   jax.ShapeDtypeStruct((B, T, RW_COLS), F32)],
        compiler_params=_cp(("parallel", "parallel")),
        name="inproj",
    )(x, mod3, gain, w_in_p)


def _dsaprep_kernel(att_ref, qn_ref, kvn_ref, ikn_ref, wuq_ref, wiq_ref, wukt_ref,
                    qabs_ref, iq_ref, ik_ref, ckv_ref, ckvt_ref, iwt_ref):
    att = att_ref[0]
    tm = att.shape[0]
    o1 = Q_LORA
    o2 = o1 + KV_LATENT
    o3 = o2 + IDX_DIM
    cq = _rms(att[:, 0:o1], qn_ref[...]).astype(BF16)
    ckv = _rms(att[:, o1:o2], kvn_ref[...])
    ckv_ref[0] = ckv.astype(BF16)
    ik_ref[0] = _rms(att[:, o2:o3], ikn_ref[...]).astype(BF16)
    tail = att[:, o2:ATT_PAD]
    w_off = o3 - o2
    for sb in range(tm // TQ):
        rows = slice(sb * TQ, (sb + 1) * TQ)
        ckvt_ref[0, sb] = ckv[rows, :].T.astype(BF16)
        iwt_ref[0, :, rows] = tail[rows, :].T[w_off:w_off + IDX_HEADS, :] * (IDX_HEADS ** -0.5 * IDX_DIM ** -0.5)
    q = _dot(cq, wuq_ref[...])
    iqv = _dot(cq, wiq_ref[...])
    for h in range(H_ATT):
        qh = q[:, h * HD_ATT:(h + 1) * HD_ATT].astype(BF16)
        qa = _dot(qh, wukt_ref[h]) * (HD_ATT ** -0.5 * LOG2E)
        for sb in range(tm // TQ):
            qabs_ref[0, sb, :, h * TQ:(h + 1) * TQ] = qa[sb * TQ:(sb + 1) * TQ, :].T.astype(BF16)
    per = TQ // IDX_DIM
    for g in range(IDX_HEADS // per):
        for sb in range(tm // TQ):
            blk = iqv[sb * TQ:(sb + 1) * TQ, g * TQ:(g + 1) * TQ].T.astype(BF16)
            for u in range(per):
                h = g * per + u
                iq_ref[0, sb, :, h * TQ:(h + 1) * TQ] = blk[u * IDX_DIM:(u + 1) * IDX_DIM, :]


def _dsaprep_call(att, q_norm, kv_norm, idx_k_norm, wuq, wiq, wukt):
    B, T, _ = att.shape
    TM = 512
    const2 = lambda b, i: (0, 0)
    return pl.pallas_call(
        _dsaprep_kernel,
        grid=(B, T // TM),
        in_specs=[pl.BlockSpec((1, TM, ATT_PAD), lambda b, i: (b, i, 0)),
                  pl.BlockSpec((1, Q_LORA), const2),
                  pl.BlockSpec((1, KV_LATENT), const2),
                  pl.BlockSpec((1, IDX_DIM), const2),
                  pl.BlockSpec((Q_LORA, ATT_WIDTH), const2),
                  pl.BlockSpec((Q_LORA, IDX_HEADS * IDX_DIM), const2),
                  pl.BlockSpec((H_ATT, HD_ATT, KV_LATENT), lambda b, i: (0, 0, 0))],
        out_specs=[pl.BlockSpec((1, TM // TQ, KV_LATENT, H_ATT * TQ), lambda b, i: (b, i, 0, 0)),
                   pl.BlockSpec((1, TM // TQ, IDX_DIM, IDX_HEADS * TQ), lambda b, i: (b, i, 0, 0)),
                   pl.BlockSpec((1, TM, IDX_DIM), lambda b, i: (b, i, 0)),
                   pl.BlockSpec((1, TM, KV_LATENT), lambda b, i: (b, i, 0)),
                   pl.BlockSpec((1, TM // TQ, KV_LATENT, TQ), lambda b, i: (b, i, 0, 0)),
                   pl.BlockSpec((1, IDX_HEADS, TM), lambda b, i: (b, 0, i))],
        out_shape=[jax.ShapeDtypeStruct((B, T // TQ, KV_LATENT, H_ATT * TQ), BF16),
                   jax.ShapeDtypeStruct((B, T // TQ, IDX_DIM, IDX_HEADS * TQ), BF16),
                   jax.ShapeDtypeStruct((B, T, IDX_DIM), BF16),
                   jax.ShapeDtypeStruct((B, T, KV_LATENT), BF16),
                   jax.ShapeDtypeStruct((B, T // TQ, KV_LATENT, TQ), BF16),
                   jax.ShapeDtypeStruct((B, IDX_HEADS, T), F32)],
        compiler_params=_cp(("parallel", "parallel")),
        name="dsaprep",
    )(att, q_norm, kv_norm, idx_k_norm, wuq, wiq, wukt)


def _t5_bucket_np(rel):
    max_exact = NUM_BUCKETS // 2
    nf = np.maximum(rel, 1).astype(np.float32)
    large = max_exact + (np.log(nf / max_exact) / math.log(MAX_DISTANCE / max_exact)
                         * (NUM_BUCKETS - max_exact)).astype(np.int32)
    large = np.minimum(large, NUM_BUCKETS - 1)
    return np.where(rel < max_exact, rel, large).astype(np.int32)


def _near_bucket_tiles():
    s = np.arange(TQ)[:, None]
    t = np.arange(TQ)[None, :]
    tiles = [_t5_bucket_np(np.maximum(t - s + TQ * d, 0)) for d in range(2)]
    assert _t5_bucket_np(np.array([TQ + 1]))[0] == NUM_BUCKETS - 1
    return np.stack(tiles)


def _bias_kernel(rb_ref, bk_ref, o_ref):
    for d in range(2):
        bk = bk_ref[d]
        for h in range(H_ATT):
            def body(b, acc):
                return jnp.where(bk == b, rb_ref[b, h], acc)
            acc = lax.fori_loop(0, NUM_BUCKETS, body, jnp.zeros((TQ, TQ), F32))
            o_ref[d, h] = (acc - rb_ref[NUM_BUCKETS - 1, h]) * LOG2E
    o_ref[2] = jnp.zeros((H_ATT, TQ, TQ), F32)


def _bias_call(rel_bias):
    buckets = jnp.asarray(_near_bucket_tiles())
    return pl.pallas_call(
        _bias_kernel,
        in_specs=[pl.BlockSpec(memory_space=pltpu.SMEM),
                  pl.BlockSpec(memory_space=pltpu.VMEM)],
        out_specs=pl.BlockSpec(memory_space=pltpu.VMEM),
        out_shape=jax.ShapeDtypeStruct((3, H_ATT, TQ, TQ), F32),
        name="bias",
    )(rel_bias, buckets)


KV = TQ // 8
FREE_PROBES = 10
COUNT_PROBES = 6


def _key_reduce(x, op):
    return op(op(x, axis=0), axis=0, keepdims=True)


def _dsa_kernel(iq_ref, iwt_ref, qabs_ref, ik_ref, ckv_ref, ckvt_ref, bias_ref, wuvt_ref, y_ref,
                score_scr, stage_a, stage_b, pv_a, pv_b, alpha_a, alpha_b, acc_scr, *, topk):
    i = pl.program_id(1)
    kf = float(topk)
    kidx = (lax.broadcasted_iota(jnp.int32, (KV, 8, TQ), 0) * 8
            + lax.broadcasted_iota(jnp.int32, (KV, 8, TQ), 1))
    qidx = lax.broadcasted_iota(jnp.int32, (KV, 8, TQ), 2)
    n_pairs = i // 2 + 1

    iwt = iwt_ref[0]
    w8 = [jnp.broadcast_to(iwt[h:h + 1, :], (8, TQ))[None] for h in range(IDX_HEADS)]
    iq2 = iq_ref[0, 0]

    n_trips = (n_pairs + 1) // 2
    stage = (stage_a, stage_b)

    def stage_pair(q, lhs_ref, rhs, dst):
        nxt = jnp.minimum(q, n_pairs - 1)
        rows = lhs_ref[0, pl.ds(pl.multiple_of(2 * nxt * TQ, 2 * TQ), 2 * TQ), :]
        dst[...] = _dot(rows, rhs).reshape(2, TQ, H_ATT * TQ)

    def two_halves(half):
        def trip(r, carry):
            carry = half(2 * r, 0, carry)
            return lax.cond(2 * r + 1 < n_pairs, lambda c: half(2 * r + 1, 1, c), lambda c: c, carry)
        return trip

    def score_half(q, par, carry):
        rmax, rmin = carry
        stage_pair(q + 1, ik_ref, iq2, stage[1 - par])
        for blk in range(2):
            j = 2 * q + blk
            sc = jnp.zeros((KV, 8, TQ), F32)
            for h in range(IDX_HEADS):
                s = stage[par][blk, :, h * TQ:(h + 1) * TQ].reshape(KV, 8, TQ)
                sc = sc + jnp.maximum(s, 0.0) * w8[h]
            causal = kidx + (j - i) * TQ <= qidx
            score_scr[j] = jnp.where(causal, sc, -jnp.inf)
            rmax = jnp.maximum(rmax, jnp.max(jnp.where(causal, sc, -jnp.inf), axis=0))
            rmin = jnp.minimum(rmin, jnp.min(jnp.where(causal, sc, jnp.inf), axis=0))
        return rmax, rmin

    stage_pair(0, ik_ref, iq2, stage[0])
    rmax, rmin = lax.fori_loop(0, n_trips, two_halves(score_half),
                               (jnp.full((8, TQ), -jnp.inf, F32), jnp.full((8, TQ), jnp.inf, F32)))
    smax = jnp.max(rmax, axis=0, keepdims=True)
    smin = jnp.min(rmin, axis=0, keepdims=True)

    nvalid = (i * TQ + lax.broadcasted_iota(jnp.int32, (1, TQ), 1) + 1).astype(F32)
    needs_thr = nvalid > kf

    def probe(v):
        vb = jnp.broadcast_to(v, (8, TQ))[None, None]

        def body(q, st):
            cnt, dn, up = st
            for j in (2 * q, 2 * q + 1):
                s = score_scr[j].reshape(4, KV // 4, 8, TQ)
                ge = s >= vb
                cnt = cnt + jnp.sum(jnp.where(ge, 1.0, 0.0), axis=1)
                dn = jnp.minimum(dn, jnp.min(jnp.where(ge, s, jnp.inf), axis=1))
                up = jnp.maximum(up, jnp.max(jnp.where(ge, -jnp.inf, s), axis=1))
            return cnt, dn, up
        cnt, dn, up = lax.fori_loop(0, n_pairs, body, (jnp.zeros((4, 8, TQ), F32),
                                                       jnp.full((4, 8, TQ), jnp.inf, F32),
                                                       jnp.full((4, 8, TQ), -jnp.inf, F32)))
        return (jnp.sum(jnp.sum(cnt, axis=0), axis=0, keepdims=True),
                jnp.min(jnp.min(dn, axis=0), axis=0, keepdims=True),
                jnp.max(jnp.max(up, axis=0), axis=0, keepdims=True))

    def probe_count(v):
        vb = jnp.broadcast_to(v, (8, TQ))[None, None]

        def body(q, cnt):
            for j in (2 * q, 2 * q + 1):
                s = score_scr[j].reshape(4, KV // 4, 8, TQ)
                cnt = cnt + jnp.sum(jnp.where(s >= vb, 1.0, 0.0), axis=1)
            return cnt
        cnt = lax.fori_loop(0, n_pairs, body, jnp.zeros((4, 8, TQ), F32))
        return jnp.sum(jnp.sum(cnt, axis=0), axis=0, keepdims=True)

    def logit(c):
        return jnp.log((c + 0.5) / (nvalid - c + 0.5))

    g_k = logit(jnp.full((1, TQ), kf - 0.5, F32))

    def search_step(st, snap=True):
        lo, hi, clo, ghi, slo, shi = st
        is_open = lo < hi
        g_lo = logit(clo)
        g_hi = logit(ghi)
        phi = jnp.clip((g_lo - g_k) / (g_lo - g_hi), 0.02, 0.98)
        stalled = (slo >= 2.0) | (shi >= 2.0)
        mid = jnp.where(stalled, 0.5 * lo + 0.5 * hi, lo + (hi - lo) * phi)
        mid = jnp.where((mid > lo) & (mid <= hi), mid, hi)
        if snap:
            c, dn, up = probe(mid)
        else:
            c = probe_count(mid)
            dn = up = mid
        ge = c >= kf
        move_lo = is_open & ge
        move_hi = is_open & jnp.logical_not(ge)
        lo = jnp.where(move_lo, dn, lo)
        clo = jnp.where(move_lo, c, clo)
        hi = jnp.where(move_hi, up, hi)
        ghi = jnp.where(move_hi, c, ghi)
        slo = jnp.where(stalled | ge, 0.0, slo + 1.0)
        shi = jnp.where(stalled | jnp.logical_not(ge), 0.0, shi + 1.0)
        return lo, hi, clo, ghi, slo, shi

    def n_open(st):
        return jnp.sum(jnp.where(st[0] < st[1], 1.0, 0.0))

    def checked_step(carry):
        st = search_step(carry[0])
        return st, n_open(st), carry[2] + 1

    search_cap = ik_ref.shape[1] + 2
    zero = jnp.zeros((1, TQ), F32)
    st = (smin, jnp.where(needs_thr, smax, smin), nvalid, zero, zero, zero)
    st = lax.fori_loop(0, COUNT_PROBES, lambda _, s: search_step(s, snap=False), st)
    st = lax.fori_loop(COUNT_PROBES, FREE_PROBES, lambda _, s: search_step(s), st)
    st, _, _ = lax.while_loop(lambda c: jnp.logical_and(c[1] > 0.0, c[2] < search_cap), checked_step,
                              (st, n_open(st), jnp.int32(0)))
    thr, _, clo, chi, _, _ = st

    tie = needs_thr & (clo > kf)
    n_tie = jnp.sum(jnp.where(tie, 1.0, 0.0))

    @pl.when(n_tie > 0.0)
    def _():
        r2 = lax.broadcasted_iota(jnp.int32, (TQ, TQ), 0)
        c2 = lax.broadcasted_iota(jnp.int32, (TQ, TQ), 1)
        lower = jnp.where(c2 <= r2, 1.0, 0.0).astype(BF16)
        thrb = jnp.broadcast_to(thr, (TQ, TQ))
        tieb = jnp.broadcast_to(tie, (TQ, TQ))
        needb = jnp.broadcast_to(kf - chi, (TQ, TQ))

        def body(j, carry):
            s = score_scr[j].reshape(TQ, TQ)
            eq = (s == thrb) & tieb
            pref = _dot(lower, jnp.where(eq, 1.0, 0.0).astype(BF16)) + carry
            score_scr[j] = jnp.where(eq & (pref > needb), -jnp.inf, s).reshape(KV, 8, TQ)
            return jnp.broadcast_to(pref[TQ - 1:TQ, :], (TQ, TQ))
        lax.fori_loop(0, i + 1, body, jnp.zeros((TQ, TQ), F32))

    thr_eff = jnp.broadcast_to(jnp.where(needs_thr, thr, LOWEST), (8, TQ))[None]

    acc_scr[...] = jnp.zeros(acc_scr.shape, F32)
    qa2 = qabs_ref[0, 0]

    pv = (pv_a, pv_b)
    alpha_st = (alpha_a, alpha_b)

    def apply_pending(par):
        acc_scr[...] = alpha_st[par][...] * acc_scr[...] + pv[par][...]

    def attend_variant(q, par, carry, near):
        m = list(carry)
        stage_pair(q + 1, ckv_ref, qa2, stage[1 - par])
        js = (2 * q, 2 * q + 1)
        sel = [score_scr[j] >= thr_eff for j in js]
        d = [jnp.clip(i - j, 0, 2) for j in js]
        kvt = jnp.concatenate([jnp.concatenate([ckvt_ref[0, js[0]], ckvt_ref[0, js[1]]], axis=1),
                               jnp.ones((8, 2 * TQ), BF16)], axis=0)
        ps, alphas = [], []
        for h in range(H_ATT):
            lg = [stage[par][b, :, h * TQ:(h + 1) * TQ] for b in range(2)]
            if near:
                lg = [lg[b] + bias_ref[d[b], h] for b in range(2)]
            lg = [jnp.where(sel[b], lg[b].reshape(KV, 8, TQ), NEG_BIG) for b in range(2)]
            m_new = jnp.maximum(m[h], jnp.maximum(_key_reduce(lg[0], jnp.max), _key_reduce(lg[1], jnp.max)))
            alpha = jnp.exp2(m[h] - m_new)
            mb = jnp.broadcast_to(m_new, (8, TQ))[None]
            p = [jnp.exp2(lg[b] - mb) for b in range(2)]
            m[h] = m_new
            alphas.append(alpha)
            ps.append(jnp.concatenate([p[0].reshape(TQ, TQ), p[1].reshape(TQ, TQ)], axis=0).astype(BF16))
            if h % 2 == 1:
                cols = slice((h - 1) * TQ, (h + 1) * TQ)
                pv[par][:, cols] = _dot(kvt, jnp.concatenate(ps[h - 1:h + 1], axis=1))
        alpha_st[par][...] = jnp.concatenate(alphas, axis=1)
        apply_pending(1 - par)
        return tuple(m)

    def attend_half(q, par, carry):
        return lax.cond(2 * q + 1 >= i - 1,
                        lambda c: attend_variant(q, par, c, True),
                        lambda c: attend_variant(q, par, c, False), carry)

    pv_b[...] = jnp.zeros(pv_b.shape, F32)
    alpha_b[...] = jnp.ones(alpha_b.shape, F32)
    stage_pair(0, ckv_ref, qa2, stage[0])
    m0 = tuple(jnp.full((1, TQ), NEG_BIG, F32) for _ in range(H_ATT))
    lax.fori_loop(0, n_trips, two_halves(attend_half), m0)
    for par in range(2):
        @pl.when((n_pairs - 1) % 2 == par)
        def _():
            apply_pending(par)

    yt = []
    for h in range(H_ATT):
        cols = slice(h * TQ, (h + 1) * TQ)
        o = acc_scr[0:KV_LATENT, cols] / acc_scr[KV_LATENT:KV_LATENT + 1, cols]
        yt.append(_dot(wuvt_ref[h], o.astype(BF16)))
    for g in range(ATT_WIDTH // TQ):
        per = TQ // HD_ATT
        y_ref[0, :, g * TQ:(g + 1) * TQ] = jnp.concatenate(yt[g * per:(g + 1) * per], axis=0).T


def _dsa_call(iq, iwt, qabs, ik, ckv, ckvt, bias_tiles, wuvt, topk):
    B, T, _ = ik.shape
    NT = T // TQ
    return pl.pallas_call(
        functools.partial(_dsa_kernel, topk=topk),
        grid=(B, NT),
        in_specs=[pl.BlockSpec((1, 1, IDX_DIM, IDX_HEADS * TQ), lambda b, i: (b, i, 0, 0)),
                  pl.BlockSpec((1, IDX_HEADS, TQ), lambda b, i: (b, 0, i)),
                  pl.BlockSpec((1, 1, KV_LATENT, H_ATT * TQ), lambda b, i: (b, i, 0, 0)),
                  pl.BlockSpec((1, T, IDX_DIM), lambda b, i: (b, 0, 0)),
                  pl.BlockSpec((1, T, KV_LATENT), lambda b, i: (b, 0, 0)),
                  pl.BlockSpec((1, NT, KV_LATENT, TQ), lambda b, i: (b, 0, 0, 0)),
                  pl.BlockSpec((3, H_ATT, TQ, TQ), lambda b, i: (0, 0, 0, 0)),
                  pl.BlockSpec((H_ATT, HD_ATT, KV_LATENT), lambda b, i: (0, 0, 0))],
        out_specs=pl.BlockSpec((1, TQ, ATT_WIDTH), lambda b, i: (b, i, 0)),
        out_shape=jax.ShapeDtypeStruct((B, T, ATT_WIDTH), F32),
        scratch_shapes=[pltpu.VMEM((NT, KV, 8, TQ), F32),
                        pltpu.VMEM((2, TQ, H_ATT * TQ), F32),
                        pltpu.VMEM((2, TQ, H_ATT * TQ), F32),
                        pltpu.VMEM((KV_LATENT + 8, H_ATT * TQ), F32),
                        pltpu.VMEM((KV_LATENT + 8, H_ATT * TQ), F32),
                        pltpu.VMEM((1, H_ATT * TQ), F32),
                        pltpu.VMEM((1, H_ATT * TQ), F32),
                        pltpu.VMEM((KV_LATENT + 8, H_ATT * TQ), F32)],
        compiler_params=_cp(("parallel", "arbitrary")),
        name="dsa",
    )(iq, iwt, qabs, ik, ckv, ckvt, bias_tiles, wuvt)


def _rwkv_kernel(p_ref, prev_ref, mu_ref, w0_ref, wdec_ref, a0_ref, waaa_ref, wgate_ref,
                 kk_ref, ka_ref, rk_ref, lng_ref, lnb_ref, y_ref, h_scr):
    c = pl.program_id(1)
    C = CHUNK
    N = HD_RWKV

    @pl.when(c == 0)
    def _():
        h_scr[...] = jnp.zeros(h_scr.shape, F32)

    NS = p_ref.shape[0]
    p = p_ref[...].reshape(NS * C, RW_COLS)
    rid = lax.broadcasted_iota(jnp.int32, (NS * C, 1), 0)
    p_prev = pltpu.roll(p, 1, 0)
    for s in range(NS):
        p_prev = jnp.where(rid == s * C, jnp.where(c > 0, prev_ref[s, 7:8, :], 0.0), p_prev)
    p = p + mu_ref[...] * (p_prev - p)

    W = RWKV_WIDTH
    r = p[:, 0:W]
    k = p[:, W:2 * W]
    v = p[:, 2 * W:3 * W]
    wd = p[:, OFF_WD:OFF_WD + DECAY_LORA]
    ad = p[:, OFF_AD:OFF_AD + AAA_LORA]
    gd = p[:, OFF_GD:OFF_GD + GATE_PAD]

    w_log = -jax.nn.softplus(-(w0_ref[...] + _dot(jnp.tanh(wd).astype(BF16), wdec_ref[...]))) - 0.5
    logw = -jnp.exp(w_log)
    a = jax.nn.sigmoid(a0_ref[...] + _dot(ad.astype(BF16), waaa_ref[...]))
    g = _dot(jax.nn.sigmoid(gd).astype(BF16), wgate_ref[...])

    r64 = lax.broadcasted_iota(jnp.int32, (C, C), 0)
    c64 = lax.broadcasted_iota(jnp.int32, (C, C), 1)
    tril_incl = r64 >= c64
    tril_strict = r64 > c64
    eye = r64 == c64
    ra = lax.broadcasted_iota(jnp.int32, (NS * C, NS * C), 0)
    ca = lax.broadcasted_iota(jnp.int32, (NS * C, NS * C), 1)
    same_seq = (ra >= ca) & (ra - ca <= ra % C)
    cum = jnp.dot(jnp.where(same_seq, 1.0, 0.0), logw, preferred_element_type=F32,
                  precision=lax.Precision.HIGHEST)
    cum_last = cum[C - 1:C, :]
    for s in range(1, NS):
        cum_last = jnp.where(rid >= s * C, cum[(s + 1) * C - 1:(s + 1) * C, :], cum_last)
    g_in = jnp.exp(cum - logw)
    g_t = jnp.exp(cum)
    g_inv = jnp.exp(-cum)
    g_end = jnp.exp(cum_last - cum)
    g_all = jnp.exp(cum_last)

    kkf = k * kk_ref[...]
    kmod = k * (1.0 + (a - 1.0) * ka_ref[...])
    rkr = r * kmod * rk_ref[...]

    P = range(NS * H_RWKV // 2)
    rows = [slice((p // (H_RWKV // 2)) * C, (p // (H_RWKV // 2) + 1) * C) for p in P]
    sls = [slice((p % (H_RWKV // 2)) * 2 * N, (p % (H_RWKV // 2) + 1) * 2 * N) for p in P]
    lane2 = lax.broadcasted_iota(jnp.int32, (C, 2 * N), 1)
    row2 = lax.broadcasted_iota(jnp.int32, (C, 2 * N), 0)
    left = lane2 < N
    col2 = jnp.where(left, lane2, lane2 - N)
    strict2 = row2 > col2
    incl2 = row2 >= col2
    eye2 = row2 == col2

    def per_head(x, op):
        return jnp.where(left, op(jnp.where(left, x, 0.0), axis=-1, keepdims=True),
                         op(jnp.where(left, 0.0, x), axis=-1, keepdims=True))

    def block_diag(x):
        zero = jnp.zeros_like(x)
        return jnp.concatenate([jnp.where(left, x, zero), jnp.where(left, zero, x)], axis=0)

    at, rt, bh, kh, vb, g4 = [], [], [], [], [], []
    for p in P:
        rs, sl = rows[p], sls[p]
        kk_p = kkf[rs, sl]
        kk_p = kk_p / jnp.maximum(jnp.sqrt(per_head(kk_p * kk_p, jnp.sum)), 1e-12)
        k_p = kmod[rs, sl]
        bvec = kk_p * a[rs, sl]
        at_p = ((-kk_p) * g_in[rs, sl]).astype(BF16)
        rt_p = (r[rs, sl] * g_t[rs, sl]).astype(BF16)
        bt_p = (bvec * g_inv[rs, sl]).astype(BF16)
        kt_p = (k_p * g_inv[rs, sl]).astype(BF16)
        wt = jnp.concatenate([block_diag(bt_p), block_diag(kt_p)], axis=0)
        g4.append(_dot_nt(jnp.concatenate([at_p, rt_p], axis=0), wt))
        at.append(at_p)
        rt.append(rt_p)
        bh.append((bvec * g_end[rs, sl]).astype(BF16))
        kh.append((k_p * g_end[rs, sl]).astype(BF16))
        vb.append(v[rs, sl].astype(BF16))

    a_ab = [jnp.where(strict2, g4[p][0:C, 0:2 * C], 0.0) for p in P]
    a_ak = [jnp.where(strict2, g4[p][0:C, 2 * C:4 * C], 0.0).astype(BF16) for p in P]
    a_rb = [jnp.where(incl2, g4[p][C:2 * C, 0:2 * C], 0.0).astype(BF16) for p in P]
    a_rk = [jnp.where(incl2, g4[p][C:2 * C, 2 * C:4 * C], 0.0).astype(BF16) for p in P]

    pw = a_ab
    tinv = [jnp.where(eye2, 1.0, 0.0) for p in P]
    for level in range(6):
        pwb = [pw[p].astype(BF16) for p in P]
        bd = [block_diag(pwb[p]) for p in P]
        if level < 5:
            res = [_dot(jnp.concatenate([pwb[p], tinv[p].astype(BF16)], axis=0), bd[p]) for p in P]
            pw = [res[p][0:C] for p in P]
            tinv = [tinv[p] + res[p][C:2 * C] for p in P]
        else:
            tinv = [tinv[p] + _dot(tinv[p].astype(BF16), bd[p]) for p in P]

    hst = [h_scr[p] for p in P]
    hbd = [block_diag(hst[p].astype(BF16)) for p in P]
    vbd = [block_diag(vb[p]) for p in P]
    x = [_dot(jnp.concatenate([at[p], a_ak[p]], axis=1), jnp.concatenate([hbd[p], vbd[p]], axis=0)) for p in P]
    ub = [_dot(tinv[p].astype(BF16), block_diag(x[p].astype(BF16))).astype(BF16) for p in P]
    y = [_dot(jnp.concatenate([rt[p], a_rb[p], a_rk[p]], axis=1),
              jnp.concatenate([hbd[p], block_diag(ub[p]), vbd[p]], axis=0)) for p in P]
    for p in P:
        ga = g_all[rows[p], sls[p]]
        g_col = jnp.where(left, jnp.sum(jnp.where(row2 == lane2, ga, 0.0), axis=1, keepdims=True),
                          jnp.sum(jnp.where(row2 == lane2 - N, ga, 0.0), axis=1, keepdims=True))
        res = _dot_tn(jnp.concatenate([bh[p], kh[p]], axis=0), jnp.concatenate([ub[p], vb[p]], axis=0))
        h_scr[p] = g_col * hst[p] + jnp.where(left, res[0:N], res[N:2 * N])

    for p in P:
        rs, sl = rows[p], sls[p]
        mean = per_head(y[p], jnp.sum) * (1.0 / N)
        yc = y[p] - mean
        var = per_head(yc * yc, jnp.sum) * (1.0 / N)
        yn = yc * lax.rsqrt(var + GN_EPS) * lng_ref[:, sl] + lnb_ref[:, sl]
        bonus = per_head(rkr[rs, sl], jnp.sum) * v[rs, sl]
        y_ref[p // (H_RWKV // 2), :, sl] = (yn + bonus) * g[rs, sl]


def _rwkv_call(rw, mu_p, w0, wdec, a0, waaa, wgate_p, k_k, k_a, r_k, lng, lnb):
    B, T, _ = rw.shape
    NC = T // CHUNK
    W = RWKV_WIDTH
    c2 = lambda b, c: (0, 0)
    rowspec = pl.BlockSpec((1, W), c2)
    NS = next(n for n in (4, 2, 1) if B % n == 0)
    return pl.pallas_call(
        _rwkv_kernel,
        grid=(B // NS, NC),
        in_specs=[pl.BlockSpec((NS, CHUNK, RW_COLS), lambda b, c: (b, c, 0)),
                  pl.BlockSpec((NS, 8, RW_COLS), lambda b, c: (b, jnp.maximum(c * (CHUNK // 8) - 1, 0), 0)),
                  pl.BlockSpec((1, RW_COLS), c2),
                  rowspec,
                  pl.BlockSpec((DECAY_LORA, W), c2),
                  rowspec,
                  pl.BlockSpec((AAA_LORA, W), c2),
                  pl.BlockSpec((GATE_PAD, W), c2),
                  rowspec, rowspec, rowspec, rowspec, rowspec],
        out_specs=pl.BlockSpec((NS, CHUNK, W), lambda b, c: (b, c, 0)),
        out_shape=jax.ShapeDtypeStruct((B, T, W), F32),
        scratch_shapes=[pltpu.VMEM((NS * H_RWKV // 2, HD_RWKV, 2 * HD_RWKV), F32)],
        compiler_params=_cp(("parallel", "arbitrary")),
        name="rwkv",
    )(rw, rw, mu_p, w0, wdec, a0, waaa, wgate_p, k_k, k_a, r_k, lng, lnb)


def _ffn_kernel(x_ref, ya_ref, yr_ref, mod_ref, gpost_ref, gpre_ref, gfpost_ref,
                wo_ref, wg_ref, wu_ref, wd_ref, o_ref, x1_scr, hf_scr, acc_scr):
    j = pl.program_id(2)

    @pl.when(j == 0)
    def _():
        for rows in _row_chunks(x_ref.shape[1]):
            mix = (_dot(ya_ref[0, rows, :].astype(BF16), wo_ref[0:ATT_WIDTH, :])
                   + _dot(yr_ref[0, rows, :].astype(BF16), wo_ref[ATT_WIDTH:, :]))
            x1 = x_ref[0, rows, :] + mod_ref[0, 2:3, :] * _rms(mix, gpost_ref[...])
            x1_scr[rows, :] = x1
            hf = _rms(x1, gpre_ref[...]) * (1.0 + mod_ref[0, 4:5, :]) + mod_ref[0, 3:4, :]
            hf_scr[rows, :] = hf.astype(BF16)
        acc_scr[...] = jnp.zeros(acc_scr.shape, F32)

    hf = hf_scr[...]
    gate = _dot(hf, wg_ref[...])
    up = _dot(hf, wu_ref[...])
    act = (gate * jax.nn.sigmoid(gate) * up).astype(BF16)
    acc_scr[...] += _dot(act, wd_ref[...])

    @pl.when(j == pl.num_programs(2) - 1)
    def _():
        o_ref[0] = x1_scr[...] + mod_ref[0, 5:6, :] * _rms(acc_scr[...], gfpost_ref[...])


def _ffn_call(x, y_att, y_rwkv, mod3, g_post, g_pre, g_fpost, wo, wg, wu, wd):
    B, T, D = x.shape
    TM = 512
    TF = D_FF // 2
    c3 = lambda b, i, j: (0, 0)
    return pl.pallas_call(
        _ffn_kernel,
        grid=(B, T // TM, D_FF // TF),
        in_specs=[pl.BlockSpec((1, TM, D), lambda b, i, j: (b, i, 0)),
                  pl.BlockSpec((1, TM, ATT_WIDTH), lambda b, i, j: (b, i, 0)),
                  pl.BlockSpec((1, TM, RWKV_WIDTH), lambda b, i, j: (b, i, 0)),
                  pl.BlockSpec((1, 6, D), lambda b, i, j: (b, 0, 0)),
                  pl.BlockSpec((1, D), c3), pl.BlockSpec((1, D), c3), pl.BlockSpec((1, D), c3),
                  pl.BlockSpec((D, D), c3),
                  pl.BlockSpec((D, TF), lambda b, i, j: (0, j)),
                  pl.BlockSpec((D, TF), lambda b, i, j: (0, j)),
                  pl.BlockSpec((TF, D), lambda b, i, j: (j, 0))],
        out_specs=pl.BlockSpec((1, TM, D), lambda b, i, j: (b, i, 0)),
        out_shape=jax.ShapeDtypeStruct((B, T, D), F32),
        scratch_shapes=[pltpu.VMEM((TM, D), F32), pltpu.VMEM((TM, D), BF16), pltpu.VMEM((TM, D), F32)],
        compiler_params=_cp(("parallel", "parallel", "arbitrary")),
        name="ffn",
    )(x, y_att, y_rwkv, mod3, g_post, g_pre, g_fpost, wo, wg, wu, wd)


def _pad_cols(w, n):
    return jnp.pad(w, ((0, 0), (0, n - w.shape[1])))


def _layer(x, mod3, rel_bias, mix_pre_norm, mix_post_norm, ffn_pre_norm, ffn_post_norm,
           w_in, q_norm, w_uq, w_idx_q, kv_norm, idx_k_norm, w_uk, w_uv,
           mu_shift, w0, w_decay_up, a0, w_aaa_up, w_gate_up, k_k, k_a, r_k, ln_x_gain, ln_x_bias,
           w_out, w_ffn_gate, w_ffn_up, w_ffn_down):
    B, T, D = x.shape
    att_cols = Q_LORA + KV_LATENT + IDX_DIM + IDX_HEADS
    n_main = 3 * RWKV_WIDTH + DECAY_LORA + AAA_LORA
    w_att = _pad_cols(w_in[:, :att_cols], ATT_PAD)
    w_rw = w_in[:, att_cols:]
    w_in_p = jnp.concatenate([w_att, w_rw[:, :n_main], _pad_cols(w_rw[:, n_main:], GATE_PAD)], axis=1).astype(BF16)
    mu_p = jnp.concatenate([mu_shift[:n_main], jnp.pad(mu_shift[n_main:], (0, GATE_PAD - GATE_LORA))]).reshape(1, RW_COLS)
    wgate_p = jnp.pad(w_gate_up, ((0, GATE_PAD - GATE_LORA), (0, 0))).astype(BF16)

    att, rw = _inproj_call(x, mod3, mix_pre_norm.reshape(1, D), w_in_p)

    wuq = w_uq.reshape(Q_LORA, ATT_WIDTH).astype(BF16)
    wiq = w_idx_q.reshape(Q_LORA, IDX_HEADS * IDX_DIM).astype(BF16)
    wukt = jnp.transpose(w_uk, (0, 2, 1)).astype(BF16)
    qabs, iq, ik, ckv, ckvt, iwt = _dsaprep_call(att, q_norm.reshape(1, -1), kv_norm.reshape(1, -1),
                                                 idx_k_norm.reshape(1, -1), wuq, wiq, wukt)
    bias_tiles = _bias_call(rel_bias)
    wuvt = jnp.transpose(w_uv, (0, 2, 1)).astype(BF16)
    topk = min(TOPK_MAX, T // 4)
    y_att = _dsa_call(iq, iwt, qabs, ik, ckv, ckvt, bias_tiles, wuvt, topk)

    row = lambda z: z.reshape(1, RWKV_WIDTH)
    y_rwkv = _rwkv_call(rw, mu_p, row(w0), w_decay_up.astype(BF16), row(a0), w_aaa_up.astype(BF16),
                        wgate_p, row(k_k), row(k_a), row(r_k), row(ln_x_gain), row(ln_x_bias))

    return _ffn_call(x, y_att, y_rwkv, mod3, mix_post_norm.reshape(1, D), ffn_pre_norm.reshape(1, D),
                     ffn_post_norm.reshape(1, D), w_out.astype(BF16), w_ffn_gate.astype(BF16),
                     w_ffn_up.astype(BF16), w_ffn_down.astype(BF16))


def kernel(x, c, rel_bias, ada_w, ada_b, mix_pre_norm, mix_post_norm, ffn_pre_norm, ffn_post_norm, w_in, q_norm, w_uq, w_idx_q, kv_norm, idx_k_norm, w_uk, w_uv, mu_shift, w0, w_decay_up, a0, w_aaa_up, w_gate_up, k_k, k_a, r_k, ln_x_gain, ln_x_bias, w_out, w_ffn_gate, w_ffn_up, w_ffn_down):
    B, T, D = x.shape
    assert D == D_MODEL and T % 512 == 0 and T // 4 >= 1
    layer_params = (mix_pre_norm, mix_post_norm, ffn_pre_norm, ffn_post_norm,
                    w_in, q_norm, w_uq, w_idx_q, kv_norm, idx_k_norm, w_uk, w_uv,
                    mu_shift, w0, w_decay_up, a0, w_aaa_up, w_gate_up, k_k, k_a, r_k, ln_x_gain, ln_x_bias,
                    w_out, w_ffn_gate, w_ffn_up, w_ffn_down)
    for l in range(ada_w.shape[0]):
        mod3 = _mod_call(c, ada_w[l], ada_b[l]).reshape(B, 6, D)
        x = _layer(x, mod3, rel_bias, *[p[l] for p in layer_params])
    return x
```

```python
import functools
import math

import numpy as np
import jax
import jax.numpy as jnp
from jax import lax
from jax.experimental import pallas as pl
from jax.experimental.pallas import tpu as pltpu

F32 = jnp.float32
BF16 = jnp.bfloat16

D_MODEL = 1024
ATT_WIDTH = 512
RWKV_WIDTH = 512
HD_ATT = 64
H_ATT = 8
HD_RWKV = 64
H_RWKV = 8
Q_LORA = 256
KV_LATENT = 128
IDX_HEADS = 8
IDX_DIM = 64
TOPK_MAX = 256
NUM_BUCKETS = 32
MAX_DISTANCE = 128
DECAY_LORA = 64
AAA_LORA = 64
GATE_LORA = 160
GN_EPS = 64e-5
D_FF = 2816
NORM_EPS = 1e-6

ATT_PAD = 512
GATE_PAD = 256
RW_COLS = 3 * RWKV_WIDTH + DECAY_LORA + AAA_LORA + GATE_PAD
OFF_WD = 3 * RWKV_WIDTH
OFF_AD = OFF_WD + DECAY_LORA
OFF_GD = OFF_AD + AAA_LORA

TQ = 128
CHUNK = 64
NEG_BIG = -1e30
LOWEST = -3.0e38
LOG2E = math.log2(math.e)
VMEM_LIMIT = 56 * 1024 * 1024


def _cp(sem):
    return pltpu.CompilerParams(dimension_semantics=sem, vmem_limit_bytes=VMEM_LIMIT)


def _dot(a, b):
    return jnp.dot(a, b, preferred_element_type=F32)


def _dot_nt(a, b):
    return lax.dot_general(a, b, (((1,), (1,)), ((), ())), preferred_element_type=F32)


def _dot_tn(a, b):
    return lax.dot_general(a, b, (((0,), (0,)), ((), ())), preferred_element_type=F32)


def _rms(z, gain):
    return z * lax.rsqrt(jnp.mean(z * z, axis=-1, keepdims=True) + NORM_EPS) * gain


def _row_chunks(n, chunk=256):
    return [slice(r, r + chunk) for r in range(0, n, chunk)]


def _mod_kernel(c_ref, w_ref, b_ref, o_ref):
    c = c_ref[...]
    s = c * jax.nn.sigmoid(c)
    o_ref[...] = _dot(s.astype(BF16), w_ref[...].astype(BF16)) + b_ref[...]


def _mod_call(c, ada_w, ada_b):
    B, D = c.shape
    N = ada_w.shape[1]
    TN = 1536
    return pl.pallas_call(
        _mod_kernel,
        grid=(N // TN,),
        in_specs=[pl.BlockSpec((B, D), lambda j: (0, 0)),
                  pl.BlockSpec((D, TN), lambda j: (0, j)),
                  pl.BlockSpec((1, TN), lambda j: (0, j))],
        out_specs=pl.BlockSpec((B, TN), lambda j: (0, j)),
        out_shape=jax.ShapeDtypeStruct((B, N), F32),
        compiler_params=_cp(("parallel",)),
        name="mod",
    )(c, ada_w, ada_b.reshape(1, N))


def _inproj_kernel(x_ref, mod_ref, g_ref, w_ref, att_ref, rw_ref):
    sh = mod_ref[0, 0:1, :]
    sc = mod_ref[0, 1:2, :]
    h = (_rms(x_ref[0], g_ref[...]) * (1.0 + sc) + sh).astype(BF16)
    att_ref[0] = _dot(h, w_ref[:, 0:ATT_PAD])
    for n0 in range(0, RW_COLS, 384):
        rw_ref[0, :, n0:n0 + 384] = _dot(h, w_ref[:, ATT_PAD + n0:ATT_PAD + n0 + 384])


def _inproj_call(x, mod3, gain, w_in_p):
    B, T, D = x.shape
    TM = 512
    NP = w_in_p.shape[1]
    return pl.pallas_call(
        _inproj_kernel,
        grid=(B, T // TM),
        in_specs=[pl.BlockSpec((1, TM, D), lambda b, i: (b, i, 0)),
                  pl.BlockSpec((1, 6, D), lambda b, i: (b, 0, 0)),
                  pl.BlockSpec((1, D), lambda b, i: (0, 0)),
                  pl.BlockSpec((D, NP), lambda b, i: (0, 0))],
        out_specs=[pl.BlockSpec((1, TM, ATT_PAD), lambda b, i: (b, i, 0)),
                   pl.BlockSpec((1, TM, RW_COLS), lambda b, i: (b, i, 0))],
        out_shape=[jax.ShapeDtypeStruct((B, T, ATT_PAD), F32),
                   jax.ShapeDtypeStruct((B, T, RW_COLS), F32)],
        compiler_params=_cp(("parallel", "parallel")),
        name="inproj",
    )(x, mod3, gain, w_in_p)


def _dsaprep_kernel(att_ref, qn_ref, kvn_ref, ikn_ref, wuq_ref, wiq_ref, wukt_ref,
                    qabs_ref, iq_ref, ik_ref, ckv_ref, ckvt_ref, iwt_ref):
    att = att_ref[0]
    tm = att.shape[0]
    o1 = Q_LORA
    o2 = o1 + KV_LATENT
    o3 = o2 + IDX_DIM
    cq = _rms(att[:, 0:o1], qn_ref[...]).astype(BF16)
    ckv = _rms(att[:, o1:o2], kvn_ref[...])
    ckv_ref[0] = ckv.astype(BF16)
    ik_ref[0] = _rms(att[:, o2:o3], ikn_ref[...]).astype(BF16)
    tail = att[:, o2:ATT_PAD]
    w_off = o3 - o2
    for sb in range(tm // TQ):
        rows = slice(sb * TQ, (sb + 1) * TQ)
        ckvt_ref[0, sb] = ckv[rows, :].T.astype(BF16)
        iwt_ref[0, :, rows] = tail[rows, :].T[w_off:w_off + IDX_HEADS, :] * (IDX_HEADS ** -0.5 * IDX_DIM ** -0.5)
    q = _dot(cq, wuq_ref[...])
    iqv = _dot(cq, wiq_ref[...])
    for h in range(H_ATT):
        qh = q[:, h * HD_ATT:(h + 1) * HD_ATT].astype(BF16)
        qa = _dot(qh, wukt_ref[h]) * (HD_ATT ** -0.5 * LOG2E)
        for sb in range(tm // TQ):
            qabs_ref[0, sb, :, h * TQ:(h + 1) * TQ] = qa[sb * TQ:(sb + 1) * TQ, :].T.astype(BF16)
    per = TQ // IDX_DIM
    for g in range(IDX_HEADS // per):
        for sb in range(tm // TQ):
            blk = iqv[sb * TQ:(sb + 1) * TQ, g * TQ:(g + 1) * TQ].T.astype(BF16)
            for u in range(per):
                h = g * per + u
                iq_ref[0, sb, :, h * TQ:(h + 1) * TQ] = blk[u * IDX_DIM:(u + 1) * IDX_DIM, :]


def _dsaprep_call(att, q_norm, kv_norm, idx_k_norm, wuq, wiq, wukt):
    B, T, _ = att.shape
    TM = 512
    const2 = lambda b, i: (0, 0)
    return pl.pallas_call(
        _dsaprep_kernel,
        grid=(B, T // TM),
        in_specs=[pl.BlockSpec((1, TM, ATT_PAD), lambda b, i: (b, i, 0)),
                  pl.BlockSpec((1, Q_LORA), const2),
                  pl.BlockSpec((1, KV_LATENT), const2),
                  pl.BlockSpec((1, IDX_DIM), const2),
                  pl.BlockSpec((Q_LORA, ATT_WIDTH), const2),
                  pl.BlockSpec((Q_LORA, IDX_HEADS * IDX_DIM), const2),
                  pl.BlockSpec((H_ATT, HD_ATT, KV_LATENT), lambda b, i: (0, 0, 0))],
        out_specs=[pl.BlockSpec((1, TM // TQ, KV_LATENT, H_ATT * TQ), lambda b, i: (b, i, 0, 0)),
                   pl.BlockSpec((1, TM // TQ, IDX_DIM, IDX_HEADS * TQ), lambda b, i: (b, i, 0, 0)),
                   pl.BlockSpec((1, TM, IDX_DIM), lambda b, i: (b, i, 0)),
                   pl.BlockSpec((1, TM, KV_LATENT), lambda b, i: (b, i, 0)),
                   pl.BlockSpec((1, TM // TQ, KV_LATENT, TQ), lambda b, i: (b, i, 0, 0)),
                   pl.BlockSpec((1, IDX_HEADS, TM), lambda b, i: (b, 0, i))],
        out_shape=[jax.ShapeDtypeStruct((B, T // TQ, KV_LATENT, H_ATT * TQ), BF16),
                   jax.ShapeDtypeStruct((B, T // TQ, IDX_DIM, IDX_HEADS * TQ), BF16),
                   jax.ShapeDtypeStruct((B, T, IDX_DIM), BF16),
                   jax.ShapeDtypeStruct((B, T, KV_LATENT), BF16),
                   jax.ShapeDtypeStruct((B, T // TQ, KV_LATENT, TQ), BF16),
                   jax.ShapeDtypeStruct((B, IDX_HEADS, T), F32)],
        compiler_params=_cp(("parallel", "parallel")),
        name="dsaprep",
    )(att, q_norm, kv_norm, idx_k_norm, wuq, wiq, wukt)


def _t5_bucket_np(rel):
    max_exact = NUM_BUCKETS // 2
    nf = np.maximum(rel, 1).astype(np.float32)
    large = max_exact + (np.log(nf / max_exact) / math.log(MAX_DISTANCE / max_exact)
                         * (NUM_BUCKETS - max_exact)).astype(np.int32)
    large = np.minimum(large, NUM_BUCKETS - 1)
    return np.where(rel < max_exact, rel, large).astype(np.int32)


def _near_bucket_tiles():
    s = np.arange(TQ)[:, None]
    t = np.arange(TQ)[None, :]
    tiles = [_t5_bucket_np(np.maximum(t - s + TQ * d, 0)) for d in range(2)]
    assert _t5_bucket_np(np.array([TQ + 1]))[0] == NUM_BUCKETS - 1
    return np.stack(tiles)


def _bias_kernel(rb_ref, bk_ref, o_ref):
    for d in range(2):
        bk = bk_ref[d]
        for h in range(H_ATT):
            def body(b, acc):
                return jnp.where(bk == b, rb_ref[b, h], acc)
            acc = lax.fori_loop(0, NUM_BUCKETS, body, jnp.zeros((TQ, TQ), F32))
            o_ref[d, h] = (acc - rb_ref[NUM_BUCKETS - 1, h]) * LOG2E
    o_ref[2] = jnp.zeros((H_ATT, TQ, TQ), F32)


def _bias_call(rel_bias):
    buckets = jnp.asarray(_near_bucket_tiles())
    return pl.pallas_call(
        _bias_kernel,
        in_specs=[pl.BlockSpec(memory_space=pltpu.SMEM),
                  pl.BlockSpec(memory_space=pltpu.VMEM)],
        out_specs=pl.BlockSpec(memory_space=pltpu.VMEM),
        out_shape=jax.ShapeDtypeStruct((3, H_ATT, TQ, TQ), F32),
        name="bias",
    )(rel_bias, buckets)


KV = TQ // 8
FREE_PROBES = 11
COUNT_PROBES = 6


def _key_reduce(x, op):
    return op(op(x, axis=0), axis=0, keepdims=True)


def _dsa_kernel(iq_ref, iwt_ref, qabs_ref, ik_ref, ckv_ref, ckvt_ref, bias_ref, wuvt_ref, y_ref,
                score_scr, stage_a, stage_b, pv_a, pv_b, alpha_a, alpha_b, acc_scr, *, topk):
    i = pl.program_id(1)
    kf = float(topk)
    kidx = (lax.broadcasted_iota(jnp.int32, (KV, 8, TQ), 0) * 8
            + lax.broadcasted_iota(jnp.int32, (KV, 8, TQ), 1))
    qidx = lax.broadcasted_iota(jnp.int32, (KV, 8, TQ), 2)
    n_pairs = i // 2 + 1

    iwt = iwt_ref[0]
    w8 = [jnp.broadcast_to(iwt[h:h + 1, :], (8, TQ))[None] for h in range(IDX_HEADS)]
    iq2 = iq_ref[0, 0]

    n_trips = (n_pairs + 1) // 2
    stage = (stage_a, stage_b)

    def stage_pair(q, lhs_ref, rhs, dst):
        nxt = jnp.minimum(q, n_pairs - 1)
        rows = lhs_ref[0, pl.ds(pl.multiple_of(2 * nxt * TQ, 2 * TQ), 2 * TQ), :]
        dst[...] = _dot(rows, rhs).reshape(2, TQ, H_ATT * TQ)

    def two_halves(half):
        def trip(r, carry):
            carry = half(2 * r, 0, carry)
            return lax.cond(2 * r + 1 < n_pairs, lambda c: half(2 * r + 1, 1, c), lambda c: c, carry)
        return trip

    def score_half(q, par, carry):
        rmax, rmin = carry
        stage_pair(q + 1, ik_ref, iq2, stage[1 - par])
        for blk in range(2):
            j = 2 * q + blk
            sc = jnp.zeros((KV, 8, TQ), F32)
            for h in range(IDX_HEADS):
                s = stage[par][blk, :, h * TQ:(h + 1) * TQ].reshape(KV, 8, TQ)
                sc = sc + jnp.maximum(s, 0.0) * w8[h]
            causal = kidx + (j - i) * TQ <= qidx
            score_scr[j] = jnp.where(causal, sc, -jnp.inf)
            rmax = jnp.maximum(rmax, jnp.max(jnp.where(causal, sc, -jnp.inf), axis=0))
            rmin = jnp.minimum(rmin, jnp.min(jnp.where(causal, sc, jnp.inf), axis=0))
        return rmax, rmin

    stage_pair(0, ik_ref, iq2, stage[0])
    rmax, rmin = lax.fori_loop(0, n_trips, two_halves(score_half),
                               (jnp.full((8, TQ), -jnp.inf, F32), jnp.full((8, TQ), jnp.inf, F32)))
    smax = jnp.max(rmax, axis=0, keepdims=True)
    smin = jnp.min(rmin, axis=0, keepdims=True)

    nvalid = (i * TQ + lax.broadcasted_iota(jnp.int32, (1, TQ), 1) + 1).astype(F32)
    needs_thr = nvalid > kf

    def probe(v):
        vb = jnp.broadcast_to(v, (8, TQ))[None, None]

        def body(q, st):
            cnt, dn, up = st
            for j in (2 * q, 2 * q + 1):
                s = score_scr[j].reshape(4, KV // 4, 8, TQ)
                ge = s >= vb
                cnt = cnt + jnp.sum(jnp.where(ge, 1.0, 0.0), axis=1)
                dn = jnp.minimum(dn, jnp.min(jnp.where(ge, s, jnp.inf), axis=1))
                up = jnp.maximum(up, jnp.max(jnp.where(ge, -jnp.inf, s), axis=1))
            return cnt, dn, up
        cnt, dn, up = lax.fori_loop(0, n_pairs, body, (jnp.zeros((4, 8, TQ), F32),
                                                       jnp.full((4, 8, TQ), jnp.inf, F32),
                                                       jnp.full((4, 8, TQ), -jnp.inf, F32)))
        return (jnp.sum(jnp.sum(cnt, axis=0), axis=0, keepdims=True),
                jnp.min(jnp.min(dn, axis=0), axis=0, keepdims=True),
                jnp.max(jnp.max(up, axis=0), axis=0, keepdims=True))

    def probe_count(v):
        vb = jnp.broadcast_to(v, (8, TQ))[None, None]

        def body(q, cnt):
            for j in (2 * q, 2 * q + 1):
                s = score_scr[j].reshape(4, KV // 4, 8, TQ)
                cnt = cnt + jnp.sum(jnp.where(s >= vb, 1.0, 0.0), axis=1)
            return cnt
        cnt = lax.fori_loop(0, n_pairs, body, jnp.zeros((4, 8, TQ), F32))
        return jnp.sum(jnp.sum(cnt, axis=0), axis=0, keepdims=True)

    def logit(c):
        return jnp.log((c + 0.5) / (nvalid - c + 0.5))

    g_k = logit(jnp.full((1, TQ), kf - 0.5, F32))

    def search_step(st, snap=True):
        lo, hi, clo, ghi, slo, shi = st
        is_open = lo < hi
        g_lo = logit(clo)
        g_hi = logit(ghi)
        phi = jnp.clip((g_lo - g_k) / (g_lo - g_hi), 0.02, 0.98)
        stalled = (slo >= 2.0) | (shi >= 2.0)
        mid = jnp.where(stalled, 0.5 * lo + 0.5 * hi, lo + (hi - lo) * phi)
        mid = jnp.where((mid > lo) & (mid <= hi), mid, hi)
        if snap:
            c, dn, up = probe(mid)
        else:
            c = probe_count(mid)
            dn = up = mid
        ge = c >= kf
        move_lo = is_open & ge
        move_hi = is_open & jnp.logical_not(ge)
        lo = jnp.where(move_lo, dn, lo)
        clo = jnp.where(move_lo, c, clo)
        hi = jnp.where(move_hi, up, hi)
        ghi = jnp.where(move_hi, c, ghi)
        slo = jnp.where(stalled | ge, 0.0, slo + 1.0)
        shi = jnp.where(stalled | jnp.logical_not(ge), 0.0, shi + 1.0)
        return lo, hi, clo, ghi, slo, shi

    def n_open(st):
        return jnp.sum(jnp.where(st[0] < st[1], 1.0, 0.0))

    def checked_step(carry):
        st = search_step(carry[0])
        return st, n_open(st), carry[2] + 1

    search_cap = ik_ref.shape[1] + 2
    zero = jnp.zeros((1, TQ), F32)
    st = (smin, jnp.where(needs_thr, smax, smin), nvalid, zero, zero, zero)
    st = lax.fori_loop(0, COUNT_PROBES, lambda _, s: search_step(s, snap=False), st)
    st = lax.fori_loop(COUNT_PROBES, FREE_PROBES, lambda _, s: search_step(s), st)
    st, _, _ = lax.while_loop(lambda c: jnp.logical_and(c[1] > 0.0, c[2] < search_cap), checked_step,
                              (st, n_open(st), jnp.int32(0)))
    thr, _, clo, chi, _, _ = st

    tie = needs_thr & (clo > kf)
    n_tie = jnp.sum(jnp.where(tie, 1.0, 0.0))

    @pl.when(n_tie > 0.0)
    def _():
        r2 = lax.broadcasted_iota(jnp.int32, (TQ, TQ), 0)
        c2 = lax.broadcasted_iota(jnp.int32, (TQ, TQ), 1)
        lower = jnp.where(c2 <= r2, 1.0, 0.0).astype(BF16)
        thrb = jnp.broadcast_to(thr, (TQ, TQ))
        tieb = jnp.broadcast_to(tie, (TQ, TQ))
        needb = jnp.broadcast_to(kf - chi, (TQ, TQ))

        def body(j, carry):
            s = score_scr[j].reshape(TQ, TQ)
            eq = (s == thrb) & tieb
            pref = _dot(lower, jnp.where(eq, 1.0, 0.0).astype(BF16)) + carry
            score_scr[j] = jnp.where(eq & (pref > needb), -jnp.inf, s).reshape(KV, 8, TQ)
            return jnp.broadcast_to(pref[TQ - 1:TQ, :], (TQ, TQ))
        lax.fori_loop(0, i + 1, body, jnp.zeros((TQ, TQ), F32))

    thr_eff = jnp.broadcast_to(jnp.where(needs_thr, thr, LOWEST), (8, TQ))[None]

    acc_scr[...] = jnp.zeros(acc_scr.shape, F32)
    qa2 = qabs_ref[0, 0]

    pv = (pv_a, pv_b)
    alpha_st = (alpha_a, alpha_b)

    def apply_pending(par):
        acc_scr[...] = alpha_st[par][...] * acc_scr[...] + pv[par][...]

    def attend_variant(q, par, carry, near):
        m = list(carry)
        stage_pair(q + 1, ckv_ref, qa2, stage[1 - par])
        js = (2 * q, 2 * q + 1)
        sel = [score_scr[j] >= thr_eff for j in js]
        d = [jnp.clip(i - j, 0, 2) for j in js]
        kvt = jnp.concatenate([jnp.concatenate([ckvt_ref[0, js[0]], ckvt_ref[0, js[1]]], axis=1),
                               jnp.ones((8, 2 * TQ), BF16)], axis=0)
        ps, alphas = [], []
        for h in range(H_ATT):
            lg = [stage[par][b, :, h * TQ:(h + 1) * TQ] for b in range(2)]
            if near:
                lg = [lg[b] + bias_ref[d[b], h] for b in range(2)]
            lg = [jnp.where(sel[b], lg[b].reshape(KV, 8, TQ), NEG_BIG) for b in range(2)]
            m_new = jnp.maximum(m[h], jnp.maximum(_key_reduce(lg[0], jnp.max), _key_reduce(lg[1], jnp.max)))
            alpha = jnp.exp2(m[h] - m_new)
            mb = jnp.broadcast_to(m_new, (8, TQ))[None]
            p = [jnp.exp2(lg[b] - mb) for b in range(2)]
            m[h] = m_new
            alphas.append(alpha)
            ps.append(jnp.concatenate([p[0].reshape(TQ, TQ), p[1].reshape(TQ, TQ)], axis=0).astype(BF16))
            if h % 2 == 1:
                cols = slice((h - 1) * TQ, (h + 1) * TQ)
                pv[par][:, cols] = _dot(kvt, jnp.concatenate(ps[h - 1:h + 1], axis=1))
        alpha_st[par][...] = jnp.concatenate(alphas, axis=1)
        apply_pending(1 - par)
        return tuple(m)

    def attend_half(q, par, carry):
        return lax.cond(2 * q + 1 >= i - 1,
                        lambda c: attend_variant(q, par, c, True),
                        lambda c: attend_variant(q, par, c, False), carry)

    pv_b[...] = jnp.zeros(pv_b.shape, F32)
    alpha_b[...] = jnp.ones(alpha_b.shape, F32)
    stage_pair(0, ckv_ref, qa2, stage[0])
    m0 = tuple(jnp.full((1, TQ), NEG_BIG, F32) for _ in range(H_ATT))
    lax.fori_loop(0, n_trips, two_halves(attend_half), m0)
    for par in range(2):
        @pl.when((n_pairs - 1) % 2 == par)
        def _():
            apply_pending(par)

    yt = []
    for h in range(H_ATT):
        cols = slice(h * TQ, (h + 1) * TQ)
        o = acc_scr[0:KV_LATENT, cols] / acc_scr[KV_LATENT:KV_LATENT + 1, cols]
        yt.append(_dot(wuvt_ref[h], o.astype(BF16)))
    for g in range(ATT_WIDTH // TQ):
        per = TQ // HD_ATT
        y_ref[0, :, g * TQ:(g + 1) * TQ] = jnp.concatenate(yt[g * per:(g + 1) * per], axis=0).T


def _dsa_call(iq, iwt, qabs, ik, ckv, ckvt, bias_tiles, wuvt, topk):
    B, T, _ = ik.shape
    NT = T // TQ
    return pl.pallas_call(
        functools.partial(_dsa_kernel, topk=topk),
        grid=(B, NT),
        in_specs=[pl.BlockSpec((1, 1, IDX_DIM, IDX_HEADS * TQ), lambda b, i: (b, i, 0, 0)),
                  pl.BlockSpec((1, IDX_HEADS, TQ), lambda b, i: (b, 0, i)),
                  pl.BlockSpec((1, 1, KV_LATENT, H_ATT * TQ), lambda b, i: (b, i, 0, 0)),
                  pl.BlockSpec((1, T, IDX_DIM), lambda b, i: (b, 0, 0)),
                  pl.BlockSpec((1, T, KV_LATENT), lambda b, i: (b, 0, 0)),
                  pl.BlockSpec((1, NT, KV_LATENT, TQ), lambda b, i: (b, 0, 0, 0)),
                  pl.BlockSpec((3, H_ATT, TQ, TQ), lambda b, i: (0, 0, 0, 0)),
                  pl.BlockSpec((H_ATT, HD_ATT, KV_LATENT), lambda b, i: (0, 0, 0))],
        out_specs=pl.BlockSpec((1, TQ, ATT_WIDTH), lambda b, i: (b, i, 0)),
        out_shape=jax.ShapeDtypeStruct((B, T, ATT_WIDTH), F32),
        scratch_shapes=[pltpu.VMEM((NT, KV, 8, TQ), F32),
                        pltpu.VMEM((2, TQ, H_ATT * TQ), F32),
                        pltpu.VMEM((2, TQ, H_ATT * TQ), F32),
                        pltpu.VMEM((KV_LATENT + 8, H_ATT * TQ), F32),
                        pltpu.VMEM((KV_LATENT + 8, H_ATT * TQ), F32),
                        pltpu.VMEM((1, H_ATT * TQ), F32),
                        pltpu.VMEM((1, H_ATT * TQ), F32),
                        pltpu.VMEM((KV_LATENT + 8, H_ATT * TQ), F32)],
        compiler_params=_cp(("parallel", "arbitrary")),
        name="dsa",
    )(iq, iwt, qabs, ik, ckv, ckvt, bias_tiles, wuvt)


def _rwkv_kernel(p_ref, prev_ref, mu_ref, w0_ref, wdec_ref, a0_ref, waaa_ref, wgate_ref,
                 kk_ref, ka_ref, rk_ref, lng_ref, lnb_ref, y_ref, h_scr):
    c = pl.program_id(1)
    C = CHUNK
    N = HD_RWKV

    @pl.when(c == 0)
    def _():
        h_scr[...] = jnp.zeros(h_scr.shape, F32)

    NS = p_ref.shape[0]
    p = p_ref[...].reshape(NS * C, RW_COLS)
    rid = lax.broadcasted_iota(jnp.int32, (NS * C, 1), 0)
    p_prev = pltpu.roll(p, 1, 0)
    for s in range(NS):
        p_prev = jnp.where(rid == s * C, jnp.where(c > 0, prev_ref[s, 7:8, :], 0.0), p_prev)
    p = p + mu_ref[...] * (p_prev - p)

    W = RWKV_WIDTH
    r = p[:, 0:W]
    k = p[:, W:2 * W]
    v = p[:, 2 * W:3 * W]
    wd = p[:, OFF_WD:OFF_WD + DECAY_LORA]
    ad = p[:, OFF_AD:OFF_AD + AAA_LORA]
    gd = p[:, OFF_GD:OFF_GD + GATE_PAD]

    w_log = -jax.nn.softplus(-(w0_ref[...] + _dot(jnp.tanh(wd).astype(BF16), wdec_ref[...]))) - 0.5
    logw = -jnp.exp(w_log)
    a = jax.nn.sigmoid(a0_ref[...] + _dot(ad.astype(BF16), waaa_ref[...]))
    g = _dot(jax.nn.sigmoid(gd).astype(BF16), wgate_ref[...])

    ra = lax.broadcasted_iota(jnp.int32, (NS * C, NS * C), 0)
    ca = lax.broadcasted_iota(jnp.int32, (NS * C, NS * C), 1)
    same_seq = (ra >= ca) & (ra - ca <= ra % C)
    cum = jnp.dot(jnp.where(same_seq, 1.0, 0.0), logw, preferred_element_type=F32,
                  precision=lax.Precision.HIGHEST)
    cum_last = cum[C - 1:C, :]
    for s in range(1, NS):
        cum_last = jnp.where(rid >= s * C, cum[(s + 1) * C - 1:(s + 1) * C, :], cum_last)
    g_in = jnp.exp(cum - logw)
    g_t = jnp.exp(cum)
    g_inv = jnp.exp(-cum)
    g_end = jnp.exp(cum_last - cum)
    g_all = jnp.exp(cum_last)

    kkf = k * kk_ref[...]
    kmod = k * (1.0 + (a - 1.0) * ka_ref[...])
    rkr = r * kmod * rk_ref[...]

    P = range(NS * H_RWKV // 2)
    rows = [slice((p // (H_RWKV // 2)) * C, (p // (H_RWKV // 2) + 1) * C) for p in P]
    sls = [slice((p % (H_RWKV // 2)) * 2 * N, (p % (H_RWKV // 2) + 1) * 2 * N) for p in P]
    lane2 = lax.broadcasted_iota(jnp.int32, (C, 2 * N), 1)
    row2 = lax.broadcasted_iota(jnp.int32, (C, 2 * N), 0)
    left = lane2 < N
    col2 = jnp.where(left, lane2, lane2 - N)
    strict2 = row2 > col2
    incl2 = row2 >= col2
    eye2 = row2 == col2

    def per_head(x, op):
        return jnp.where(left, op(jnp.where(left, x, 0.0), axis=-1, keepdims=True),
                         op(jnp.where(left, 0.0, x), axis=-1, keepdims=True))

    def block_diag(x):
        zero = jnp.zeros_like(x)
        return jnp.concatenate([jnp.where(left, x, zero), jnp.where(left, zero, x)], axis=0)

    at, rt, bh, kh, vb, g4 = [], [], [], [], [], []
    for p in P:
        rs, sl = rows[p], sls[p]
        kk_p = kkf[rs, sl]
        kk_p = kk_p / jnp.maximum(jnp.sqrt(per_head(kk_p * kk_p, jnp.sum)), 1e-12)
        k_p = kmod[rs, sl]
        bvec = kk_p * a[rs, sl]
        at_p = ((-kk_p) * g_in[rs, sl]).astype(BF16)
        rt_p = (r[rs, sl] * g_t[rs, sl]).astype(BF16)
        bt_p = (bvec * g_inv[rs, sl]).astype(BF16)
        kt_p = (k_p * g_inv[rs, sl]).astype(BF16)
        wt = jnp.concatenate([block_diag(bt_p), block_diag(kt_p)], axis=0)
        g4.append(_dot_nt(jnp.concatenate([at_p, rt_p], axis=0), wt))
        at.append(at_p)
        rt.append(rt_p)
        bh.append((bvec * g_end[rs, sl]).astype(BF16))
        kh.append((k_p * g_end[rs, sl]).astype(BF16))
        vb.append(v[rs, sl].astype(BF16))

    a_ab = [jnp.where(strict2, g4[p][0:C, 0:2 * C], 0.0) for p in P]
    a_ak = [jnp.where(strict2, g4[p][0:C, 2 * C:4 * C], 0.0).astype(BF16) for p in P]
    a_rb = [jnp.where(incl2, g4[p][C:2 * C, 0:2 * C], 0.0).astype(BF16) for p in P]
    a_rk = [jnp.where(incl2, g4[p][C:2 * C, 2 * C:4 * C], 0.0).astype(BF16) for p in P]

    pw = a_ab
    tinv = [jnp.where(eye2, 1.0, 0.0) for p in P]
    for level in range(6):
        pwb = [pw[p].astype(BF16) for p in P]
        bd = [block_diag(pwb[p]) for p in P]
        if level < 5:
            res = [_dot(jnp.concatenate([pwb[p], tinv[p].astype(BF16)], axis=0), bd[p]) for p in P]
            pw = [res[p][0:C] for p in P]
            tinv = [tinv[p] + res[p][C:2 * C] for p in P]
        else:
            tinv = [tinv[p] + _dot(tinv[p].astype(BF16), bd[p]) for p in P]

    hst = [h_scr[p] for p in P]
    hbd = [block_diag(hst[p].astype(BF16)) for p in P]
    vbd = [block_diag(vb[p]) for p in P]
    x = [_dot(jnp.concatenate([at[p], a_ak[p]], axis=1), jnp.concatenate([hbd[p], vbd[p]], axis=0)) for p in P]
    ub = [_dot(tinv[p].astype(BF16), block_diag(x[p].astype(BF16))).astype(BF16) for p in P]
    y = [_dot(jnp.concatenate([rt[p], a_rb[p], a_rk[p]], axis=1),
              jnp.concatenate([hbd[p], block_diag(ub[p]), vbd[p]], axis=0)) for p in P]
    for p in P:
        ga = g_all[rows[p], sls[p]]
        g_col = jnp.where(left, jnp.sum(jnp.where(row2 == lane2, ga, 0.0), axis=1, keepdims=True),
                          jnp.sum(jnp.where(row2 == lane2 - N, ga, 0.0), axis=1, keepdims=True))
        res = _dot_tn(jnp.concatenate([bh[p], kh[p]], axis=0), jnp.concatenate([ub[p], vb[p]], axis=0))
        h_scr[p] = g_col * hst[p] + jnp.where(left, res[0:N], res[N:2 * N])

    for p in P:
        rs, sl = rows[p], sls[p]
        mean = per_head(y[p], jnp.sum) * (1.0 / N)
        yc = y[p] - mean
        var = per_head(yc * yc, jnp.sum) * (1.0 / N)
        yn = yc * lax.rsqrt(var + GN_EPS) * lng_ref[:, sl] + lnb_ref[:, sl]
        bonus = per_head(rkr[rs, sl], jnp.sum) * v[rs, sl]
        y_ref[p // (H_RWKV // 2), :, sl] = (yn + bonus) * g[rs, sl]


def _rwkv_call(rw, mu_p, w0, wdec, a0, waaa, wgate_p, k_k, k_a, r_k, lng, lnb):
    B, T, _ = rw.shape
    NC = T // CHUNK
    W = RWKV_WIDTH
    c2 = lambda b, c: (0, 0)
    rowspec = pl.BlockSpec((1, W), c2)
    NS = next(n for n in (4, 2, 1) if B % n == 0)
    return pl.pallas_call(
        _rwkv_kernel,
        grid=(B // NS, NC),
        in_specs=[pl.BlockSpec((NS, CHUNK, RW_COLS), lambda b, c: (b, c, 0)),
                  pl.BlockSpec((NS, 8, RW_COLS), lambda b, c: (b, jnp.maximum(c * (CHUNK // 8) - 1, 0), 0)),
                  pl.BlockSpec((1, RW_COLS), c2),
                  rowspec,
                  pl.BlockSpec((DECAY_LORA, W), c2),
                  rowspec,
                  pl.BlockSpec((AAA_LORA, W), c2),
                  pl.BlockSpec((GATE_PAD, W), c2),
                  rowspec, rowspec, rowspec, rowspec, rowspec],
        out_specs=pl.BlockSpec((NS, CHUNK, W), lambda b, c: (b, c, 0)),
        out_shape=jax.ShapeDtypeStruct((B, T, W), F32),
        scratch_shapes=[pltpu.VMEM((NS * H_RWKV // 2, HD_RWKV, 2 * HD_RWKV), F32)],
        compiler_params=_cp(("parallel", "arbitrary")),
        name="rwkv",
    )(rw, rw, mu_p, w0, wdec, a0, waaa, wgate_p, k_k, k_a, r_k, lng, lnb)


def _ffn_kernel(x_ref, ya_ref, yr_ref, mod_ref, gpost_ref, gpre_ref, gfpost_ref,
                wo_ref, wg_ref, wu_ref, wd_ref, o_ref, x1_scr, hf_scr, acc_scr):
    j = pl.program_id(2)

    @pl.when(j == 0)
    def _():
        for rows in _row_chunks(x_ref.shape[1]):
            mix = (_dot(ya_ref[0, rows, :].astype(BF16), wo_ref[0:ATT_WIDTH, :])
                   + _dot(yr_ref[0, rows, :].astype(BF16), wo_ref[ATT_WIDTH:, :]))
            x1 = x_ref[0, rows, :] + mod_ref[0, 2:3, :] * _rms(mix, gpost_ref[...])
            x1_scr[rows, :] = x1
            hf = _rms(x1, gpre_ref[...]) * (1.0 + mod_ref[0, 4:5, :]) + mod_ref[0, 3:4, :]
            hf_scr[rows, :] = hf.astype(BF16)
        acc_scr[...] = jnp.zeros(acc_scr.shape, F32)

    hf = hf_scr[...]
    gate = _dot(hf, wg_ref[...])
    up = _dot(hf, wu_ref[...])
    act = (gate * jax.nn.sigmoid(gate) * up).astype(BF16)
    acc_scr[...] += _dot(act, wd_ref[...])

    @pl.when(j == pl.num_programs(2) - 1)
    def _():
        o_ref[0] = x1_scr[...] + mod_ref[0, 5:6, :] * _rms(acc_scr[...], gfpost_ref[...])


def _ffn_call(x, y_att, y_rwkv, mod3, g_post, g_pre, g_fpost, wo, wg, wu, wd):
    B, T, D = x.shape
    TM = 512
    TF = D_FF // 2
    c3 = lambda b, i, j: (0, 0)
    return pl.pallas_call(
        _ffn_kernel,
        grid=(B, T // TM, D_FF // TF),
        in_specs=[pl.BlockSpec((1, TM, D), lambda b, i, j: (b, i, 0)),
                  pl.BlockSpec((1, TM, ATT_WIDTH), lambda b, i, j: (b, i, 0)),
                  pl.BlockSpec((1, TM, RWKV_WIDTH), lambda b, i, j: (b, i, 0)),
                  pl.BlockSpec((1, 6, D), lambda b, i, j: (b, 0, 0)),
                  pl.BlockSpec((1, D), c3), pl.BlockSpec((1, D), c3), pl.BlockSpec((1, D), c3),
                  pl.BlockSpec((D, D), c3),
                  pl.BlockSpec((D, TF), lambda b, i, j: (0, j)),
                  pl.BlockSpec((D, TF), lambda b, i, j: (0, j)),
                  pl.BlockSpec((TF, D), lambda b, i, j: (j, 0))],
        out_specs=pl.BlockSpec((1, TM, D), lambda b, i, j: (b, i, 0)),
        out_shape=jax.ShapeDtypeStruct((B, T, D), F32),
        scratch_shapes=[pltpu.VMEM((TM, D), F32), pltpu.VMEM((TM, D), BF16), pltpu.VMEM((TM, D), F32)],
        compiler_params=_cp(("parallel", "parallel", "arbitrary")),
        name="ffn",
    )(x, y_att, y_rwkv, mod3, g_post, g_pre, g_fpost, wo, wg, wu, wd)


def _pad_cols(w, n):
    return jnp.pad(w, ((0, 0), (0, n - w.shape[1])))


def _layer(x, mod3, rel_bias, mix_pre_norm, mix_post_norm, ffn_pre_norm, ffn_post_norm,
           w_in, q_norm, w_uq, w_idx_q, kv_norm, idx_k_norm, w_uk, w_uv,
           mu_shift, w0, w_decay_up, a0, w_aaa_up, w_gate_up, k_k, k_a, r_k, ln_x_gain, ln_x_bias,
           w_out, w_ffn_gate, w_ffn_up, w_ffn_down):
    B, T, D = x.shape
    att_cols = Q_LORA + KV_LATENT + IDX_DIM + IDX_HEADS
    n_main = 3 * RWKV_WIDTH + DECAY_LORA + AAA_LORA
    w_att = _pad_cols(w_in[:, :att_cols], ATT_PAD)
    w_rw = w_in[:, att_cols:]
    w_in_p = jnp.concatenate([w_att, w_rw[:, :n_main], _pad_cols(w_rw[:, n_main:], GATE_PAD)], axis=1).astype(BF16)
    mu_p = jnp.concatenate([mu_shift[:n_main], jnp.pad(mu_shift[n_main:], (0, GATE_PAD - GATE_LORA))]).reshape(1, RW_COLS)
    wgate_p = jnp.pad(w_gate_up, ((0, GATE_PAD - GATE_LORA), (0, 0))).astype(BF16)

    att, rw = _inproj_call(x, mod3, mix_pre_norm.reshape(1, D), w_in_p)

    wuq = w_uq.reshape(Q_LORA, ATT_WIDTH).astype(BF16)
    wiq = w_idx_q.reshape(Q_LORA, IDX_HEADS * IDX_DIM).astype(BF16)
    wukt = jnp.transpose(w_uk, (0, 2, 1)).astype(BF16)
    qabs, iq, ik, ckv, ckvt, iwt = _dsaprep_call(att, q_norm.reshape(1, -1), kv_norm.reshape(1, -1),
                                                 idx_k_norm.reshape(1, -1), wuq, wiq, wukt)
    bias_tiles = _bias_call(rel_bias)
    wuvt = jnp.transpose(w_uv, (0, 2, 1)).astype(BF16)
    topk = min(TOPK_MAX, T // 4)
    y_att = _dsa_call(iq, iwt, qabs, ik, ckv, ckvt, bias_tiles, wuvt, topk)

    row = lambda z: z.reshape(1, RWKV_WIDTH)
    y_rwkv = _rwkv_call(rw, mu_p, row(w0), w_decay_up.astype(BF16), row(a0), w_aaa_up.astype(BF16),
                        wgate_p, row(k_k), row(k_a), row(r_k), row(ln_x_gain), row(ln_x_bias))

    return _ffn_call(x, y_att, y_rwkv, mod3, mix_post_norm.reshape(1, D), ffn_pre_norm.reshape(1, D),
                     ffn_post_norm.reshape(1, D), w_out.astype(BF16), w_ffn_gate.astype(BF16),
                     w_ffn_up.astype(BF16), w_ffn_down.astype(BF16))


def kernel(x, c, rel_bias, ada_w, ada_b, mix_pre_norm, mix_post_norm, ffn_pre_norm, ffn_post_norm, w_in, q_norm, w_uq, w_idx_q, kv_norm, idx_k_norm, w_uk, w_uv, mu_shift, w0, w_decay_up, a0, w_aaa_up, w_gate_up, k_k, k_a, r_k, ln_x_gain, ln_x_bias, w_out, w_ffn_gate, w_ffn_up, w_ffn_down):
    B, T, D = x.shape
    assert D == D_MODEL and T % 512 == 0 and T // 4 >= 1
    layer_params = (mix_pre_norm, mix_post_norm, ffn_pre_norm, ffn_post_norm,
                    w_in, q_norm, w_uq, w_idx_q, kv_norm, idx_k_norm, w_uk, w_uv,
                    mu_shift, w0, w_decay_up, a0, w_aaa_up, w_gate_up, k_k, k_a, r_k, ln_x_gain, ln_x_bias,
                    w_out, w_ffn_gate, w_ffn_up, w_ffn_down)
    for l in range(ada_w.shape[0]):
        mod3 = _mod_call(c, ada_w[l], ada_b[l]).reshape(B, 6, D)
        x = _layer(x, mod3, rel_bias, *[p[l] for p in layer_params])
    return x
```

```python
import functools
import math

import numpy as np
import jax
import jax.numpy as jnp
from jax import lax
from jax.experimental import pallas as pl
from jax.experimental.pallas import tpu as pltpu

F32 = jnp.float32
BF16 = jnp.bfloat16

D_MODEL = 1024
ATT_WIDTH = 512
RWKV_WIDTH = 512
HD_ATT = 64
H_ATT = 8
HD_RWKV = 64
H_RWKV = 8
Q_LORA = 256
KV_LATENT = 128
IDX_HEADS = 8
IDX_DIM = 64
TOPK_MAX = 256
NUM_BUCKETS = 32
MAX_DISTANCE = 128
DECAY_LORA = 64
AAA_LORA = 64
GATE_LORA = 160
GN_EPS = 64e-5
D_FF = 2816
NORM_EPS = 1e-6

ATT_PAD = 512
GATE_PAD = 256
RW_COLS = 3 * RWKV_WIDTH + DECAY_LORA + AAA_LORA + GATE_PAD
OFF_WD = 3 * RWKV_WIDTH
OFF_AD = OFF_WD + DECAY_LORA
OFF_GD = OFF_AD + AAA_LORA

TQ = 128
CHUNK = 64
NEG_BIG = -1e30
LOWEST = -3.0e38
LOG2E = math.log2(math.e)
VMEM_LIMIT = 56 * 1024 * 1024


def _cp(sem):
    return pltpu.CompilerParams(dimension_semantics=sem, vmem_limit_bytes=VMEM_LIMIT)


def _dot(a, b):
    return jnp.dot(a, b, preferred_element_type=F32)


def _dot_nt(a, b):
    return lax.dot_general(a, b, (((1,), (1,)), ((), ())), preferred_element_type=F32)


def _dot_tn(a, b):
    return lax.dot_general(a, b, (((0,), (0,)), ((), ())), preferred_element_type=F32)


def _rms(z, gain):
    return z * lax.rsqrt(jnp.mean(z * z, axis=-1, keepdims=True) + NORM_EPS) * gain


def _row_chunks(n, chunk=256):
    return [slice(r, r + chunk) for r in range(0, n, chunk)]


def _mod_kernel(c_ref, w_ref, b_ref, o_ref):
    c = c_ref[...]
    s = c * jax.nn.sigmoid(c)
    o_ref[...] = _dot(s.astype(BF16), w_ref[...].astype(BF16)) + b_ref[...]


def _mod_call(c, ada_w, ada_b):
    B, D = c.shape
    N = ada_w.shape[1]
    TN = 1536
    return pl.pallas_call(
        _mod_kernel,
        grid=(N // TN,),
        in_specs=[pl.BlockSpec((B, D), lambda j: (0, 0)),
                  pl.BlockSpec((D, TN), lambda j: (0, j)),
                  pl.BlockSpec((1, TN), lambda j: (0, j))],
        out_specs=pl.BlockSpec((B, TN), lambda j: (0, j)),
        out_shape=jax.ShapeDtypeStruct((B, N), F32),
        compiler_params=_cp(("parallel",)),
        name="mod",
    )(c, ada_w, ada_b.reshape(1, N))


def _inproj_kernel(x_ref, mod_ref, g_ref, w_ref, att_ref, rw_ref):
    sh = mod_ref[0, 0:1, :]
    sc = mod_ref[0, 1:2, :]
    h = (_rms(x_ref[0], g_ref[...]) * (1.0 + sc) + sh).astype(BF16)
    att_ref[0] = _dot(h, w_ref[:, 0:ATT_PAD])
    for n0 in range(0, RW_COLS, 384):
        rw_ref[0, :, n0:n0 + 384] = _dot(h, w_ref[:, ATT_PAD + n0:ATT_PAD + n0 + 384])


def _inproj_call(x, mod3, gain, w_in_p):
    B, T, D = x.shape
    TM = 1024 if T % 1024 == 0 else 512
    NP = w_in_p.shape[1]
    return pl.pallas_call(
        _inproj_kernel,
        grid=(B, T // TM),
        in_specs=[pl.BlockSpec((1, TM, D), lambda b, i: (b, i, 0)),
                  pl.BlockSpec((1, 6, D), lambda b, i: (b, 0, 0)),
                  pl.BlockSpec((1, D), lambda b, i: (0, 0)),
                  pl.BlockSpec((D, NP), lambda b, i: (0, 0))],
        out_specs=[pl.BlockSpec((1, TM, ATT_PAD), lambda b, i: (b, i, 0)),
                   pl.BlockSpec((1, TM, RW_COLS), lambda b, i: (b, i, 0))],
        out_shape=[jax.ShapeDtypeStruct((B, T, ATT_PAD), F32),
                   jax.ShapeDtypeStruct((B, T, RW_COLS), F32)],
        compiler_params=_cp(("parallel", "parallel")),
        name="inproj",
    )(x, mod3, gain, w_in_p)


def _dsaprep_kernel(att_ref, qn_ref, kvn_ref, ikn_ref, wuq_ref, wiq_ref, wukt_ref,
                    qabs_ref, iq_ref, ik_ref, ckv_ref, ckvt_ref, iwt_ref):
    att = att_ref[0]
    tm = att.shape[0]
    o1 = Q_LORA
    o2 = o1 + KV_LATENT
    o3 = o2 + IDX_DIM
    cq = _rms(att[:, 0:o1], qn_ref[...]).astype(BF16)
    ckv = _rms(att[:, o1:o2], kvn_ref[...])
    ckv_ref[0] = ckv.astype(BF16)
    ik_ref[0] = _rms(att[:, o2:o3], ikn_ref[...]).astype(BF16)
    tail = att[:, o2:ATT_PAD]
    w_off = o3 - o2
    for sb in range(tm // TQ):
        rows = slice(sb * TQ, (sb + 1) * TQ)
        ckvt_ref[0, sb] = ckv[rows, :].T.astype(BF16)
        iwt_ref[0, :, rows] = tail[rows, :].T[w_off:w_off + IDX_HEADS, :] * (IDX_HEADS ** -0.5 * IDX_DIM ** -0.5)
    q = _dot(cq, wuq_ref[...])
    iqv = _dot(cq, wiq_ref[...])
    for h in range(H_ATT):
        qh = q[:, h * HD_ATT:(h + 1) * HD_ATT].astype(BF16)
        qa = _dot(qh, wukt_ref[h]) * (HD_ATT ** -0.5 * LOG2E)
        for sb in range(tm // TQ):
            qabs_ref[0, sb, :, h * TQ:(h + 1) * TQ] = qa[sb * TQ:(sb + 1) * TQ, :].T.astype(BF16)
    per = TQ // IDX_DIM
    for g in range(IDX_HEADS // per):
        for sb in range(tm // TQ):
            blk = iqv[sb * TQ:(sb + 1) * TQ, g * TQ:(g + 1) * TQ].T.astype(BF16)
            for u in range(per):
                h = g * per + u
                iq_ref[0, sb, :, h * TQ:(h + 1) * TQ] = blk[u * IDX_DIM:(u + 1) * IDX_DIM, :]


def _dsaprep_call(att, q_norm, kv_norm, idx_k_norm, wuq, wiq, wukt):
    B, T, _ = att.shape
    TM = 1024 if T % 1024 == 0 else 512
    const2 = lambda b, i: (0, 0)
    return pl.pallas_call(
        _dsaprep_kernel,
        grid=(B, T // TM),
        in_specs=[pl.BlockSpec((1, TM, ATT_PAD), lambda b, i: (b, i, 0)),
                  pl.BlockSpec((1, Q_LORA), const2),
                  pl.BlockSpec((1, KV_LATENT), const2),
                  pl.BlockSpec((1, IDX_DIM), const2),
                  pl.BlockSpec((Q_LORA, ATT_WIDTH), const2),
                  pl.BlockSpec((Q_LORA, IDX_HEADS * IDX_DIM), const2),
                  pl.BlockSpec((H_ATT, HD_ATT, KV_LATENT), lambda b, i: (0, 0, 0))],
        out_specs=[pl.BlockSpec((1, TM // TQ, KV_LATENT, H_ATT * TQ), lambda b, i: (b, i, 0, 0)),
                   pl.BlockSpec((1, TM // TQ, IDX_DIM, IDX_HEADS * TQ), lambda b, i: (b, i, 0, 0)),
                   pl.BlockSpec((1, TM, IDX_DIM), lambda b, i: (b, i, 0)),
                   pl.BlockSpec((1, TM, KV_LATENT), lambda b, i: (b, i, 0)),
                   pl.BlockSpec((1, TM // TQ, KV_LATENT, TQ), lambda b, i: (b, i, 0, 0)),
                   pl.BlockSpec((1, IDX_HEADS, TM), lambda b, i: (b, 0, i))],
        out_shape=[jax.ShapeDtypeStruct((B, T // TQ, KV_LATENT, H_ATT * TQ), BF16),
                   jax.ShapeDtypeStruct((B, T // TQ, IDX_DIM, IDX_HEADS * TQ), BF16),
                   jax.ShapeDtypeStruct((B, T, IDX_DIM), BF16),
                   jax.ShapeDtypeStruct((B, T, KV_LATENT), BF16),
                   jax.ShapeDtypeStruct((B, T // TQ, KV_LATENT, TQ), BF16),
                   jax.ShapeDtypeStruct((B, IDX_HEADS, T), F32)],
        compiler_params=_cp(("parallel", "parallel")),
        name="dsaprep",
    )(att, q_norm, kv_norm, idx_k_norm, wuq, wiq, wukt)


def _t5_bucket_np(rel):
    max_exact = NUM_BUCKETS // 2
    nf = np.maximum(rel, 1).astype(np.float32)
    large = max_exact + (np.log(nf / max_exact) / math.log(MAX_DISTANCE / max_exact)
                         * (NUM_BUCKETS - max_exact)).astype(np.int32)
    large = np.minimum(large, NUM_BUCKETS - 1)
    return np.where(rel < max_exact, rel, large).astype(np.int32)


def _near_bucket_tiles():
    s = np.arange(TQ)[:, None]
    t = np.arange(TQ)[None, :]
    tiles = [_t5_bucket_np(np.maximum(t - s + TQ * d, 0)) for d in range(2)]
    assert _t5_bucket_np(np.array([TQ + 1]))[0] == NUM_BUCKETS - 1
    return np.stack(tiles)


def _bias_kernel(rb_ref, bk_ref, o_ref):
    for d in range(2):
        bk = bk_ref[d]
        for h in range(H_ATT):
            def body(b, acc):
                return jnp.where(bk == b, rb_ref[b, h], acc)
            acc = lax.fori_loop(0, NUM_BUCKETS, body, jnp.zeros((TQ, TQ), F32))
            o_ref[d, h] = (acc - rb_ref[NUM_BUCKETS - 1, h]) * LOG2E
    o_ref[2] = jnp.zeros((H_ATT, TQ, TQ), F32)


def _bias_call(rel_bias):
    buckets = jnp.asarray(_near_bucket_tiles())
    return pl.pallas_call(
        _bias_kernel,
        in_specs=[pl.BlockSpec(memory_space=pltpu.SMEM),
                  pl.BlockSpec(memory_space=pltpu.VMEM)],
        out_specs=pl.BlockSpec(memory_space=pltpu.VMEM),
        out_shape=jax.ShapeDtypeStruct((3, H_ATT, TQ, TQ), F32),
        name="bias",
    )(rel_bias, buckets)


KV = TQ // 8
FREE_PROBES = 11
COUNT_PROBES = 6


def _key_reduce(x, op):
    return op(op(x, axis=0), axis=0, keepdims=True)


def _dsa_kernel(iq_ref, iwt_ref, qabs_ref, ik_ref, ckv_ref, ckvt_ref, bias_ref, wuvt_ref, y_ref,
                score_scr, stage_a, stage_b, pv_a, pv_b, alpha_a, alpha_b, acc_scr, *, topk):
    i = pl.program_id(1)
    kf = float(topk)
    kidx = (lax.broadcasted_iota(jnp.int32, (KV, 8, TQ), 0) * 8
            + lax.broadcasted_iota(jnp.int32, (KV, 8, TQ), 1))
    qidx = lax.broadcasted_iota(jnp.int32, (KV, 8, TQ), 2)
    n_pairs = i // 2 + 1

    iwt = iwt_ref[0]
    w8 = [jnp.broadcast_to(iwt[h:h + 1, :], (8, TQ))[None] for h in range(IDX_HEADS)]
    iq2 = iq_ref[0, 0]

    n_trips = (n_pairs + 1) // 2
    stage = (stage_a, stage_b)

    def stage_pair(q, lhs_ref, rhs, dst):
        nxt = jnp.minimum(q, n_pairs - 1)
        rows = lhs_ref[0, pl.ds(pl.multiple_of(2 * nxt * TQ, 2 * TQ), 2 * TQ), :]
        dst[...] = _dot(rows, rhs).reshape(2, TQ, H_ATT * TQ)

    def two_halves(half):
        def trip(r, carry):
            carry = half(2 * r, 0, carry)
            return lax.cond(2 * r + 1 < n_pairs, lambda c: half(2 * r + 1, 1, c), lambda c: c, carry)
        return trip

    def score_half(q, par, carry):
        rmax, rmin = carry
        stage_pair(q + 1, ik_ref, iq2, stage[1 - par])
        for blk in range(2):
            j = 2 * q + blk
            sc = jnp.zeros((KV, 8, TQ), F32)
            for h in range(IDX_HEADS):
                s = stage[par][blk, :, h * TQ:(h + 1) * TQ].reshape(KV, 8, TQ)
                sc = sc + jnp.maximum(s, 0.0) * w8[h]
            causal = kidx + (j - i) * TQ <= qidx
            score_scr[j] = jnp.where(causal, sc, -jnp.inf)
            rmax = jnp.maximum(rmax, jnp.max(jnp.where(causal, sc, -jnp.inf), axis=0))
            rmin = jnp.minimum(rmin, jnp.min(jnp.where(causal, sc, jnp.inf), axis=0))
        return rmax, rmin

    stage_pair(0, ik_ref, iq2, stage[0])
    rmax, rmin = lax.fori_loop(0, n_trips, two_halves(score_half),
                               (jnp.full((8, TQ), -jnp.inf, F32), jnp.full((8, TQ), jnp.inf, F32)))
    smax = jnp.max(rmax, axis=0, keepdims=True)
    smin = jnp.min(rmin, axis=0, keepdims=True)

    nvalid = (i * TQ + lax.broadcasted_iota(jnp.int32, (1, TQ), 1) + 1).astype(F32)
    needs_thr = nvalid > kf

    def probe(v):
        vb = jnp.broadcast_to(v, (8, TQ))[None, None]

        def body(q, st):
            cnt, dn, up = st
            for j in (2 * q, 2 * q + 1):
                s = score_scr[j].reshape(4, KV // 4, 8, TQ)
                ge = s >= vb
                cnt = cnt + jnp.sum(jnp.where(ge, 1.0, 0.0), axis=1)
                dn = jnp.minimum(dn, jnp.min(jnp.where(ge, s, jnp.inf), axis=1))
                up = jnp.maximum(up, jnp.max(jnp.where(ge, -jnp.inf, s), axis=1))
            return cnt, dn, up
        cnt, dn, up = lax.fori_loop(0, n_pairs, body, (jnp.zeros((4, 8, TQ), F32),
                                                       jnp.full((4, 8, TQ), jnp.inf, F32),
                                                       jnp.full((4, 8, TQ), -jnp.inf, F32)))
        return (jnp.sum(jnp.sum(cnt, axis=0), axis=0, keepdims=True),
                jnp.min(jnp.min(dn, axis=0), axis=0, keepdims=True),
                jnp.max(jnp.max(up, axis=0), axis=0, keepdims=True))

    def probe_count(v):
        vb = jnp.broadcast_to(v, (8, TQ))[None, None]

        def body(q, cnt):
            for j in (2 * q, 2 * q + 1):
                s = score_scr[j].reshape(4, KV // 4, 8, TQ)
                cnt = cnt + jnp.sum(jnp.where(s >= vb, 1.0, 0.0), axis=1)
            return cnt
        cnt = lax.fori_loop(0, n_pairs, body, jnp.zeros((4, 8, TQ), F32))
        return jnp.sum(jnp.sum(cnt, axis=0), axis=0, keepdims=True)

    def logit(c):
        return jnp.log((c + 0.5) / (nvalid - c + 0.5))

    g_k = logit(jnp.full((1, TQ), kf - 0.5, F32))

    def search_step(st, snap=True):
        lo, hi, clo, ghi, slo, shi = st
        is_open = lo < hi
        g_lo = logit(clo)
        g_hi = logit(ghi)
        phi = jnp.clip((g_lo - g_k) / (g_lo - g_hi), 0.02, 0.98)
        stalled = (slo >= 2.0) | (shi >= 2.0)
        mid = jnp.where(stalled, 0.5 * lo + 0.5 * hi, lo + (hi - lo) * phi)
        mid = jnp.where((mid > lo) & (mid <= hi), mid, hi)
        if snap:
            c, dn, up = probe(mid)
        else:
            c = probe_count(mid)
            dn = up = mid
        ge = c >= kf
        move_lo = is_open & ge
        move_hi = is_open & jnp.logical_not(ge)
        lo = jnp.where(move_lo, dn, lo)
        clo = jnp.where(move_lo, c, clo)
        hi = jnp.where(move_hi, up, hi)
        ghi = jnp.where(move_hi, c, ghi)
        slo = jnp.where(stalled | ge, 0.0, slo + 1.0)
        shi = jnp.where(stalled | jnp.logical_not(ge), 0.0, shi + 1.0)
        return lo, hi, clo, ghi, slo, shi

    def n_open(st):
        return jnp.sum(jnp.where(st[0] < st[1], 1.0, 0.0))

    def checked_step(carry):
        st = search_step(carry[0])
        return st, n_open(st), carry[2] + 1

    search_cap = ik_ref.shape[1] + 2
    zero = jnp.zeros((1, TQ), F32)
    st = (smin, jnp.where(needs_thr, smax, smin), nvalid, zero, zero, zero)
    st = lax.fori_loop(0, COUNT_PROBES, lambda _, s: search_step(s, snap=False), st)
    st = lax.fori_loop(COUNT_PROBES, FREE_PROBES, lambda _, s: search_step(s), st)
    st, _, _ = lax.while_loop(lambda c: jnp.logical_and(c[1] > 0.0, c[2] < search_cap), checked_step,
                              (st, n_open(st), jnp.int32(0)))
    thr, _, clo, chi, _, _ = st

    tie = needs_thr & (clo > kf)
    n_tie = jnp.sum(jnp.where(tie, 1.0, 0.0))

    @pl.when(n_tie > 0.0)
    def _():
        r2 = lax.broadcasted_iota(jnp.int32, (TQ, TQ), 0)
        c2 = lax.broadcasted_iota(jnp.int32, (TQ, TQ), 1)
        lower = jnp.where(c2 <= r2, 1.0, 0.0).astype(BF16)
        thrb = jnp.broadcast_to(thr, (TQ, TQ))
        tieb = jnp.broadcast_to(tie, (TQ, TQ))
        needb = jnp.broadcast_to(kf - chi, (TQ, TQ))

        def body(j, carry):
            s = score_scr[j].reshape(TQ, TQ)
            eq = (s == thrb) & tieb
            pref = _dot(lower, jnp.where(eq, 1.0, 0.0).astype(BF16)) + carry
            score_scr[j] = jnp.where(eq & (pref > needb), -jnp.inf, s).reshape(KV, 8, TQ)
            return jnp.broadcast_to(pref[TQ - 1:TQ, :], (TQ, TQ))
        lax.fori_loop(0, i + 1, body, jnp.zeros((TQ, TQ), F32))

    thr_eff = jnp.broadcast_to(jnp.where(needs_thr, thr, LOWEST), (8, TQ))[None]

    acc_scr[...] = jnp.zeros(acc_scr.shape, F32)
    qa2 = qabs_ref[0, 0]

    pv = (pv_a, pv_b)
    alpha_st = (alpha_a, alpha_b)

    def apply_pending(par):
        acc_scr[...] = alpha_st[par][...] * acc_scr[...] + pv[par][...]

    def attend_variant(q, par, carry, near):
        m = list(carry)
        stage_pair(q + 1, ckv_ref, qa2, stage[1 - par])
        js = (2 * q, 2 * q + 1)
        sel = [score_scr[j] >= thr_eff for j in js]
        d = [jnp.clip(i - j, 0, 2) for j in js]
        kvt = jnp.concatenate([jnp.concatenate([ckvt_ref[0, js[0]], ckvt_ref[0, js[1]]], axis=1),
                               jnp.ones((8, 2 * TQ), BF16)], axis=0)
        ps, alphas = [], []
        for h in range(H_ATT):
            lg = [stage[par][b, :, h * TQ:(h + 1) * TQ] for b in range(2)]
            if near:
                lg = [lg[b] + bias_ref[d[b], h] for b in range(2)]
            lg = [jnp.where(sel[b], lg[b].reshape(KV, 8, TQ), NEG_BIG) for b in range(2)]
            m_new = jnp.maximum(m[h], jnp.maximum(_key_reduce(lg[0], jnp.max), _key_reduce(lg[1], jnp.max)))
            alpha = jnp.exp2(m[h] - m_new)
            mb = jnp.broadcast_to(m_new, (8, TQ))[None]
            p = [jnp.exp2(lg[b] - mb) for b in range(2)]
            m[h] = m_new
            alphas.append(alpha)
            ps.append(jnp.concatenate([p[0].reshape(TQ, TQ), p[1].reshape(TQ, TQ)], axis=0).astype(BF16))
            if h % 2 == 1:
                cols = slice((h - 1) * TQ, (h + 1) * TQ)
                pv[par][:, cols] = _dot(kvt, jnp.concatenate(ps[h - 1:h + 1], axis=1))
        alpha_st[par][...] = jnp.concatenate(alphas, axis=1)
        apply_pending(1 - par)
        return tuple(m)

    def attend_half(q, par, carry):
        return lax.cond(2 * q + 1 >= i - 1,
                        lambda c: attend_variant(q, par, c, True),
                        lambda c: attend_variant(q, par, c, False), carry)

    pv_b[...] = jnp.zeros(pv_b.shape, F32)
    alpha_b[...] = jnp.ones(alpha_b.shape, F32)
    stage_pair(0, ckv_ref, qa2, stage[0])
    m0 = tuple(jnp.full((1, TQ), NEG_BIG, F32) for _ in range(H_ATT))
    lax.fori_loop(0, n_trips, two_halves(attend_half), m0)
    for par in range(2):
        @pl.when((n_pairs - 1) % 2 == par)
        def _():
            apply_pending(par)

    yt = []
    for h in range(H_ATT):
        cols = slice(h * TQ, (h + 1) * TQ)
        o = acc_scr[0:KV_LATENT, cols] / acc_scr[KV_LATENT:KV_LATENT + 1, cols]
        yt.append(_dot(wuvt_ref[h], o.astype(BF16)))
    for g in range(ATT_WIDTH // TQ):
        per = TQ // HD_ATT
        y_ref[0, :, g * TQ:(g + 1) * TQ] = jnp.concatenate(yt[g * per:(g + 1) * per], axis=0).T


def _dsa_call(iq, iwt, qabs, ik, ckv, ckvt, bias_tiles, wuvt, topk):
    B, T, _ = ik.shape
    NT = T // TQ
    return pl.pallas_call(
        functools.partial(_dsa_kernel, topk=topk),
        grid=(B, NT),
        in_specs=[pl.BlockSpec((1, 1, IDX_DIM, IDX_HEADS * TQ), lambda b, i: (b, i, 0, 0)),
                  pl.BlockSpec((1, IDX_HEADS, TQ), lambda b, i: (b, 0, i)),
                  pl.BlockSpec((1, 1, KV_LATENT, H_ATT * TQ), lambda b, i: (b, i, 0, 0)),
                  pl.BlockSpec((1, T, IDX_DIM), lambda b, i: (b, 0, 0)),
                  pl.BlockSpec((1, T, KV_LATENT), lambda b, i: (b, 0, 0)),
                  pl.BlockSpec((1, NT, KV_LATENT, TQ), lambda b, i: (b, 0, 0, 0)),
                  pl.BlockSpec((3, H_ATT, TQ, TQ), lambda b, i: (0, 0, 0, 0)),
                  pl.BlockSpec((H_ATT, HD_ATT, KV_LATENT), lambda b, i: (0, 0, 0))],
        out_specs=pl.BlockSpec((1, TQ, ATT_WIDTH), lambda b, i: (b, i, 0)),
        out_shape=jax.ShapeDtypeStruct((B, T, ATT_WIDTH), F32),
        scratch_shapes=[pltpu.VMEM((NT, KV, 8, TQ), F32),
                        pltpu.VMEM((2, TQ, H_ATT * TQ), F32),
                        pltpu.VMEM((2, TQ, H_ATT * TQ), F32),
                        pltpu.VMEM((KV_LATENT + 8, H_ATT * TQ), F32),
                        pltpu.VMEM((KV_LATENT + 8, H_ATT * TQ), F32),
                        pltpu.VMEM((1, H_ATT * TQ), F32),
                        pltpu.VMEM((1, H_ATT * TQ), F32),
                        pltpu.VMEM((KV_LATENT + 8, H_ATT * TQ), F32)],
        compiler_params=_cp(("parallel", "arbitrary")),
        name="dsa",
    )(iq, iwt, qabs, ik, ckv, ckvt, bias_tiles, wuvt)


def _rwkv_kernel(p_ref, prev_ref, mu_ref, w0_ref, wdec_ref, a0_ref, waaa_ref, wgate_ref,
                 kk_ref, ka_ref, rk_ref, lng_ref, lnb_ref, y_ref, h_scr):
    c = pl.program_id(1)
    C = CHUNK
    N = HD_RWKV

    @pl.when(c == 0)
    def _():
        h_scr[...] = jnp.zeros(h_scr.shape, F32)

    NS = p_ref.shape[0]
    p = p_ref[...].reshape(NS * C, RW_COLS)
    rid = lax.broadcasted_iota(jnp.int32, (NS * C, 1), 0)
    p_prev = pltpu.roll(p, 1, 0)
    for s in range(NS):
        p_prev = jnp.where(rid == s * C, jnp.where(c > 0, prev_ref[s, 7:8, :], 0.0), p_prev)
    p = p + mu_ref[...] * (p_prev - p)

    W = RWKV_WIDTH
    r = p[:, 0:W]
    k = p[:, W:2 * W]
    v = p[:, 2 * W:3 * W]
    wd = p[:, OFF_WD:OFF_WD + DECAY_LORA]
    ad = p[:, OFF_AD:OFF_AD + AAA_LORA]
    gd = p[:, OFF_GD:OFF_GD + GATE_PAD]

    w_log = -jax.nn.softplus(-(w0_ref[...] + _dot(jnp.tanh(wd).astype(BF16), wdec_ref[...]))) - 0.5
    logw = -jnp.exp(w_log)
    a = jax.nn.sigmoid(a0_ref[...] + _dot(ad.astype(BF16), waaa_ref[...]))
    g = _dot(jax.nn.sigmoid(gd).astype(BF16), wgate_ref[...])

    ra = lax.broadcasted_iota(jnp.int32, (NS * C, NS * C), 0)
    ca = lax.broadcasted_iota(jnp.int32, (NS * C, NS * C), 1)
    same_seq = (ra >= ca) & (ra - ca <= ra % C)
    cum = jnp.dot(jnp.where(same_seq, 1.0, 0.0), logw, preferred_element_type=F32,
                  precision=lax.Precision.HIGHEST)
    cum_last = cum[C - 1:C, :]
    for s in range(1, NS):
        cum_last = jnp.where(rid >= s * C, cum[(s + 1) * C - 1:(s + 1) * C, :], cum_last)
    g_in = jnp.exp(cum - logw)
    g_t = jnp.exp(cum)
    g_inv = jnp.exp(-cum)
    g_end = jnp.exp(cum_last - cum)
    g_all = jnp.exp(cum_last)

    kkf = k * kk_ref[...]
    kmod = k * (1.0 + (a - 1.0) * ka_ref[...])
    rkr = r * kmod * rk_ref[...]

    P = range(NS * H_RWKV // 2)
    rows = [slice((p // (H_RWKV // 2)) * C, (p // (H_RWKV // 2) + 1) * C) for p in P]
    sls = [slice((p % (H_RWKV // 2)) * 2 * N, (p % (H_RWKV // 2) + 1) * 2 * N) for p in P]
    lane2 = lax.broadcasted_iota(jnp.int32, (C, 2 * N), 1)
    row2 = lax.broadcasted_iota(jnp.int32, (C, 2 * N), 0)
    left = lane2 < N
    col2 = jnp.where(left, lane2, lane2 - N)
    strict2 = row2 > col2
    incl2 = row2 >= col2
    eye2 = row2 == col2

    def per_head(x, op):
        return jnp.where(left, op(jnp.where(left, x, 0.0), axis=-1, keepdims=True),
                         op(jnp.where(left, 0.0, x), axis=-1, keepdims=True))

    def block_diag(x):
        zero = jnp.zeros_like(x)
        return jnp.concatenate([jnp.where(left, x, zero), jnp.where(left, zero, x)], axis=0)

    at, rt, bh, kh, vb, g4 = [], [], [], [], [], []
    for p in P:
        rs, sl = rows[p], sls[p]
        kk_p = kkf[rs, sl]
        kk_p = kk_p / jnp.maximum(jnp.sqrt(per_head(kk_p * kk_p, jnp.sum)), 1e-12)
        k_p = kmod[rs, sl]
        bvec = kk_p * a[rs, sl]
        at_p = ((-kk_p) * g_in[rs, sl]).astype(BF16)
        rt_p = (r[rs, sl] * g_t[rs, sl]).astype(BF16)
        bt_p = (bvec * g_inv[rs, sl]).astype(BF16)
        kt_p = (k_p * g_inv[rs, sl]).astype(BF16)
        wt = jnp.concatenate([block_diag(bt_p), block_diag(kt_p)], axis=0)
        g4.append(_dot_nt(jnp.concatenate([at_p, rt_p], axis=0), wt))
        at.append(at_p)
        rt.append(rt_p)
        bh.append((bvec * g_end[rs, sl]).astype(BF16))
        kh.append((k_p * g_end[rs, sl]).astype(BF16))
        vb.append(v[rs, sl].astype(BF16))

    a_ab = [jnp.where(strict2, g4[p][0:C, 0:2 * C], 0.0) for p in P]
    a_ak = [jnp.where(strict2, g4[p][0:C, 2 * C:4 * C], 0.0).astype(BF16) for p in P]
    a_rb = [jnp.where(incl2, g4[p][C:2 * C, 0:2 * C], 0.0).astype(BF16) for p in P]
    a_rk = [jnp.where(incl2, g4[p][C:2 * C, 2 * C:4 * C], 0.0).astype(BF16) for p in P]

    pw = a_ab
    tinv = [jnp.where(eye2, 1.0, 0.0) for p in P]
    for level in range(6):
        pwb = [pw[p].astype(BF16) for p in P]
        bd = [block_diag(pwb[p]) for p in P]
        if level < 5:
            res = [_dot(jnp.concatenate([pwb[p], tinv[p].astype(BF16)], axis=0), bd[p]) for p in P]
            pw = [res[p][0:C] for p in P]
            tinv = [tinv[p] + res[p][C:2 * C] for p in P]
        else:
            tinv = [tinv[p] + _dot(tinv[p].astype(BF16), bd[p]) for p in P]

    hst = [h_scr[p] for p in P]
    hbd = [block_diag(hst[p].astype(BF16)) for p in P]
    vbd = [block_diag(vb[p]) for p in P]
    x = [_dot(jnp.concatenate([at[p], a_ak[p]], axis=1), jnp.concatenate([hbd[p], vbd[p]], axis=0)) for p in P]
    ub = [_dot(tinv[p].astype(BF16), block_diag(x[p].astype(BF16))).astype(BF16) for p in P]
    y = [_dot(jnp.concatenate([rt[p], a_rb[p], a_rk[p]], axis=1),
              jnp.concatenate([hbd[p], block_diag(ub[p]), vbd[p]], axis=0)) for p in P]
    for p in P:
        ga = g_all[rows[p], sls[p]]
        g_col = jnp.where(left, jnp.sum(jnp.where(row2 == lane2, ga, 0.0), axis=1, keepdims=True),
                          jnp.sum(jnp.where(row2 == lane2 - N, ga, 0.0), axis=1, keepdims=True))
        res = _dot_tn(jnp.concatenate([bh[p], kh[p]], axis=0), jnp.concatenate([ub[p], vb[p]], axis=0))
        h_scr[p] = g_col * hst[p] + jnp.where(left, res[0:N], res[N:2 * N])

    for p in P:
        rs, sl = rows[p], sls[p]
        mean = per_head(y[p], jnp.sum) * (1.0 / N)
        yc = y[p] - mean
        var = per_head(yc * yc, jnp.sum) * (1.0 / N)
        yn = yc * lax.rsqrt(var + GN_EPS) * lng_ref[:, sl] + lnb_ref[:, sl]
        bonus = per_head(rkr[rs, sl], jnp.sum) * v[rs, sl]
        y_ref[p // (H_RWKV // 2), :, sl] = (yn + bonus) * g[rs, sl]


def _rwkv_call(rw, mu_p, w0, wdec, a0, waaa, wgate_p, k_k, k_a, r_k, lng, lnb):
    B, T, _ = rw.shape
    NC = T // CHUNK
    W = RWKV_WIDTH
    c2 = lambda b, c: (0, 0)
    rowspec = pl.BlockSpec((1, W), c2)
    NS = next(n for n in (4, 2, 1) if B % n == 0)
    return pl.pallas_call(
        _rwkv_kernel,
        grid=(B // NS, NC),
        in_specs=[pl.BlockSpec((NS, CHUNK, RW_COLS), lambda b, c: (b, c, 0)),
                  pl.BlockSpec((NS, 8, RW_COLS), lambda b, c: (b, jnp.maximum(c * (CHUNK // 8) - 1, 0), 0)),
                  pl.BlockSpec((1, RW_COLS), c2),
                  rowspec,
                  pl.BlockSpec((DECAY_LORA, W), c2),
                  rowspec,
                  pl.BlockSpec((AAA_LORA, W), c2),
                  pl.BlockSpec((GATE_PAD, W), c2),
                  rowspec, rowspec, rowspec, rowspec, rowspec],
        out_specs=pl.BlockSpec((NS, CHUNK, W), lambda b, c: (b, c, 0)),
        out_shape=jax.ShapeDtypeStruct((B, T, W), F32),
        scratch_shapes=[pltpu.VMEM((NS * H_RWKV // 2, HD_RWKV, 2 * HD_RWKV), F32)],
        compiler_params=_cp(("parallel", "arbitrary")),
        name="rwkv",
    )(rw, rw, mu_p, w0, wdec, a0, waaa, wgate_p, k_k, k_a, r_k, lng, lnb)


def _ffn_kernel(x_ref, ya_ref, yr_ref, mod_ref, gpost_ref, gpre_ref, gfpost_ref,
                wo_ref, wg_ref, wu_ref, wd_ref, o_ref, x1_scr, hf_scr, acc_scr):
    j = pl.program_id(2)

    @pl.when(j == 0)
    def _():
        for rows in _row_chunks(x_ref.shape[1]):
            mix = (_dot(ya_ref[0, rows, :].astype(BF16), wo_ref[0:ATT_WIDTH, :])
                   + _dot(yr_ref[0, rows, :].astype(BF16), wo_ref[ATT_WIDTH:, :]))
            x1 = x_ref[0, rows, :] + mod_ref[0, 2:3, :] * _rms(mix, gpost_ref[...])
            x1_scr[rows, :] = x1
            hf = _rms(x1, gpre_ref[...]) * (1.0 + mod_ref[0, 4:5, :]) + mod_ref[0, 3:4, :]
            hf_scr[rows, :] = hf.astype(BF16)
        acc_scr[...] = jnp.zeros(acc_scr.shape, F32)

    hf = hf_scr[...]
    gate = _dot(hf, wg_ref[...])
    up = _dot(hf, wu_ref[...])
    act = (gate * jax.nn.sigmoid(gate) * up).astype(BF16)
    acc_scr[...] += _dot(act, wd_ref[...])

    @pl.when(j == pl.num_programs(2) - 1)
    def _():
        o_ref[0] = x1_scr[...] + mod_ref[0, 5:6, :] * _rms(acc_scr[...], gfpost_ref[...])


def _ffn_call(x, y_att, y_rwkv, mod3, g_post, g_pre, g_fpost, wo, wg, wu, wd):
    B, T, D = x.shape
    TM = 512
    TF = D_FF // 2
    c3 = lambda b, i, j: (0, 0)
    return pl.pallas_call(
        _ffn_kernel,
        grid=(B, T // TM, D_FF // TF),
        in_specs=[pl.BlockSpec((1, TM, D), lambda b, i, j: (b, i, 0)),
                  pl.BlockSpec((1, TM, ATT_WIDTH), lambda b, i, j: (b, i, 0)),
                  pl.BlockSpec((1, TM, RWKV_WIDTH), lambda b, i, j: (b, i, 0)),
                  pl.BlockSpec((1, 6, D), lambda b, i, j: (b, 0, 0)),
                  pl.BlockSpec((1, D), c3), pl.BlockSpec((1, D), c3), pl.BlockSpec((1, D), c3),
                  pl.BlockSpec((D, D), c3),
                  pl.BlockSpec((D, TF), lambda b, i, j: (0, j)),
                  pl.BlockSpec((D, TF), lambda b, i, j: (0, j)),
                  pl.BlockSpec((TF, D), lambda b, i, j: (j, 0))],
        out_specs=pl.BlockSpec((1, TM, D), lambda b, i, j: (b, i, 0)),
        out_shape=jax.ShapeDtypeStruct((B, T, D), F32),
        scratch_shapes=[pltpu.VMEM((TM, D), F32), pltpu.VMEM((TM, D), BF16), pltpu.VMEM((TM, D), F32)],
        compiler_params=_cp(("parallel", "parallel", "arbitrary")),
        name="ffn",
    )(x, y_att, y_rwkv, mod3, g_post, g_pre, g_fpost, wo, wg, wu, wd)


def _pad_cols(w, n):
    return jnp.pad(w, ((0, 0), (0, n - w.shape[1])))


def _layer(x, mod3, rel_bias, mix_pre_norm, mix_post_norm, ffn_pre_norm, ffn_post_norm,
           w_in, q_norm, w_uq, w_idx_q, kv_norm, idx_k_norm, w_uk, w_uv,
           mu_shift, w0, w_decay_up, a0, w_aaa_up, w_gate_up, k_k, k_a, r_k, ln_x_gain, ln_x_bias,
           w_out, w_ffn_gate, w_ffn_up, w_ffn_down):
    B, T, D = x.shape
    att_cols = Q_LORA + KV_LATENT + IDX_DIM + IDX_HEADS
    n_main = 3 * RWKV_WIDTH + DECAY_LORA + AAA_LORA
    w_att = _pad_cols(w_in[:, :att_cols], ATT_PAD)
    w_rw = w_in[:, att_cols:]
    w_in_p = jnp.concatenate([w_att, w_rw[:, :n_main], _pad_cols(w_rw[:, n_main:], GATE_PAD)], axis=1).astype(BF16)
    mu_p = jnp.concatenate([mu_shift[:n_main], jnp.pad(mu_shift[n_main:], (0, GATE_PAD - GATE_LORA))]).reshape(1, RW_COLS)
    wgate_p = jnp.pad(w_gate_up, ((0, GATE_PAD - GATE_LORA), (0, 0))).astype(BF16)

    att, rw = _inproj_call(x, mod3, mix_pre_norm.reshape(1, D), w_in_p)

    wuq = w_uq.reshape(Q_LORA, ATT_WIDTH).astype(BF16)
    wiq = w_idx_q.reshape(Q_LORA, IDX_HEADS * IDX_DIM).astype(BF16)
    wukt = jnp.transpose(w_uk, (0, 2, 1)).astype(BF16)
    qabs, iq, ik, ckv, ckvt, iwt = _dsaprep_call(att, q_norm.reshape(1, -1), kv_norm.reshape(1, -1),
                                                 idx_k_norm.reshape(1, -1), wuq, wiq, wukt)
    bias_tiles = _bias_call(rel_bias)
    wuvt = jnp.transpose(w_uv, (0, 2, 1)).astype(BF16)
    topk = min(TOPK_MAX, T // 4)
    y_att = _dsa_call(iq, iwt, qabs, ik, ckv, ckvt, bias_tiles, wuvt, topk)

    row = lambda z: z.reshape(1, RWKV_WIDTH)
    y_rwkv = _rwkv_call(rw, mu_p, row(w0), w_decay_up.astype(BF16), row(a0), w_aaa_up.astype(BF16),
                        wgate_p, row(k_k), row(k_a), row(r_k), row(ln_x_gain), row(ln_x_bias))

    return _ffn_call(x, y_att, y_rwkv, mod3, mix_post_norm.reshape(1, D), ffn_pre_norm.reshape(1, D),
                     ffn_post_norm.reshape(1, D), w_out.astype(BF16), w_ffn_gate.astype(BF16),
                     w_ffn_up.astype(BF16), w_ffn_down.astype(BF16))


def kernel(x, c, rel_bias, ada_w, ada_b, mix_pre_norm, mix_post_norm, ffn_pre_norm, ffn_post_norm, w_in, q_norm, w_uq, w_idx_q, kv_norm, idx_k_norm, w_uk, w_uv, mu_shift, w0, w_decay_up, a0, w_aaa_up, w_gate_up, k_k, k_a, r_k, ln_x_gain, ln_x_bias, w_out, w_ffn_gate, w_ffn_up, w_ffn_down):
    B, T, D = x.shape
    assert D == D_MODEL and T % 512 == 0 and T // 4 >= 1
    layer_params = (mix_pre_norm, mix_post_norm, ffn_pre_norm, ffn_post_norm,
                    w_in, q_norm, w_uq, w_idx_q, kv_norm, idx_k_norm, w_uk, w_uv,
                    mu_shift, w0, w_decay_up, a0, w_aaa_up, w_gate_up, k_k, k_a, r_k, ln_x_gain, ln_x_bias,
                    w_out, w_ffn_gate, w_ffn_up, w_ffn_down)
    for l in range(ada_w.shape[0]):
        mod3 = _mod_call(c, ada_w[l], ada_b[l]).reshape(B, 6, D)
        x = _layer(x, mod3, rel_bias, *[p[l] for p in layer_params])
    return x
```

```python
import functools
import math

import numpy as np
import jax
import jax.numpy as jnp
from jax import lax
from jax.experimental import pallas as pl
from jax.experimental.pallas import tpu as pltpu

F32 = jnp.float32
BF16 = jnp.bfloat16

D_MODEL = 1024
ATT_WIDTH = 512
RWKV_WIDTH = 512
HD_ATT = 64
H_ATT = 8
HD_RWKV = 64
H_RWKV = 8
Q_LORA = 256
KV_LATENT = 128
IDX_HEADS = 8
IDX_DIM = 64
TOPK_MAX = 256
NUM_BUCKETS = 32
MAX_DISTANCE = 128
DECAY_LORA = 64
AAA_LORA = 64
GATE_LORA = 160
GN_EPS = 64e-5
D_FF = 2816
NORM_EPS = 1e-6

ATT_PAD = 512
GATE_PAD = 256
RW_COLS = 3 * RWKV_WIDTH + DECAY_LORA + AAA_LORA + GATE_PAD
OFF_WD = 3 * RWKV_WIDTH
OFF_AD = OFF_WD + DECAY_LORA
OFF_GD = OFF_AD + AAA_LORA

TQ = 128
CHUNK = 64
NEG_BIG = -1e30
LOWEST = -3.0e38
LOG2E = math.log2(math.e)
VMEM_LIMIT = 56 * 1024 * 1024


def _cp(sem):
    return pltpu.CompilerParams(dimension_semantics=sem, vmem_limit_bytes=VMEM_LIMIT)


def _dot(a, b):
    return jnp.dot(a, b, preferred_element_type=F32)


def _dot_nt(a, b):
    return lax.dot_general(a, b, (((1,), (1,)), ((), ())), preferred_element_type=F32)


def _dot_tn(a, b):
    return lax.dot_general(a, b, (((0,), (0,)), ((), ())), preferred_element_type=F32)


def _rms(z, gain):
    return z * lax.rsqrt(jnp.mean(z * z, axis=-1, keepdims=True) + NORM_EPS) * gain


def _row_chunks(n, chunk=256):
    return [slice(r, r + chunk) for r in range(0, n, chunk)]


def _mod_kernel(c_ref, w_ref, b_ref, o_ref):
    c = c_ref[...]
    s = c * jax.nn.sigmoid(c)
    o_ref[...] = _dot(s.astype(BF16), w_ref[...].astype(BF16)) + b_ref[...]


def _mod_call(c, ada_w, ada_b):
    B, D = c.shape
    N = ada_w.shape[1]
    TN = 1536
    return pl.pallas_call(
        _mod_kernel,
        grid=(N // TN,),
        in_specs=[pl.BlockSpec((B, D), lambda j: (0, 0)),
                  pl.BlockSpec((D, TN), lambda j: (0, j)),
                  pl.BlockSpec((1, TN), lambda j: (0, j))],
        out_specs=pl.BlockSpec((B, TN), lambda j: (0, j)),
        out_shape=jax.ShapeDtypeStruct((B, N), F32),
        compiler_params=_cp(("parallel",)),
        name="mod",
    )(c, ada_w, ada_b.reshape(1, N))


def _inproj_kernel(x_ref, mod_ref, g_ref, w_ref, att_ref, rw_ref):
    sh = mod_ref[0, 0:1, :]
    sc = mod_ref[0, 1:2, :]
    h = (_rms(x_ref[0], g_ref[...]) * (1.0 + sc) + sh).astype(BF16)
    att_ref[0] = _dot(h, w_ref[:, 0:ATT_PAD])
    for n0 in range(0, RW_COLS, 384):
        rw_ref[0, :, n0:n0 + 384] = _dot(h, w_ref[:, ATT_PAD + n0:ATT_PAD + n0 + 384])


def _inproj_call(x, mod3, gain, w_in_p):
    B, T, D = x.shape
    TM = 1024 if T % 1024 == 0 else 512
    NP = w_in_p.shape[1]
    return pl.pallas_call(
        _inproj_kernel,
        grid=(B, T // TM),
        in_specs=[pl.BlockSpec((1, TM, D), lambda b, i: (b, i, 0)),
                  pl.BlockSpec((1, 6, D), lambda b, i: (b, 0, 0)),
                  pl.BlockSpec((1, D), lambda b, i: (0, 0)),
                  pl.BlockSpec((D, NP), lambda b, i: (0, 0))],
        out_specs=[pl.BlockSpec((1, TM, ATT_PAD), lambda b, i: (b, i, 0)),
                   pl.BlockSpec((1, TM, RW_COLS), lambda b, i: (b, i, 0))],
        out_shape=[jax.ShapeDtypeStruct((B, T, ATT_PAD), F32),
                   jax.ShapeDtypeStruct((B, T, RW_COLS), F32)],
        compiler_params=_cp(("parallel", "parallel")),
        name="inproj",
    )(x, mod3, gain, w_in_p)


def _dsaprep_kernel(att_ref, qn_ref, kvn_ref, ikn_ref, wuq_ref, wiq_ref, wukt_ref,
                    qabs_ref, iq_ref, ik_ref, ckv_ref, ckvt_ref, iwt_ref):
    att = att_ref[0]
    tm = att.shape[0]
    o1 = Q_LORA
    o2 = o1 + KV_LATENT
    o3 = o2 + IDX_DIM
    cq = _rms(att[:, 0:o1], qn_ref[...]).astype(BF16)
    ckv = _rms(att[:, o1:o2], kvn_ref[...])
    ckv_ref[0] = ckv.astype(BF16)
    ik_ref[0] = _rms(att[:, o2:o3], ikn_ref[...]).astype(BF16)
    tail = att[:, o2:ATT_PAD]
    w_off = o3 - o2
    for sb in range(tm // TQ):
        rows = slice(sb * TQ, (sb + 1) * TQ)
        ckvt_ref[0, sb] = ckv[rows, :].T.astype(BF16)
        iwt_ref[0, :, rows] = tail[rows, :].T[w_off:w_off + IDX_HEADS, :] * (IDX_HEADS ** -0.5 * IDX_DIM ** -0.5)
    q = _dot(cq, wuq_ref[...])
    iqv = _dot(cq, wiq_ref[...])
    for h in range(H_ATT):
        qh = q[:, h * HD_ATT:(h + 1) * HD_ATT].astype(BF16)
        qa = _dot(qh, wukt_ref[h]) * (HD_ATT ** -0.5 * LOG2E)
        for sb in range(tm // TQ):
            qabs_ref[0, sb, :, h * TQ:(h + 1) * TQ] = qa[sb * TQ:(sb + 1) * TQ, :].T.astype(BF16)
    per = TQ // IDX_DIM
    for g in range(IDX_HEADS // per):
        for sb in range(tm // TQ):
            blk = iqv[sb * TQ:(sb + 1) * TQ, g * TQ:(g + 1) * TQ].T.astype(BF16)
            for u in range(per):
                h = g * per + u
                iq_ref[0, sb, :, h * TQ:(h + 1) * TQ] = blk[u * IDX_DIM:(u + 1) * IDX_DIM, :]


def _dsaprep_call(att, q_norm, kv_norm, idx_k_norm, wuq, wiq, wukt):
    B, T, _ = att.shape
    TM = 1024 if T % 1024 == 0 else 512
    const2 = lambda b, i: (0, 0)
    return pl.pallas_call(
        _dsaprep_kernel,
        grid=(B, T // TM),
        in_specs=[pl.BlockSpec((1, TM, ATT_PAD), lambda b, i: (b, i, 0)),
                  pl.BlockSpec((1, Q_LORA), const2),
                  pl.BlockSpec((1, KV_LATENT), const2),
                  pl.BlockSpec((1, IDX_DIM), const2),
                  pl.BlockSpec((Q_LORA, ATT_WIDTH), const2),
                  pl.BlockSpec((Q_LORA, IDX_HEADS * IDX_DIM), const2),
                  pl.BlockSpec((H_ATT, HD_ATT, KV_LATENT), lambda b, i: (0, 0, 0))],
        out_specs=[pl.BlockSpec((1, TM // TQ, KV_LATENT, H_ATT * TQ), lambda b, i: (b, i, 0, 0)),
                   pl.BlockSpec((1, TM // TQ, IDX_DIM, IDX_HEADS * TQ), lambda b, i: (b, i, 0, 0)),
                   pl.BlockSpec((1, TM, IDX_DIM), lambda b, i: (b, i, 0)),
                   pl.BlockSpec((1, TM, KV_LATENT), lambda b, i: (b, i, 0)),
                   pl.BlockSpec((1, TM // TQ, KV_LATENT, TQ), lambda b, i: (b, i, 0, 0)),
                   pl.BlockSpec((1, IDX_HEADS, TM), lambda b, i: (b, 0, i))],
        out_shape=[jax.ShapeDtypeStruct((B, T // TQ, KV_LATENT, H_ATT * TQ), BF16),
                   jax.ShapeDtypeStruct((B, T // TQ, IDX_DIM, IDX_HEADS * TQ), BF16),
                   jax.ShapeDtypeStruct((B, T, IDX_DIM), BF16),
                   jax.ShapeDtypeStruct((B, T, KV_LATENT), BF16),
                   jax.ShapeDtypeStruct((B, T // TQ, KV_LATENT, TQ), BF16),
                   jax.ShapeDtypeStruct((B, IDX_HEADS, T), F32)],
        compiler_params=_cp(("parallel", "parallel")),
        name="dsaprep",
    )(att, q_norm, kv_norm, idx_k_norm, wuq, wiq, wukt)


def _t5_bucket_np(rel):
    max_exact = NUM_BUCKETS // 2
    nf = np.maximum(rel, 1).astype(np.float32)
    large = max_exact + (np.log(nf / max_exact) / math.log(MAX_DISTANCE / max_exact)
                         * (NUM_BUCKETS - max_exact)).astype(np.int32)
    large = np.minimum(large, NUM_BUCKETS - 1)
    return np.where(rel < max_exact, rel, large).astype(np.int32)


def _near_bucket_tiles():
    s = np.arange(TQ)[:, None]
    t = np.arange(TQ)[None, :]
    tiles = [_t5_bucket_np(np.maximum(t - s + TQ * d, 0)) for d in range(2)]
    assert _t5_bucket_np(np.array([TQ + 1]))[0] == NUM_BUCKETS - 1
    return np.stack(tiles)


def _bias_kernel(rb_ref, bk_ref, o_ref):
    for d in range(2):
        bk = bk_ref[d]
        for h in range(H_ATT):
            def body(b, acc):
                return jnp.where(bk == b, rb_ref[b, h], acc)
            acc = lax.fori_loop(0, NUM_BUCKETS, body, jnp.zeros((TQ, TQ), F32))
            o_ref[d, h] = (acc - rb_ref[NUM_BUCKETS - 1, h]) * LOG2E
    o_ref[2] = jnp.zeros((H_ATT, TQ, TQ), F32)


def _bias_call(rel_bias):
    buckets = jnp.asarray(_near_bucket_tiles())
    return pl.pallas_call(
        _bias_kernel,
        in_specs=[pl.BlockSpec(memory_space=pltpu.SMEM),
                  pl.BlockSpec(memory_space=pltpu.VMEM)],
        out_specs=pl.BlockSpec(memory_space=pltpu.VMEM),
        out_shape=jax.ShapeDtypeStruct((3, H_ATT, TQ, TQ), F32),
        name="bias",
    )(rel_bias, buckets)


KV = TQ // 8
FREE_PROBES = 11
COUNT_PROBES = 6


def _key_reduce(x, op):
    return op(op(x, axis=0), axis=0, keepdims=True)


def _dsa_kernel(iq_ref, iwt_ref, qabs_ref, ik_ref, ckv_ref, ckvt_ref, bias_ref, wuvt_ref, y_ref,
                score_scr, stage_a, stage_b, pv_a, pv_b, alpha_a, alpha_b, acc_scr, *, topk):
    i = pl.program_id(1)
    kf = float(topk)
    kidx = (lax.broadcasted_iota(jnp.int32, (KV, 8, TQ), 0) * 8
            + lax.broadcasted_iota(jnp.int32, (KV, 8, TQ), 1))
    qidx = lax.broadcasted_iota(jnp.int32, (KV, 8, TQ), 2)
    n_pairs = i // 2 + 1

    iwt = iwt_ref[0]
    w8 = [jnp.broadcast_to(iwt[h:h + 1, :], (8, TQ))[None] for h in range(IDX_HEADS)]
    iq2 = iq_ref[0, 0]

    n_trips = (n_pairs + 1) // 2
    stage = (stage_a, stage_b)

    def stage_pair(q, lhs_ref, rhs, dst):
        nxt = jnp.minimum(q, n_pairs - 1)
        rows = lhs_ref[0, pl.ds(pl.multiple_of(2 * nxt * TQ, 2 * TQ), 2 * TQ), :]
        dst[...] = _dot(rows, rhs).reshape(2, TQ, H_ATT * TQ)

    def two_halves(half):
        def trip(r, carry):
            carry = half(2 * r, 0, carry)
            return lax.cond(2 * r + 1 < n_pairs, lambda c: half(2 * r + 1, 1, c), lambda c: c, carry)
        return trip

    def score_half(q, par, carry):
        rmax, rmin = carry
        rows = ik_ref[0, pl.ds(pl.multiple_of(2 * q * TQ, 2 * TQ), 2 * TQ), :]
        for blk in range(2):
            j = 2 * q + blk
            sc = jnp.zeros((KV, 8, TQ), F32)
            for hp in range(IDX_HEADS // 2):
                s2 = _dot(rows[blk * TQ:(blk + 1) * TQ], iq2[:, hp * 2 * TQ:(hp + 1) * 2 * TQ])
                for h in (2 * hp, 2 * hp + 1):
                    s = s2[:, (h % 2) * TQ:(h % 2 + 1) * TQ].reshape(KV, 8, TQ)
                    sc = sc + jnp.maximum(s, 0.0) * w8[h]
            causal = kidx + (j - i) * TQ <= qidx
            score_scr[j] = jnp.where(causal, sc, -jnp.inf)
            rmax = jnp.maximum(rmax, jnp.max(jnp.where(causal, sc, -jnp.inf), axis=0))
            rmin = jnp.minimum(rmin, jnp.min(jnp.where(causal, sc, jnp.inf), axis=0))
        return rmax, rmin

    rmax, rmin = lax.fori_loop(0, n_trips, two_halves(score_half),
                               (jnp.full((8, TQ), -jnp.inf, F32), jnp.full((8, TQ), jnp.inf, F32)))
    smax = jnp.max(rmax, axis=0, keepdims=True)
    smin = jnp.min(rmin, axis=0, keepdims=True)

    nvalid = (i * TQ + lax.broadcasted_iota(jnp.int32, (1, TQ), 1) + 1).astype(F32)
    needs_thr = nvalid > kf

    def probe(v):
        vb = jnp.broadcast_to(v, (8, TQ))[None, None]

        def body(q, st):
            cnt, dn, up = st
            for j in (2 * q, 2 * q + 1):
                s = score_scr[j].reshape(4, KV // 4, 8, TQ)
                ge = s >= vb
                cnt = cnt + jnp.sum(jnp.where(ge, 1.0, 0.0), axis=1)
                dn = jnp.minimum(dn, jnp.min(jnp.where(ge, s, jnp.inf), axis=1))
                up = jnp.maximum(up, jnp.max(jnp.where(ge, -jnp.inf, s), axis=1))
            return cnt, dn, up
        cnt, dn, up = lax.fori_loop(0, n_pairs, body, (jnp.zeros((4, 8, TQ), F32),
                                                       jnp.full((4, 8, TQ), jnp.inf, F32),
                                                       jnp.full((4, 8, TQ), -jnp.inf, F32)))
        return (jnp.sum(jnp.sum(cnt, axis=0), axis=0, keepdims=True),
                jnp.min(jnp.min(dn, axis=0), axis=0, keepdims=True),
                jnp.max(jnp.max(up, axis=0), axis=0, keepdims=True))

    def probe_count(v):
        vb = jnp.broadcast_to(v, (8, TQ))[None, None]

        def body(q, cnt):
            for j in (2 * q, 2 * q + 1):
                s = score_scr[j].reshape(4, KV // 4, 8, TQ)
                cnt = cnt + jnp.sum(jnp.where(s >= vb, 1.0, 0.0), axis=1)
            return cnt
        cnt = lax.fori_loop(0, n_pairs, body, jnp.zeros((4, 8, TQ), F32))
        return jnp.sum(jnp.sum(cnt, axis=0), axis=0, keepdims=True)

    def logit(c):
        return jnp.log((c + 0.5) / (nvalid - c + 0.5))

    g_k = logit(jnp.full((1, TQ), kf - 0.5, F32))

    def search_step(st, snap=True):
        lo, hi, clo, ghi, slo, shi = st
        is_open = lo < hi
        g_lo = logit(clo)
        g_hi = logit(ghi)
        phi = jnp.clip((g_lo - g_k) / (g_lo - g_hi), 0.02, 0.98)
        stalled = (slo >= 2.0) | (shi >= 2.0)
        mid = jnp.where(stalled, 0.5 * lo + 0.5 * hi, lo + (hi - lo) * phi)
        mid = jnp.where((mid > lo) & (mid <= hi), mid, hi)
        if snap:
            c, dn, up = probe(mid)
        else:
            c = probe_count(mid)
            dn = up = mid
        ge = c >= kf
        move_lo = is_open & ge
        move_hi = is_open & jnp.logical_not(ge)
        lo = jnp.where(move_lo, dn, lo)
        clo = jnp.where(move_lo, c, clo)
        hi = jnp.where(move_hi, up, hi)
        ghi = jnp.where(move_hi, c, ghi)
        slo = jnp.where(stalled | ge, 0.0, slo + 1.0)
        shi = jnp.where(stalled | jnp.logical_not(ge), 0.0, shi + 1.0)
        return lo, hi, clo, ghi, slo, shi

    def n_open(st):
        return jnp.sum(jnp.where(st[0] < st[1], 1.0, 0.0))

    def checked_step(carry):
        st = search_step(carry[0])
        return st, n_open(st), carry[2] + 1

    search_cap = ik_ref.shape[1] + 2
    zero = jnp.zeros((1, TQ), F32)
    st = (smin, jnp.where(needs_thr, smax, smin), nvalid, zero, zero, zero)
    st = lax.fori_loop(0, COUNT_PROBES, lambda _, s: search_step(s, snap=False), st)
    st = lax.fori_loop(COUNT_PROBES, FREE_PROBES, lambda _, s: search_step(s), st)
    st, _, _ = lax.while_loop(lambda c: jnp.logical_and(c[1] > 0.0, c[2] < search_cap), checked_step,
                              (st, n_open(st), jnp.int32(0)))
    thr, _, clo, chi, _, _ = st

    tie = needs_thr & (clo > kf)
    n_tie = jnp.sum(jnp.where(tie, 1.0, 0.0))

    @pl.when(n_tie > 0.0)
    def _():
        r2 = lax.broadcasted_iota(jnp.int32, (TQ, TQ), 0)
        c2 = lax.broadcasted_iota(jnp.int32, (TQ, TQ), 1)
        lower = jnp.where(c2 <= r2, 1.0, 0.0).astype(BF16)
        thrb = jnp.broadcast_to(thr, (TQ, TQ))
        tieb = jnp.broadcast_to(tie, (TQ, TQ))
        needb = jnp.broadcast_to(kf - chi, (TQ, TQ))

        def body(j, carry):
            s = score_scr[j].reshape(TQ, TQ)
            eq = (s == thrb) & tieb
            pref = _dot(lower, jnp.where(eq, 1.0, 0.0).astype(BF16)) + carry
            score_scr[j] = jnp.where(eq & (pref > needb), -jnp.inf, s).reshape(KV, 8, TQ)
            return jnp.broadcast_to(pref[TQ - 1:TQ, :], (TQ, TQ))
        lax.fori_loop(0, i + 1, body, jnp.zeros((TQ, TQ), F32))

    thr_eff = jnp.broadcast_to(jnp.where(needs_thr, thr, LOWEST), (8, TQ))[None]

    acc_scr[...] = jnp.zeros(acc_scr.shape, F32)
    qa2 = qabs_ref[0, 0]

    pv = (pv_a, pv_b)
    alpha_st = (alpha_a, alpha_b)

    def apply_pending(par):
        acc_scr[...] = alpha_st[par][...] * acc_scr[...] + pv[par][...]

    def attend_variant(q, par, carry, near):
        m = list(carry)
        stage_pair(q + 1, ckv_ref, qa2, stage[1 - par])
        js = (2 * q, 2 * q + 1)
        sel = [score_scr[j] >= thr_eff for j in js]
        d = [jnp.clip(i - j, 0, 2) for j in js]
        kvt = jnp.concatenate([jnp.concatenate([ckvt_ref[0, js[0]], ckvt_ref[0, js[1]]], axis=1),
                               jnp.ones((8, 2 * TQ), BF16)], axis=0)
        ps, alphas = [], []
        for h in range(H_ATT):
            lg = [stage[par][b, :, h * TQ:(h + 1) * TQ] for b in range(2)]
            if near:
                lg = [lg[b] + bias_ref[d[b], h] for b in range(2)]
            lg = [jnp.where(sel[b], lg[b].reshape(KV, 8, TQ), NEG_BIG) for b in range(2)]
            m_new = jnp.maximum(m[h], jnp.maximum(_key_reduce(lg[0], jnp.max), _key_reduce(lg[1], jnp.max)))
            alpha = jnp.exp2(m[h] - m_new)
            mb = jnp.broadcast_to(m_new, (8, TQ))[None]
            p = [jnp.exp2(lg[b] - mb) for b in range(2)]
            m[h] = m_new
            alphas.append(alpha)
            ps.append(jnp.concatenate([p[0].reshape(TQ, TQ), p[1].reshape(TQ, TQ)], axis=0).astype(BF16))
            if h % 2 == 1:
                cols = slice((h - 1) * TQ, (h + 1) * TQ)
                pv[par][:, cols] = _dot(kvt, jnp.concatenate(ps[h - 1:h + 1], axis=1))
        alpha_st[par][...] = jnp.concatenate(alphas, axis=1)
        apply_pending(1 - par)
        return tuple(m)

    def attend_half(q, par, carry):
        return lax.cond(2 * q + 1 >= i - 1,
                        lambda c: attend_variant(q, par, c, True),
                        lambda c: attend_variant(q, par, c, False), carry)

    pv_b[...] = jnp.zeros(pv_b.shape, F32)
    alpha_b[...] = jnp.ones(alpha_b.shape, F32)
    stage_pair(0, ckv_ref, qa2, stage[0])
    m0 = tuple(jnp.full((1, TQ), NEG_BIG, F32) for _ in range(H_ATT))
    lax.fori_loop(0, n_trips, two_halves(attend_half), m0)
    for par in range(2):
        @pl.when((n_pairs - 1) % 2 == par)
        def _():
            apply_pending(par)

    yt = []
    for h in range(H_ATT):
        cols = slice(h * TQ, (h + 1) * TQ)
        o = acc_scr[0:KV_LATENT, cols] / acc_scr[KV_LATENT:KV_LATENT + 1, cols]
        yt.append(_dot(wuvt_ref[h], o.astype(BF16)))
    for g in range(ATT_WIDTH // TQ):
        per = TQ // HD_ATT
        y_ref[0, :, g * TQ:(g + 1) * TQ] = jnp.concatenate(yt[g * per:(g + 1) * per], axis=0).T


def _dsa_call(iq, iwt, qabs, ik, ckv, ckvt, bias_tiles, wuvt, topk):
    B, T, _ = ik.shape
    NT = T // TQ
    return pl.pallas_call(
        functools.partial(_dsa_kernel, topk=topk),
        grid=(B, NT),
        in_specs=[pl.BlockSpec((1, 1, IDX_DIM, IDX_HEADS * TQ), lambda b, i: (b, i, 0, 0)),
                  pl.BlockSpec((1, IDX_HEADS, TQ), lambda b, i: (b, 0, i)),
                  pl.BlockSpec((1, 1, KV_LATENT, H_ATT * TQ), lambda b, i: (b, i, 0, 0)),
                  pl.BlockSpec((1, T, IDX_DIM), lambda b, i: (b, 0, 0)),
                  pl.BlockSpec((1, T, KV_LATENT), lambda b, i: (b, 0, 0)),
                  pl.BlockSpec((1, NT, KV_LATENT, TQ), lambda b, i: (b, 0, 0, 0)),
                  pl.BlockSpec((3, H_ATT, TQ, TQ), lambda b, i: (0, 0, 0, 0)),
                  pl.BlockSpec((H_ATT, HD_ATT, KV_LATENT), lambda b, i: (0, 0, 0))],
        out_specs=pl.BlockSpec((1, TQ, ATT_WIDTH), lambda b, i: (b, i, 0)),
        out_shape=jax.ShapeDtypeStruct((B, T, ATT_WIDTH), F32),
        scratch_shapes=[pltpu.VMEM((NT, KV, 8, TQ), F32),
                        pltpu.VMEM((2, TQ, H_ATT * TQ), F32),
                        pltpu.VMEM((2, TQ, H_ATT * TQ), F32),
                        pltpu.VMEM((KV_LATENT + 8, H_ATT * TQ), F32),
                        pltpu.VMEM((KV_LATENT + 8, H_ATT * TQ), F32),
                        pltpu.VMEM((1, H_ATT * TQ), F32),
                        pltpu.VMEM((1, H_ATT * TQ), F32),
                        pltpu.VMEM((KV_LATENT + 8, H_ATT * TQ), F32)],
        compiler_params=_cp(("parallel", "arbitrary")),
        name="dsa",
    )(iq, iwt, qabs, ik, ckv, ckvt, bias_tiles, wuvt)


def _rwkv_kernel(p_ref, prev_ref, mu_ref, w0_ref, wdec_ref, a0_ref, waaa_ref, wgate_ref,
                 kk_ref, ka_ref, rk_ref, lng_ref, lnb_ref, y_ref, h_scr):
    c = pl.program_id(1)
    C = CHUNK
    N = HD_RWKV

    @pl.when(c == 0)
    def _():
        h_scr[...] = jnp.zeros(h_scr.shape, F32)

    NS = p_ref.shape[0]
    p = p_ref[...].reshape(NS * C, RW_COLS)
    rid = lax.broadcasted_iota(jnp.int32, (NS * C, 1), 0)
    p_prev = pltpu.roll(p, 1, 0)
    for s in range(NS):
        p_prev = jnp.where(rid == s * C, jnp.where(c > 0, prev_ref[s, 7:8, :], 0.0), p_prev)
    p = p + mu_ref[...] * (p_prev - p)

    W = RWKV_WIDTH
    r = p[:, 0:W]
    k = p[:, W:2 * W]
    v = p[:, 2 * W:3 * W]
    wd = p[:, OFF_WD:OFF_WD + DECAY_LORA]
    ad = p[:, OFF_AD:OFF_AD + AAA_LORA]
    gd = p[:, OFF_GD:OFF_GD + GATE_PAD]

    w_log = -jax.nn.softplus(-(w0_ref[...] + _dot(jnp.tanh(wd).astype(BF16), wdec_ref[...]))) - 0.5
    logw = -jnp.exp(w_log)
    a = jax.nn.sigmoid(a0_ref[...] + _dot(ad.astype(BF16), waaa_ref[...]))
    g = _dot(jax.nn.sigmoid(gd).astype(BF16), wgate_ref[...])

    ra = lax.broadcasted_iota(jnp.int32, (NS * C, NS * C), 0)
    ca = lax.broadcasted_iota(jnp.int32, (NS * C, NS * C), 1)
    same_seq = (ra >= ca) & (ra - ca <= ra % C)
    cum = jnp.dot(jnp.where(same_seq, 1.0, 0.0), logw, preferred_element_type=F32,
                  precision=lax.Precision.HIGHEST)
    cum_last = cum[C - 1:C, :]
    for s in range(1, NS):
        cum_last = jnp.where(rid >= s * C, cum[(s + 1) * C - 1:(s + 1) * C, :], cum_last)
    g_in = jnp.exp(cum - logw)
    g_t = jnp.exp(cum)
    g_inv = jnp.exp(-cum)
    g_end = jnp.exp(cum_last - cum)
    g_all = jnp.exp(cum_last)

    kkf = k * kk_ref[...]
    kmod = k * (1.0 + (a - 1.0) * ka_ref[...])
    rkr = r * kmod * rk_ref[...]

    P = range(NS * H_RWKV // 2)
    rows = [slice((p // (H_RWKV // 2)) * C, (p // (H_RWKV // 2) + 1) * C) for p in P]
    sls = [slice((p % (H_RWKV // 2)) * 2 * N, (p % (H_RWKV // 2) + 1) * 2 * N) for p in P]
    lane2 = lax.broadcasted_iota(jnp.int32, (C, 2 * N), 1)
    row2 = lax.broadcasted_iota(jnp.int32, (C, 2 * N), 0)
    left = lane2 < N
    col2 = jnp.where(left, lane2, lane2 - N)
    strict2 = row2 > col2
    incl2 = row2 >= col2
    eye2 = row2 == col2

    def per_head(x, op):
        return jnp.where(left, op(jnp.where(left, x, 0.0), axis=-1, keepdims=True),
                         op(jnp.where(left, 0.0, x), axis=-1, keepdims=True))

    def block_diag(x):
        zero = jnp.zeros_like(x)
        return jnp.concatenate([jnp.where(left, x, zero), jnp.where(left, zero, x)], axis=0)

    at, rt, bh, kh, vb, g4 = [], [], [], [], [], []
    for p in P:
        rs, sl = rows[p], sls[p]
        kk_p = kkf[rs, sl]
        kk_p = kk_p / jnp.maximum(jnp.sqrt(per_head(kk_p * kk_p, jnp.sum)), 1e-12)
        k_p = kmod[rs, sl]
        bvec = kk_p * a[rs, sl]
        at_p = ((-kk_p) * g_in[rs, sl]).astype(BF16)
        rt_p = (r[rs, sl] * g_t[rs, sl]).astype(BF16)
        bt_p = (bvec * g_inv[rs, sl]).astype(BF16)
        kt_p = (k_p * g_inv[rs, sl]).astype(BF16)
        wt = jnp.concatenate([block_diag(bt_p), block_diag(kt_p)], axis=0)
        g4.append(_dot_nt(jnp.concatenate([at_p, rt_p], axis=0), wt))
        at.append(at_p)
        rt.append(rt_p)
        bh.append((bvec * g_end[rs, sl]).astype(BF16))
        kh.append((k_p * g_end[rs, sl]).astype(BF16))
        vb.append(v[rs, sl].astype(BF16))

    a_ab = [jnp.where(strict2, g4[p][0:C, 0:2 * C], 0.0) for p in P]
    a_ak = [jnp.where(strict2, g4[p][0:C, 2 * C:4 * C], 0.0).astype(BF16) for p in P]
    a_rb = [jnp.where(incl2, g4[p][C:2 * C, 0:2 * C], 0.0).astype(BF16) for p in P]
    a_rk = [jnp.where(incl2, g4[p][C:2 * C, 2 * C:4 * C], 0.0).astype(BF16) for p in P]

    pw = a_ab
    tinv = [jnp.where(eye2, 1.0, 0.0) for p in P]
    for level in range(6):
        pwb = [pw[p].astype(BF16) for p in P]
        bd = [block_diag(pwb[p]) for p in P]
        if level < 5:
            res = [_dot(jnp.concatenate([pwb[p], tinv[p].astype(BF16)], axis=0), bd[p]) for p in P]
            pw = [res[p][0:C] for p in P]
            tinv = [tinv[p] + res[p][C:2 * C] for p in P]
        else:
            tinv = [tinv[p] + _dot(tinv[p].astype(BF16), bd[p]) for p in P]

    hst = [h_scr[p] for p in P]
    hbd = [block_diag(hst[p].astype(BF16)) for p in P]
    vbd = [block_diag(vb[p]) for p in P]
    x = [_dot(jnp.concatenate([at[p], a_ak[p]], axis=1), jnp.concatenate([hbd[p], vbd[p]], axis=0)) for p in P]
    ub = [_dot(tinv[p].astype(BF16), block_diag(x[p].astype(BF16))).astype(BF16) for p in P]
    y = [_dot(jnp.concatenate([rt[p], a_rb[p], a_rk[p]], axis=1),
              jnp.concatenate([hbd[p], block_diag(ub[p]), vbd[p]], axis=0)) for p in P]
    for p in P:
        ga = g_all[rows[p], sls[p]]
        g_col = jnp.where(left, jnp.sum(jnp.where(row2 == lane2, ga, 0.0), axis=1, keepdims=True),
                          jnp.sum(jnp.where(row2 == lane2 - N, ga, 0.0), axis=1, keepdims=True))
        res = _dot_tn(jnp.concatenate([bh[p], kh[p]], axis=0), jnp.concatenate([ub[p], vb[p]], axis=0))
        h_scr[p] = g_col * hst[p] + jnp.where(left, res[0:N], res[N:2 * N])

    for p in P:
        rs, sl = rows[p], sls[p]
        mean = per_head(y[p], jnp.sum) * (1.0 / N)
        yc = y[p] - mean
        var = per_head(yc * yc, jnp.sum) * (1.0 / N)
        yn = yc * lax.rsqrt(var + GN_EPS) * lng_ref[:, sl] + lnb_ref[:, sl]
        bonus = per_head(rkr[rs, sl], jnp.sum) * v[rs, sl]
        y_ref[p // (H_RWKV // 2), :, sl] = (yn + bonus) * g[rs, sl]


def _rwkv_call(rw, mu_p, w0, wdec, a0, waaa, wgate_p, k_k, k_a, r_k, lng, lnb):
    B, T, _ = rw.shape
    NC = T // CHUNK
    W = RWKV_WIDTH
    c2 = lambda b, c: (0, 0)
    rowspec = pl.BlockSpec((1, W), c2)
    NS = next(n for n in (4, 2, 1) if B % n == 0)
    return pl.pallas_call(
        _rwkv_kernel,
        grid=(B // NS, NC),
        in_specs=[pl.BlockSpec((NS, CHUNK, RW_COLS), lambda b, c: (b, c, 0)),
                  pl.BlockSpec((NS, 8, RW_COLS), lambda b, c: (b, jnp.maximum(c * (CHUNK // 8) - 1, 0), 0)),
                  pl.BlockSpec((1, RW_COLS), c2),
                  rowspec,
                  pl.BlockSpec((DECAY_LORA, W), c2),
                  rowspec,
                  pl.BlockSpec((AAA_LORA, W), c2),
                  pl.BlockSpec((GATE_PAD, W), c2),
                  rowspec, rowspec, rowspec, rowspec, rowspec],
        out_specs=pl.BlockSpec((NS, CHUNK, W), lambda b, c: (b, c, 0)),
        out_shape=jax.ShapeDtypeStruct((B, T, W), F32),
        scratch_shapes=[pltpu.VMEM((NS * H_RWKV // 2, HD_RWKV, 2 * HD_RWKV), F32)],
        compiler_params=_cp(("parallel", "arbitrary")),
        name="rwkv",
    )(rw, rw, mu_p, w0, wdec, a0, waaa, wgate_p, k_k, k_a, r_k, lng, lnb)


def _ffn_kernel(x_ref, ya_ref, yr_ref, mod_ref, gpost_ref, gpre_ref, gfpost_ref,
                wo_ref, wg_ref, wu_ref, wd_ref, o_ref, x1_scr, hf_scr, acc_scr):
    j = pl.program_id(2)

    @pl.when(j == 0)
    def _():
        for rows in _row_chunks(x_ref.shape[1]):
            mix = (_dot(ya_ref[0, rows, :].astype(BF16), wo_ref[0:ATT_WIDTH, :])
                   + _dot(yr_ref[0, rows, :].astype(BF16), wo_ref[ATT_WIDTH:, :]))
            x1 = x_ref[0, rows, :] + mod_ref[0, 2:3, :] * _rms(mix, gpost_ref[...])
            x1_scr[rows, :] = x1
            hf = _rms(x1, gpre_ref[...]) * (1.0 + mod_ref[0, 4:5, :]) + mod_ref[0, 3:4, :]
            hf_scr[rows, :] = hf.astype(BF16)
        acc_scr[...] = jnp.zeros(acc_scr.shape, F32)

    hf = hf_scr[...]
    gate = _dot(hf, wg_ref[...])
    up = _dot(hf, wu_ref[...])
    act = (gate * jax.nn.sigmoid(gate) * up).astype(BF16)
    acc_scr[...] += _dot(act, wd_ref[...])

    @pl.when(j == pl.num_programs(2) - 1)
    def _():
        o_ref[0] = x1_scr[...] + mod_ref[0, 5:6, :] * _rms(acc_scr[...], gfpost_ref[...])


def _ffn_call(x, y_att, y_rwkv, mod3, g_post, g_pre, g_fpost, wo, wg, wu, wd):
    B, T, D = x.shape
    TM = 512
    TF = D_FF // 2
    c3 = lambda b, i, j: (0, 0)
    return pl.pallas_call(
        _ffn_kernel,
        grid=(B, T // TM, D_FF // TF),
        in_specs=[pl.BlockSpec((1, TM, D), lambda b, i, j: (b, i, 0)),
                  pl.BlockSpec((1, TM, ATT_WIDTH), lambda b, i, j: (b, i, 0)),
                  pl.BlockSpec((1, TM, RWKV_WIDTH), lambda b, i, j: (b, i, 0)),
                  pl.BlockSpec((1, 6, D), lambda b, i, j: (b, 0, 0)),
                  pl.BlockSpec((1, D), c3), pl.BlockSpec((1, D), c3), pl.BlockSpec((1, D), c3),
                  pl.BlockSpec((D, D), c3),
                  pl.BlockSpec((D, TF), lambda b, i, j: (0, j)),
                  pl.BlockSpec((D, TF), lambda b, i, j: (0, j)),
                  pl.BlockSpec((TF, D), lambda b, i, j: (j, 0))],
        out_specs=pl.BlockSpec((1, TM, D), lambda b, i, j: (b, i, 0)),
        out_shape=jax.ShapeDtypeStruct((B, T, D), F32),
        scratch_shapes=[pltpu.VMEM((TM, D), F32), pltpu.VMEM((TM, D), BF16), pltpu.VMEM((TM, D), F32)],
        compiler_params=_cp(("parallel", "parallel", "arbitrary")),
        name="ffn",
    )(x, y_att, y_rwkv, mod3, g_post, g_pre, g_fpost, wo, wg, wu, wd)


def _pad_cols(w, n):
    return jnp.pad(w, ((0, 0), (0, n - w.shape[1])))


def _layer(x, mod3, rel_bias, mix_pre_norm, mix_post_norm, ffn_pre_norm, ffn_post_norm,
           w_in, q_norm, w_uq, w_idx_q, kv_norm, idx_k_norm, w_uk, w_uv,
           mu_shift, w0, w_decay_up, a0, w_aaa_up, w_gate_up, k_k, k_a, r_k, ln_x_gain, ln_x_bias,
           w_out, w_ffn_gate, w_ffn_up, w_ffn_down):
    B, T, D = x.shape
    att_cols = Q_LORA + KV_LATENT + IDX_DIM + IDX_HEADS
    n_main = 3 * RWKV_WIDTH + DECAY_LORA + AAA_LORA
    w_att = _pad_cols(w_in[:, :att_cols], ATT_PAD)
    w_rw = w_in[:, att_cols:]
    w_in_p = jnp.concatenate([w_att, w_rw[:, :n_main], _pad_cols(w_rw[:, n_main:], GATE_PAD)], axis=1).astype(BF16)
    mu_p = jnp.concatenate([mu_shift[:n_main], jnp.pad(mu_shift[n_main:], (0, GATE_PAD - GATE_LORA))]).reshape(1, RW_COLS)
    wgate_p = jnp.pad(w_gate_up, ((0, GATE_PAD - GATE_LORA), (0, 0))).astype(BF16)

    att, rw = _inproj_call(x, mod3, mix_pre_norm.reshape(1, D), w_in_p)

    wuq = w_uq.reshape(Q_LORA, ATT_WIDTH).astype(BF16)
    wiq = w_idx_q.reshape(Q_LORA, IDX_HEADS * IDX_DIM).astype(BF16)
    wukt = jnp.transpose(w_uk, (0, 2, 1)).astype(BF16)
    qabs, iq, ik, ckv, ckvt, iwt = _dsaprep_call(att, q_norm.reshape(1, -1), kv_norm.reshape(1, -1),
                                                 idx_k_norm.reshape(1, -1), wuq, wiq, wukt)
    bias_tiles = _bias_call(rel_bias)
    wuvt = jnp.transpose(w_uv, (0, 2, 1)).astype(BF16)
    topk = min(TOPK_MAX, T // 4)
    y_att = _dsa_call(iq, iwt, qabs, ik, ckv, ckvt, bias_tiles, wuvt, topk)

    row = lambda z: z.reshape(1, RWKV_WIDTH)
    y_rwkv = _rwkv_call(rw, mu_p, row(w0), w_decay_up.astype(BF16), row(a0), w_aaa_up.astype(BF16),
                        wgate_p, row(k_k), row(k_a), row(r_k), row(ln_x_gain), row(ln_x_bias))

    return _ffn_call(x, y_att, y_rwkv, mod3, mix_post_norm.reshape(1, D), ffn_pre_norm.reshape(1, D),
                     ffn_post_norm.reshape(1, D), w_out.astype(BF16), w_ffn_gate.astype(BF16),
                     w_ffn_up.astype(BF16), w_ffn_down.astype(BF16))


def kernel(x, c, rel_bias, ada_w, ada_b, mix_pre_norm, mix_post_norm, ffn_pre_norm, ffn_post_norm, w_in, q_norm, w_uq, w_idx_q, kv_norm, idx_k_norm, w_uk, w_uv, mu_shift, w0, w_decay_up, a0, w_aaa_up, w_gate_up, k_k, k_a, r_k, ln_x_gain, ln_x_bias, w_out, w_ffn_gate, w_ffn_up, w_ffn_down):
    B, T, D = x.shape
    assert D == D_MODEL and T % 512 == 0 and T // 4 >= 1
    layer_params = (mix_pre_norm, mix_post_norm, ffn_pre_norm, ffn_post_norm,
                    w_in, q_norm, w_uq, w_idx_q, kv_norm, idx_k_norm, w_uk, w_uv,
                    mu_shift, w0, w_decay_up, a0, w_aaa_up, w_gate_up, k_k, k_a, r_k, ln_x_gain, ln_x_bias,
                    w_out, w_ffn_gate, w_ffn_up, w_ffn_down)
    for l in range(ada_w.shape[0]):
        mod3 = _mod_call(c, ada_w[l], ada_b[l]).reshape(B, 6, D)
        x = _layer(x, mod3, rel_bias, *[p[l] for p in layer_params])
    return x
```
